```python
import jax, jax.numpy as jnp
from jax import lax
import numpy as np

D_MODEL = 2048
BATCH = 2
SEQ = 8192
DEPTH = 1

RET_HEADS = 4
RET_HEAD_DIM = D_MODEL // 8
RET_WIDTH = RET_HEADS * RET_HEAD_DIM
HGRN_HEADS = 8
HGRN_HEAD_DIM = D_MODEL // 16
HGRN_WIDTH = HGRN_HEADS * HGRN_HEAD_DIM
MIX_WIDTH = RET_WIDTH + HGRN_WIDTH
IN_COLS = 4 * RET_WIDTH + 4 * HGRN_WIDTH
D_FF = ((8 * D_MODEL // 3 + 255) // 256) * 256
RET_CHUNK = 128
HGRN_CHUNK = 64
ROPE_BASE = 10000.0
EPS = 1e-6
FFN_RESIDUAL_WEIGHT = 0.5

kernel_name = "hymba_style_retention_hgrn2_macaron"


def rmsnorm(x, g):
    x32 = x.astype(jnp.float32)
    y = x32 * lax.rsqrt(jnp.mean(x32 * x32, axis=-1, keepdims=True) + EPS)
    return (y * g.astype(jnp.float32)).astype(x.dtype)


def swiglu(h, w_gate, w_up, w_down):
    return (jax.nn.silu(h @ w_gate) * (h @ w_up)) @ w_down


def rope(x):
    d = x.shape[-1]
    s = x.shape[1]
    inv = jnp.power(ROPE_BASE, -jnp.arange(0, d, 2, dtype=jnp.float32) / d)
    ang = jnp.arange(s, dtype=jnp.float32)[:, None] * inv[None, :]
    cos = jnp.cos(ang)[None, :, None, :]
    sin = jnp.sin(ang)[None, :, None, :]
    x32 = x.astype(jnp.float32)
    x1, x2 = x32[..., : d // 2], x32[..., d // 2:]
    return jnp.concatenate([x1 * cos - x2 * sin, x2 * cos + x1 * sin], axis=-1)


def retention_chunkwise(q, k, v):
    b, s, h, dk = q.shape
    dv = v.shape[-1]
    c = RET_CHUNK
    n = s // c
    log_gamma = jnp.log(1.0 - jnp.exp2(-5.0 - jnp.arange(h, dtype=jnp.float32)))
    q = q.astype(jnp.float32).reshape(b, n, c, h, dk)
    k = k.astype(jnp.float32).reshape(b, n, c, h, dk)
    v = v.astype(jnp.float32).reshape(b, n, c, h, dv)
    idx = jnp.arange(c, dtype=jnp.float32)
    rel = idx[:, None] - idx[None, :]
    mask = rel >= 0
    decay = jnp.where(mask[None], jnp.exp(log_gamma[:, None, None] * jnp.where(mask, rel, 0.0)[None]), 0.0)
    scores = jnp.einsum('bnihd,bnjhd->bnhij', q, k) * decay[None, None]
    inner = jnp.einsum('bnhij,bnjhe->bnihe', scores, v)
    k_dec = k * jnp.exp(log_gamma[None, :] * (c - 1.0 - idx)[:, None])[None, None, :, :, None]
    kv = jnp.einsum('bnjhd,bnjhe->nbhde', k_dec, v)
    g_chunk = jnp.exp(log_gamma * c)[None, :, None, None]

    def step(state, kv_n):
        return g_chunk * state + kv_n, state

    _, r_prev = lax.scan(step, jnp.zeros((b, h, dk, dv), jnp.float32), kv)
    q_dec = q * jnp.exp(log_gamma[None, :] * (idx + 1.0)[:, None])[None, None, :, :, None]
    cross = jnp.einsum('bnihd,nbhde->bnihe', q_dec, r_prev)
    return (inner + cross).reshape(b, s, h, dv)


def hgrn2_chunkwise(q, k, v, log_f):
    b, s, h, dk = q.shape
    dv = v.shape[-1]
    c = HGRN_CHUNK
    n = s // c

    def to_chunks(t):
        return t.astype(jnp.float32).reshape(b, n, c, h, t.shape[-1]).transpose(1, 0, 3, 2, 4)

    causal = jnp.tril(jnp.ones((c, c), dtype=bool))

    def step(state, inp):
        q_c, k_c, v_c, lf_c = inp
        cum = jnp.cumsum(lf_c, axis=-2)
        diff = cum[:, :, :, None, :] - cum[:, :, None, :, :]
        pair_decay = jnp.exp(jnp.where(causal[:, :, None], diff, -jnp.inf))
        attn = jnp.einsum('bhtd,bhjd,bhtjd->bhtj', q_c, k_c, pair_decay)
        o = jnp.einsum('bhtj,bhje->bhte', attn, v_c) + jnp.einsum('bhtd,bhde->bhte', q_c * jnp.exp(cum), state)
        last = cum[:, :, -1:, :]
        state = jnp.exp(last[:, :, 0, :])[..., None] * state + jnp.einsum('bhjd,bhje->bhde', k_c * jnp.exp(last - cum), v_c)
        return state, o

    _, o = lax.scan(step, jnp.zeros((b, h, dk, dv), jnp.float32),
                    (to_chunks(q), to_chunks(k), to_chunks(v), to_chunks(log_f)))
    return o.transpose(1, 0, 3, 2, 4).reshape(b, s, h, dv)


def hgrn_lower_bounds(lb_logits):
    logits = jnp.concatenate([lb_logits.astype(jnp.float32), jnp.zeros((1, lb_logits.shape[-1]), jnp.float32)], axis=0)
    return jnp.cumsum(jax.nn.softmax(logits, axis=0), axis=0)[:DEPTH]


def hybrid_mixer(h, w_in, ret_norm_g, lb, hgrn_norm_g, w_out):
    b, s, _ = h.shape
    proj = h @ w_in
    splits = np.cumsum([RET_WIDTH] * 4 + [HGRN_WIDTH] * 3).tolist()
    rq, rk, rv, rg, hq, hf, hi, hg = jnp.split(proj, splits, axis=-1)
    rq = rope(rq.reshape(b, s, RET_HEADS, RET_HEAD_DIM)) * (RET_HEAD_DIM ** -0.5)
    rk = rope(rk.reshape(b, s, RET_HEADS, RET_HEAD_DIM))
    ret = retention_chunkwise(rq, rk, rv.reshape(b, s, RET_HEADS, RET_HEAD_DIM))
    mu = jnp.mean(ret, axis=-1, keepdims=True)
    var = jnp.mean(jnp.square(ret - mu), axis=-1, keepdims=True)
    ret = ((ret - mu) * lax.rsqrt(var + EPS)).reshape(b, s, RET_WIDTH)
    ret = ret * ret_norm_g.astype(jnp.float32) * jax.nn.silu(rg.astype(jnp.float32))
    z = hf.astype(jnp.float32).reshape(b, s, HGRN_HEADS, HGRN_HEAD_DIM)
    lbh = lb.reshape(HGRN_HEADS, HGRN_HEAD_DIM)
    f = lbh + (1.0 - lbh) * jax.nn.sigmoid(z)
    key = (1.0 - lbh) * jax.nn.sigmoid(-z)
    hq_act = jax.nn.silu(hq.astype(jnp.float32)).reshape(b, s, HGRN_HEADS, HGRN_HEAD_DIM)
    hv = hi.reshape(b, s, HGRN_HEADS, HGRN_HEAD_DIM)
    hg_out = hgrn2_chunkwise(hq_act, key, hv, jnp.log(f))
    hg_out = (hg_out * lax.rsqrt(jnp.mean(hg_out * hg_out, axis=-1, keepdims=True) + EPS)).reshape(b, s, HGRN_WIDTH)
    hg_out = hg_out * hgrn_norm_g.astype(jnp.float32) * jax.nn.silu(hg.astype(jnp.float32))
    merged = jnp.concatenate([ret, hg_out], axis=-1).astype(h.dtype)
    return merged @ w_out


def setup_inputs(seed: int = 0) -> dict:
    key = jax.random.key(seed)
    ks = jax.random.split(key, 16)
    f32 = jnp.float32

    def w(k, shape, fan_in):
        return jax.random.normal(k, shape, f32) * (fan_in ** -0.5)

    def gain(k, shape):
        return 1.0 + 0.02 * jax.random.normal(k, shape, f32)

    return {
        "x": jax.random.normal(ks[0], (BATCH, SEQ, D_MODEL), f32),
        "ffn1_norm": gain(ks[1], (DEPTH, D_MODEL)),
        "ffn1_w_gate": w(ks[2], (DEPTH, D_MODEL, D_FF), D_MODEL),
        "ffn1_w_up": w(ks[3], (DEPTH, D_MODEL, D_FF), D_MODEL),
        "ffn1_w_down": w(ks[4], (DEPTH, D_FF, D_MODEL), D_FF),
        "mix_norm": gain(ks[5], (DEPTH, D_MODEL)),
        "w_in": w(ks[6], (DEPTH, D_MODEL, IN_COLS), D_MODEL),
        "ret_norm_g": gain(ks[7], (DEPTH, RET_WIDTH)),
        "hgrn_lb_logits": 0.5 * jax.random.normal(ks[8], (DEPTH, HGRN_WIDTH), f32),
        "hgrn_norm_g": gain(ks[9], (DEPTH, HGRN_WIDTH)),
        "w_out": w(ks[10], (DEPTH, MIX_WIDTH, D_MODEL), MIX_WIDTH),
        "ffn2_norm": gain(ks[11], (DEPTH, D_MODEL)),
        "ffn2_w_gate": w(ks[12], (DEPTH, D_MODEL, D_FF), D_MODEL),
        "ffn2_w_up": w(ks[13], (DEPTH, D_MODEL, D_FF), D_MODEL),
        "ffn2_w_down": w(ks[14], (DEPTH, D_FF, D_MODEL), D_FF),
        "final_norm": gain(ks[15], (D_MODEL,)),
    }


def reference(x, ffn1_norm, ffn1_w_gate, ffn1_w_up, ffn1_w_down, mix_norm, w_in, ret_norm_g,
              hgrn_lb_logits, hgrn_norm_g, w_out, ffn2_norm, ffn2_w_gate, ffn2_w_up, ffn2_w_down, final_norm):
    lbs = hgrn_lower_bounds(hgrn_lb_logits)
    for l in range(DEPTH):
        y = swiglu(rmsnorm(x, ffn1_norm[l]), ffn1_w_gate[l], ffn1_w_up[l], ffn1_w_down[l])
        x = x + (FFN_RESIDUAL_WEIGHT * y).astype(x.dtype)
        y = hybrid_mixer(rmsnorm(x, mix_norm[l]), w_in[l], ret_norm_g[l], lbs[l], hgrn_norm_g[l], w_out[l])
        x = x + y.astype(x.dtype)
        y = swiglu(rmsnorm(x, ffn2_norm[l]), ffn2_w_gate[l], ffn2_w_up[l], ffn2_w_down[l])
        x = x + (FFN_RESIDUAL_WEIGHT * y).astype(x.dtype)
    return rmsnorm(x, final_norm)
```

```python
import functools
import math

import jax
import jax.numpy as jnp
from jax import lax
from jax.experimental import pallas as pl
from jax.experimental.pallas import tpu as pltpu

F32 = jnp.float32
BF16 = jnp.bfloat16

RET_HEADS = 4
RET_HEAD_DIM = 256
HGRN_HEADS = 8
HGRN_HEAD_DIM = 128
ROPE_BASE = 10000.0
EPS = 1e-6
FFN_RESIDUAL_WEIGHT = 0.5

VMEM_LIMIT_BYTES = 56 * 1024 * 1024

FFN_TM = 512
FFN_TF = 512
INPROJ_TM = 512
RET_TB = 256
RET_CHUNK = 128
HGRN_TB = 512
HGRN_CHUNK = 64
HGRN_SUB = 16
NEG_BIG = -1e30


def _dot(a, b):
    return jnp.dot(a, b, preferred_element_type=F32)


def _dot_nt(a, b):
    return lax.dot_general(a, b, (((1,), (1,)), ((), ())), preferred_element_type=F32)


def _dot_tn(a, b):
    return lax.dot_general(a, b, (((0,), (0,)), ((), ())), preferred_element_type=F32)


def _rmsnorm(x, g):
    return x * lax.rsqrt(jnp.mean(x * x, axis=-1, keepdims=True) + EPS) * g


def _silu(x):
    return x * jax.nn.sigmoid(x)


def _ffn_kernel(*refs, pre_proj, final_norm, n_ff_steps):
    refs = list(refs)
    x_ref = refs.pop(0)
    if pre_proj:
        ret_ref, hgo_ref, wo_r_ref, wo_h_ref = refs[:4]
        refs = refs[4:]
    g_ref, wg_ref, wu_ref, wd_ref = refs[:4]
    refs = refs[4:]
    if final_norm:
        fg_ref = refs.pop(0)
    out_ref, h_ref = refs

    j = pl.program_id(1)

    @pl.when(j == 0)
    def _():
        x = x_ref[...]
        if pre_proj:
            x = x + _dot(ret_ref[...], wo_r_ref[...]) + _dot(hgo_ref[...], wo_h_ref[...])
        out_ref[...] = x
        h_ref[...] = _rmsnorm(x, g_ref[...]).astype(BF16)

    h = h_ref[...]
    gate = _dot(h, wg_ref[...])
    up = _dot(h, wu_ref[...])
    act = (_silu(gate) * up * FFN_RESIDUAL_WEIGHT).astype(BF16)
    out_ref[...] += _dot(act, wd_ref[...])

    if final_norm:
        @pl.when(j == n_ff_steps - 1)
        def _():
            out_ref[...] = _rmsnorm(out_ref[...], fg_ref[...])


def _ffn(x, norm_g, w_gate, w_up, w_down, *, pre=None, final_g=None):
    t, d = x.shape
    d_ff = w_gate.shape[1]
    tm, tf = FFN_TM, FFN_TF
    assert t % tm == 0 and d_ff % tf == 0
    n_ff_steps = d_ff // tf
    row = lambda i, j: (i, 0)
    const = lambda i, j: (0, 0)
    in_specs = [pl.BlockSpec((tm, d), row)]
    args = [x]
    if pre is not None:
        ret, hgo, wo_r, wo_h = pre
        in_specs += [pl.BlockSpec((tm, ret.shape[1]), row), pl.BlockSpec((tm, hgo.shape[1]), row),
                     pl.BlockSpec(wo_r.shape, const), pl.BlockSpec(wo_h.shape, const)]
        args += [ret, hgo, wo_r, wo_h]
    in_specs += [pl.BlockSpec((1, d), const),
                 pl.BlockSpec((d, tf), lambda i, j: (0, j)),
                 pl.BlockSpec((d, tf), lambda i, j: (0, j)),
                 pl.BlockSpec((tf, d), lambda i, j: (j, 0))]
    args += [norm_g.reshape(1, d), w_gate, w_up, w_down]
    if final_g is not None:
        in_specs.append(pl.BlockSpec((1, d), const))
        args.append(final_g.reshape(1, d))
    kern = functools.partial(_ffn_kernel, pre_proj=pre is not None, final_norm=final_g is not None,
                             n_ff_steps=n_ff_steps)
    return pl.pallas_call(
        kern,
        grid=(t // tm, n_ff_steps),
        in_specs=in_specs,
        out_specs=pl.BlockSpec((tm, d), row),
        out_shape=jax.ShapeDtypeStruct((t, d), F32),
        scratch_shapes=[pltpu.VMEM((tm, d), BF16)],
        compiler_params=pltpu.CompilerParams(
            dimension_semantics=("parallel", "arbitrary"), vmem_limit_bytes=VMEM_LIMIT_BYTES),
        name="ffn2" if pre is not None else "ffn1",
    )(*args)


def _inproj_kernel(x_ref, g_ref, cos_ref, sin_ref, lbl_ref,
                   w_rq, w_rk, w_rv, w_rg, w_hq, w_hf, w_hi, w_hg,
                   rq_ref, rk_ref, rv_ref, rg_ref, hq_ref, hk_ref, hlf_ref, hi_ref, hg_ref,
                   h_ref):
    @pl.when(pl.program_id(1) == 0)
    def _():
        h_ref[...] = _rmsnorm(x_ref[...], g_ref[...]).astype(BF16)

    h = h_ref[...]
    cos = cos_ref[...]
    sin = sin_ref[...]
    half = RET_HEAD_DIM // 2

    def rope(p):
        x1, x2 = p[:, :half], p[:, half:]
        return jnp.concatenate([x1 * cos - x2 * sin, x2 * cos + x1 * sin], axis=-1)

    rq_ref[...] = (rope(_dot(h, w_rq[...])) * (RET_HEAD_DIM ** -0.5)).astype(BF16)
    rk_ref[...] = rope(_dot(h, w_rk[...])).astype(BF16)
    rv_ref[...] = _dot(h, w_rv[...]).astype(BF16)
    rg_ref[...] = _silu(_dot(h, w_rg[...])).astype(BF16)
    hq_ref[...] = _silu(_dot(h, w_hq[...])).astype(BF16)

    lbl = lbl_ref[...]
    m = jnp.maximum(lbl, 0.0)
    e_l = jnp.exp(lbl - m)
    lb = e_l / (e_l + jnp.exp(-m))
    z = _dot(h, w_hf[...])
    f = lb + (1.0 - lb) * jax.nn.sigmoid(z)
    hk_ref[...] = ((1.0 - lb) * jax.nn.sigmoid(-z)).astype(BF16)
    hlf_ref[...] = jnp.log(f)
    hi_ref[...] = _dot(h, w_hi[...]).astype(BF16)
    hg_ref[...] = _silu(_dot(h, w_hg[...])).astype(BF16)


def _inproj(x1, norm_g, w_in, lb_logits, cos, sin, seq):
    t, d = x1.shape
    tm = INPROJ_TM
    tn = RET_HEAD_DIM
    width = RET_HEADS * RET_HEAD_DIM
    assert w_in.shape[1] == 8 * width and width == HGRN_HEADS * HGRN_HEAD_DIM
    assert t % tm == 0 and seq % tm == 0
    nj = width // tn
    pos_blocks = seq // tm
    row = lambda i, j: (i, 0)
    const = lambda i, j: (0, 0)

    def w_spec(group):
        return pl.BlockSpec((d, tn), lambda i, j, g=group: (0, g * nj + j))

    out_spec = pl.BlockSpec((tm, tn), lambda i, j: (i, j))
    out_bf = jax.ShapeDtypeStruct((t, width), BF16)
    out_f32 = jax.ShapeDtypeStruct((t, width), F32)
    return pl.pallas_call(
        _inproj_kernel,
        grid=(t // tm, nj),
        in_specs=[pl.BlockSpec((tm, d), row), pl.BlockSpec((1, d), const),
                  pl.BlockSpec((tm, tn // 2), lambda i, j: (i % pos_blocks, 0)),
                  pl.BlockSpec((tm, tn // 2), lambda i, j: (i % pos_blocks, 0)),
                  pl.BlockSpec((1, tn), lambda i, j: (0, j))] + [w_spec(g) for g in range(8)],
        out_specs=[out_spec] * 9,
        out_shape=[out_bf, out_bf, out_bf, out_bf, out_bf, out_bf, out_f32, out_bf, out_bf],
        scratch_shapes=[pltpu.VMEM((tm, d), BF16)],
        compiler_params=pltpu.CompilerParams(
            dimension_semantics=("parallel", "arbitrary"), vmem_limit_bytes=VMEM_LIMIT_BYTES),
        name="inproj",
    )(x1, norm_g.reshape(1, d), cos, sin, lb_logits, *([w_in] * 8))


def _ret_kernel(q_ref, k_ref, v_ref, gate_ref, gain_ref, out_ref, state_ref, *, tb, c):
    @pl.when(pl.program_id(1) == 0)
    def _():
        state_ref[...] = jnp.zeros_like(state_ref)

    dk = RET_HEAD_DIM
    row_cc = lax.broadcasted_iota(jnp.int32, (c, c), 0)
    col_cc = lax.broadcasted_iota(jnp.int32, (c, c), 1)
    rel = (row_cc - col_cc).astype(F32)
    row_cd = lax.broadcasted_iota(jnp.int32, (c, dk), 0).astype(F32)

    for h in range(RET_HEADS):
        lg = math.log(1.0 - 2.0 ** (-5.0 - h))
        decay = jnp.where(rel >= 0, jnp.exp(lg * jnp.maximum(rel, 0.0)), 0.0)
        q_dec = jnp.exp(lg * (row_cd + 1.0))
        k_dec = jnp.exp(lg * (c - 1.0 - row_cd))
        g_chunk = math.exp(lg * c)
        cols = slice(h * dk, (h + 1) * dk)
        gain = gain_ref[:, cols]
        for ci in range(tb // c):
            rows = slice(ci * c, (ci + 1) * c)
            q = q_ref[rows, cols]
            k = k_ref[rows, cols]
            v = v_ref[rows, cols]
            state = state_ref[h]
            scores = _dot_nt(q, k) * decay
            o = _dot(scores.astype(BF16), v) + _dot(q, state.astype(BF16)) * q_dec
            k_scaled = (k.astype(F32) * k_dec).astype(BF16)
            state_ref[h] = g_chunk * state + _dot_tn(k_scaled, v)
            mu = jnp.mean(o, axis=-1, keepdims=True)
            oc = o - mu
            var = jnp.mean(oc * oc, axis=-1, keepdims=True)
            y = oc * lax.rsqrt(var + EPS) * gain * gate_ref[rows, cols].astype(F32)
            out_ref[rows, cols] = y.astype(BF16)


def _retention(rq, rk, rv, rg, gain, batch, seq):
    t, width = rq.shape
    tb, c = RET_TB, RET_CHUNK
    assert seq % tb == 0 and tb % c == 0
    nblk = seq // tb
    blk = pl.BlockSpec((tb, width), lambda b, n: (b * nblk + n, 0))
    return pl.pallas_call(
        functools.partial(_ret_kernel, tb=tb, c=c),
        grid=(batch, nblk),
        in_specs=[blk, blk, blk, blk, pl.BlockSpec((1, width), lambda b, n: (0, 0))],
        out_specs=blk,
        out_shape=jax.ShapeDtypeStruct((t, width), BF16),
        scratch_shapes=[pltpu.VMEM((RET_HEADS, RET_HEAD_DIM, RET_HEAD_DIM), F32)],
        compiler_params=pltpu.CompilerParams(
            dimension_semantics=("parallel", "arbitrary"), vmem_limit_bytes=VMEM_LIMIT_BYTES),
        name="retention",
    )(rq, rk, rv, rg, gain.reshape(1, width))


def _hgrn_kernel(q_ref, k_ref, v_ref, lf_ref, gate_ref, gain_ref, out_ref,
                 state_ref, cum_ref, kf_ref, vf_ref, o_ref, *, tb, c):
    @pl.when(pl.program_id(2) == 0)
    def _():
        state_ref[...] = jnp.zeros_like(state_ref)

    sub = HGRN_SUB
    d = HGRN_HEAD_DIM
    tri = (lax.broadcasted_iota(jnp.int32, (c, c), 0) >=
           lax.broadcasted_iota(jnp.int32, (c, c), 1)).astype(BF16)
    sub_row = lax.broadcasted_iota(jnp.int32, (sub, d), 0)
    gain = gain_ref[...]

    def chunk(ci, carry):
        rows = pl.ds(pl.multiple_of(ci * c, c), c)
        lf = lf_ref[rows, :]
        lf1 = lf.astype(BF16)
        r1 = lf - lf1.astype(F32)
        lf2 = r1.astype(BF16)
        lf3 = (r1 - lf2.astype(F32)).astype(BF16)
        cum = _dot(tri, lf1) + _dot(tri, lf2) + _dot(tri, lf3)
        total = cum[c - 1:c, :]

        q = q_ref[rows, :].astype(F32)
        k = k_ref[rows, :].astype(F32)
        v = v_ref[rows, :]
        state = state_ref[...]

        o_ref[...] = _dot_nt((q * jnp.exp(cum)).astype(BF16), state.astype(BF16))
        k_end = (k * jnp.exp(total - cum)).astype(BF16)
        state_ref[...] = jnp.exp(total) * state + _dot_tn(v, k_end)

        cum_ref[...] = cum
        kf_ref[...] = k
        vf_ref[...] = v.astype(F32)

        def diag(lo):
            q_i = q[lo:lo + sub]
            c_i = cum[lo:lo + sub]
            acc = jnp.zeros((sub, d), F32)
            for j in range(sub):
                r = lo + j
                arg = jnp.where(sub_row >= j, c_i - cum_ref[r:r + 1, :], NEG_BIG)
                p = q_i * kf_ref[r:r + 1, :] * jnp.exp(arg)
                acc = acc + jnp.sum(p, axis=-1, keepdims=True) * vf_ref[r:r + 1, :]
            o_ref[lo:lo + sub, :] += acc

        def intra(lo, hi):
            if hi - lo == sub:
                diag(lo)
                return
            mid = (lo + hi) // 2
            c_b = cum[mid - 1:mid, :]
            q_a = (q[mid:hi] * jnp.exp(cum[mid:hi] - c_b)).astype(BF16)
            k_a = (k[lo:mid] * jnp.exp(c_b - cum[lo:mid])).astype(BF16)
            attn = _dot_nt(q_a, k_a)
            o_ref[mid:hi, :] += _dot(attn.astype(BF16), v[lo:mid])
            intra(lo, mid)
            intra(mid, hi)

        intra(0, c)

        o = o_ref[...]
        y = o * lax.rsqrt(jnp.mean(o * o, axis=-1, keepdims=True) + EPS)
        out_ref[rows, :] = (y * gain * gate_ref[rows, :].astype(F32)).astype(BF16)
        return carry

    lax.fori_loop(0, tb // c, chunk, 0)


def _hgrn(hq, hk, hv, hlf, hg, gain, batch, seq):
    t, width = hq.shape
    tb, c, d = HGRN_TB, HGRN_CHUNK, HGRN_HEAD_DIM
    assert seq % tb == 0 and tb % c == 0 and c % HGRN_SUB == 0
    nblk = seq // tb
    blk = pl.BlockSpec((tb, d), lambda b, h, n: (b * nblk + n, h))
    return pl.pallas_call(
        functools.partial(_hgrn_kernel, tb=tb, c=c),
        grid=(batch, width // d, nblk),
        in_specs=[blk, blk, blk, blk, blk, pl.BlockSpec((1, d), lambda b, h, n: (0, h))],
        out_specs=blk,
        out_shape=jax.ShapeDtypeStruct((t, width), BF16),
        scratch_shapes=[pltpu.VMEM((d, d), F32), pltpu.VMEM((c, d), F32), pltpu.VMEM((c, d), F32),
                        pltpu.VMEM((c, d), F32), pltpu.VMEM((c, d), F32)],
        compiler_params=pltpu.CompilerParams(
            dimension_semantics=("parallel", "parallel", "arbitrary"),
            vmem_limit_bytes=VMEM_LIMIT_BYTES),
        name="hgrn2",
    )(hq, hk, hv, hlf, hg, gain.reshape(1, width))


@jax.jit
def kernel(x, ffn1_norm, ffn1_w_gate, ffn1_w_up, ffn1_w_down, mix_norm, w_in, ret_norm_g, hgrn_lb_logits, hgrn_norm_g, w_out, ffn2_norm, ffn2_w_gate, ffn2_w_up, ffn2_w_down, final_norm):
    batch, seq, d = x.shape
    assert ffn1_norm.shape[0] == 1, "single-layer stack"
    t = batch * seq
    ret_width = RET_HEADS * RET_HEAD_DIM
    bf = lambda w: w[0].astype(BF16)

    inv = jnp.power(ROPE_BASE, -jnp.arange(0, RET_HEAD_DIM, 2, dtype=F32) / RET_HEAD_DIM)
    ang = jnp.arange(seq, dtype=F32)[:, None] * inv[None, :]
    cos, sin = jnp.cos(ang), jnp.sin(ang)

    x0 = x.reshape(t, d)
    x1 = _ffn(x0, ffn1_norm[0], bf(ffn1_w_gate), bf(ffn1_w_up), bf(ffn1_w_down))
    rq, rk, rv, rg, hq, hk, hlf, hi, hg = _inproj(
        x1, mix_norm[0], bf(w_in), hgrn_lb_logits.astype(F32), cos, sin, seq)
    ret = _retention(rq, rk, rv, rg, ret_norm_g[0].astype(F32), batch, seq)
    hgo = _hgrn(hq, hk, hi, hlf, hg, hgrn_norm_g[0].astype(F32), batch, seq)
    wo = bf(w_out)
    out = _ffn(x1, ffn2_norm[0], bf(ffn2_w_gate), bf(ffn2_w_up), bf(ffn2_w_down),
               pre=(ret, hgo, wo[:ret_width], wo[ret_width:]), final_g=final_norm)
    return out.reshape(batch, seq, d)
```

```python
import functools
import math

import jax
import jax.numpy as jnp
from jax import lax
from jax.experimental import pallas as pl
from jax.experimental.pallas import tpu as pltpu

F32 = jnp.float32
BF16 = jnp.bfloat16

RET_HEADS = 4
RET_HEAD_DIM = 256
HGRN_HEADS = 8
HGRN_HEAD_DIM = 128
ROPE_BASE = 10000.0
EPS = 1e-6
FFN_RESIDUAL_WEIGHT = 0.5

VMEM_LIMIT_BYTES = 56 * 1024 * 1024
BF16_TILE_ROWS = 16

FFN1_TM = 1024
FFN2_TM = 512
FFN_TF = 512
INPROJ_TM = 512
RET_TB = 256
RET_CHUNK = 128
HGRN_TB = 512
HGRN_CHUNK = 64
HGRN_SUB = 8
NEG_BIG = -1e30


def _dot(a, b):
    return jnp.dot(a, b, preferred_element_type=F32)


def _dot_nt(a, b):
    return lax.dot_general(a, b, (((1,), (1,)), ((), ())), preferred_element_type=F32)


def _dot_tn(a, b):
    return lax.dot_general(a, b, (((0,), (0,)), ((), ())), preferred_element_type=F32)


def _rmsnorm(x, g):
    return x * lax.rsqrt(jnp.mean(x * x, axis=-1, keepdims=True) + EPS) * g


def _silu(x):
    return x * jax.nn.sigmoid(x)


def _ffn_kernel(*refs, pre_proj, final_norm, n_ff_steps):
    refs = list(refs)
    x_ref = refs.pop(0)
    if pre_proj:
        ret_ref, hgo_ref, wo_r_ref, wo_h_ref = refs[:4]
        refs = refs[4:]
    g_ref, wg_ref, wu_ref, wd_ref = refs[:4]
    refs = refs[4:]
    if final_norm:
        fg_ref = refs.pop(0)
    out_ref, h_ref = refs

    j = pl.program_id(1)

    @pl.when(j == 0)
    def _():
        x = x_ref[...]
        if pre_proj:
            x = x + _dot(ret_ref[...], wo_r_ref[...]) + _dot(hgo_ref[...], wo_h_ref[...])
        out_ref[...] = x
        h_ref[...] = _rmsnorm(x, g_ref[...]).astype(BF16)

    h = h_ref[...]
    gate = _dot(h, wg_ref[...])
    up = _dot(h, wu_ref[...])
    act = (_silu(gate) * up * FFN_RESIDUAL_WEIGHT).astype(BF16)
    out_ref[...] += _dot(act, wd_ref[...])

    if final_norm:
        @pl.when(j == n_ff_steps - 1)
        def _():
            out_ref[...] = _rmsnorm(out_ref[...], fg_ref[...])


def _ffn(x, norm_g, w_gate, w_up, w_down, *, tm, pre=None, final_g=None):
    t, d = x.shape
    d_ff = w_gate.shape[1]
    tf = FFN_TF
    assert t % tm == 0 and d_ff % tf == 0
    n_ff_steps = d_ff // tf
    row = lambda i, j: (i, 0)
    const = lambda i, j: (0, 0)
    in_specs = [pl.BlockSpec((tm, d), row)]
    args = [x]
    if pre is not None:
        ret, hgo, wo_r, wo_h = pre
        in_specs += [pl.BlockSpec((tm, ret.shape[1]), row), pl.BlockSpec((tm, hgo.shape[1]), row),
                     pl.BlockSpec(wo_r.shape, const), pl.BlockSpec(wo_h.shape, const)]
        args += [ret, hgo, wo_r, wo_h]
    in_specs += [pl.BlockSpec((1, d), const),
                 pl.BlockSpec((d, tf), lambda i, j: (0, j)),
                 pl.BlockSpec((d, tf), lambda i, j: (0, j)),
                 pl.BlockSpec((tf, d), lambda i, j: (j, 0))]
    args += [norm_g.reshape(1, d), w_gate, w_up, w_down]
    if final_g is not None:
        in_specs.append(pl.BlockSpec((1, d), const))
        args.append(final_g.reshape(1, d))
    kern = functools.partial(_ffn_kernel, pre_proj=pre is not None, final_norm=final_g is not None,
                             n_ff_steps=n_ff_steps)
    return pl.pallas_call(
        kern,
        grid=(t // tm, n_ff_steps),
        in_specs=in_specs,
        out_specs=pl.BlockSpec((tm, d), row),
        out_shape=jax.ShapeDtypeStruct((t, d), F32),
        scratch_shapes=[pltpu.VMEM((tm, d), BF16)],
        compiler_params=pltpu.CompilerParams(
            dimension_semantics=("parallel", "arbitrary"), vmem_limit_bytes=VMEM_LIMIT_BYTES),
        name="ffn2" if pre is not None else "ffn1",
    )(*args)


def _inproj_kernel(x_ref, g_ref, cos_ref, sin_ref, lbl_ref,
                   w_rq, w_rk, w_rv, w_rg, w_hq, w_hf, w_hi, w_hg,
                   rq_ref, rk_ref, rv_ref, rg_ref, hq_ref, hk_ref, hlf_ref, hi_ref, hg_ref,
                   h_ref):
    @pl.when(pl.program_id(1) == 0)
    def _():
        h_ref[...] = _rmsnorm(x_ref[...], g_ref[...]).astype(BF16)

    h = h_ref[...]
    cos = cos_ref[...]
    sin = sin_ref[...]
    half = RET_HEAD_DIM // 2

    def rope(p):
        x1, x2 = p[:, :half], p[:, half:]
        return jnp.concatenate([x1 * cos - x2 * sin, x2 * cos + x1 * sin], axis=-1)

    rq_ref[...] = (rope(_dot(h, w_rq[...])) * (RET_HEAD_DIM ** -0.5)).astype(BF16)
    rk_ref[...] = rope(_dot(h, w_rk[...])).astype(BF16)
    rv_ref[...] = _dot(h, w_rv[...]).astype(BF16)
    rg_ref[...] = _silu(_dot(h, w_rg[...])).astype(BF16)
    hq_ref[...] = _silu(_dot(h, w_hq[...])).astype(BF16)

    lbl = lbl_ref[...]
    m = jnp.maximum(lbl, 0.0)
    e_l = jnp.exp(lbl - m)
    lb = e_l / (e_l + jnp.exp(-m))
    z = _dot(h, w_hf[...])
    f = lb + (1.0 - lb) * jax.nn.sigmoid(z)
    hk_ref[...] = ((1.0 - lb) * jax.nn.sigmoid(-z)).astype(BF16)
    hlf_ref[...] = jnp.log2(f)
    hi_ref[...] = _dot(h, w_hi[...]).astype(BF16)
    hg_ref[...] = _silu(_dot(h, w_hg[...])).astype(BF16)


def _inproj(x1, norm_g, w_in, lb_logits, cos, sin, seq):
    t, d = x1.shape
    tm = INPROJ_TM
    tn = RET_HEAD_DIM
    width = RET_HEADS * RET_HEAD_DIM
    assert w_in.shape[1] == 8 * width and width == HGRN_HEADS * HGRN_HEAD_DIM
    assert t % tm == 0 and seq % tm == 0
    nj = width // tn
    pos_blocks = seq // tm
    row = lambda i, j: (i, 0)
    const = lambda i, j: (0, 0)

    def w_spec(group):
        return pl.BlockSpec((d, tn), lambda i, j, g=group: (0, g * nj + j))

    out_spec = pl.BlockSpec((tm, tn), lambda i, j: (i, j))
    out_bf = jax.ShapeDtypeStruct((t, width), BF16)
    out_f32 = jax.ShapeDtypeStruct((t, width), F32)
    return pl.pallas_call(
        _inproj_kernel,
        grid=(t // tm, nj),
        in_specs=[pl.BlockSpec((tm, d), row), pl.BlockSpec((1, d), const),
                  pl.BlockSpec((tm, tn // 2), lambda i, j: (i % pos_blocks, 0)),
                  pl.BlockSpec((tm, tn // 2), lambda i, j: (i % pos_blocks, 0)),
                  pl.BlockSpec((1, tn), lambda i, j: (0, j))] + [w_spec(g) for g in range(8)],
        out_specs=[out_spec] * 9,
        out_shape=[out_bf, out_bf, out_bf, out_bf, out_bf, out_bf, out_f32, out_bf, out_bf],
        scratch_shapes=[pltpu.VMEM((tm, d), BF16)],
        compiler_params=pltpu.CompilerParams(
            dimension_semantics=("parallel", "arbitrary"), vmem_limit_bytes=VMEM_LIMIT_BYTES),
        name="inproj",
    )(x1, norm_g.reshape(1, d), cos, sin, lb_logits, *([w_in] * 8))


def _ret_kernel(q_ref, k_ref, v_ref, gate_ref, gain_ref, out_ref, state_ref, *, tb, c):
    @pl.when(pl.program_id(1) == 0)
    def _():
        state_ref[...] = jnp.zeros_like(state_ref)

    dk = RET_HEAD_DIM
    row_cc = lax.broadcasted_iota(jnp.int32, (c, c), 0)
    col_cc = lax.broadcasted_iota(jnp.int32, (c, c), 1)
    rel = (row_cc - col_cc).astype(F32)
    row_cd = lax.broadcasted_iota(jnp.int32, (c, dk), 0).astype(F32)

    for h in range(RET_HEADS):
        lg = math.log(1.0 - 2.0 ** (-5.0 - h))
        decay = jnp.where(rel >= 0, jnp.exp(lg * jnp.maximum(rel, 0.0)), 0.0)
        q_dec = jnp.exp(lg * (row_cd + 1.0))
        k_dec = jnp.exp(lg * (c - 1.0 - row_cd))
        g_chunk = math.exp(lg * c)
        cols = slice(h * dk, (h + 1) * dk)
        gain = gain_ref[:, cols]
        for ci in range(tb // c):
            rows = slice(ci * c, (ci + 1) * c)
            q = q_ref[rows, cols]
            k = k_ref[rows, cols]
            v = v_ref[rows, cols]
            state = state_ref[h]
            scores = _dot_nt(q, k) * decay
            o = _dot(scores.astype(BF16), v) + _dot(q, state.astype(BF16)) * q_dec
            k_scaled = (k.astype(F32) * k_dec).astype(BF16)
            state_ref[h] = g_chunk * state + _dot_tn(k_scaled, v)
            mu = jnp.mean(o, axis=-1, keepdims=True)
            oc = o - mu
            var = jnp.mean(oc * oc, axis=-1, keepdims=True)
            y = oc * lax.rsqrt(var + EPS) * gain * gate_ref[rows, cols].astype(F32)
            out_ref[rows, cols] = y.astype(BF16)


def _retention(rq, rk, rv, rg, gain, batch, seq):
    t, width = rq.shape
    tb, c = RET_TB, RET_CHUNK
    assert seq % tb == 0 and tb % c == 0
    nblk = seq // tb
    blk = pl.BlockSpec((tb, width), lambda b, n: (b * nblk + n, 0))
    return pl.pallas_call(
        functools.partial(_ret_kernel, tb=tb, c=c),
        grid=(batch, nblk),
        in_specs=[blk, blk, blk, blk, pl.BlockSpec((1, width), lambda b, n: (0, 0))],
        out_specs=blk,
        out_shape=jax.ShapeDtypeStruct((t, width), BF16),
        scratch_shapes=[pltpu.VMEM((RET_HEADS, RET_HEAD_DIM, RET_HEAD_DIM), F32)],
        compiler_params=pltpu.CompilerParams(
            dimension_semantics=("parallel", "arbitrary"), vmem_limit_bytes=VMEM_LIMIT_BYTES),
        name="retention",
    )(rq, rk, rv, rg, gain.reshape(1, width))


def _hgrn_kernel(q_ref, k_ref, v_ref, lf_ref, gate_ref, gain_ref, out_ref,
                 state_ref, cp_ref, *, tb, c, sub):
    @pl.when(pl.program_id(2) == 0)
    def _():
        state_ref[...] = jnp.zeros_like(state_ref)

    d = HGRN_HEAD_DIM
    pair = BF16_TILE_ROWS
    chunks = [slice(n * c, (n + 1) * c) for n in range(tb // c)]
    tri = (lax.broadcasted_iota(jnp.int32, (c, c), 0) >=
           lax.broadcasted_iota(jnp.int32, (c, c), 1)).astype(BF16)
    pair_row = lax.broadcasted_iota(jnp.int32, (pair, d), 0) % sub
    pair_lane = lax.broadcasted_iota(jnp.int32, (pair, d), 1) % sub
    a_row = lax.broadcasted_iota(jnp.int32, (c, c), 0)
    a_col = lax.broadcasted_iota(jnp.int32, (c, c), 1)
    halves = []
    h = c // 2
    while h >= sub:
        halves.append(h)
        h //= 2
    gain = gain_ref[...]

    cums = []
    for rows in chunks:
        lf = lf_ref[rows, :]
        lf1 = lf.astype(BF16)
        r1 = lf - lf1.astype(F32)
        lf2 = r1.astype(BF16)
        lf3 = (r1 - lf2.astype(F32)).astype(BF16)
        cum3 = _dot(tri, jnp.concatenate([lf1, lf2, lf3], axis=1))
        cums.append(cum3[:, :d] + cum3[:, d:2 * d] + cum3[:, 2 * d:])

    level_ops, a_diags, q_ins, k_ends, decays = [], [], [], [], []
    for rows, cum in zip(chunks, cums):
        r0 = rows.start
        q = q_ref[rows, :].astype(F32)
        k = k_ref[rows, :].astype(F32)
        cp = cum - jnp.log2(k)
        cp_ref[rows, :] = cp
        total = cum[c - 1:c, :]
        q_ins.append((q * jnp.exp2(cum)).astype(BF16))
        k_ends.append(jnp.exp2(total - cp).astype(BF16))
        decays.append(jnp.exp2(total))

        ops = []
        for h in halves:
            q_parts, k_parts = [], []
            for p0 in range(0, c, 2 * h):
                lo, up = slice(p0, p0 + h), slice(p0 + h, p0 + 2 * h)
                c_b = cum[p0 + h - 1:p0 + h, :]
                q_parts += [jnp.zeros((h, d), F32), q[up] * jnp.exp2(cum[up] - c_b)]
                k_parts += [jnp.exp2(c_b - cp[lo]), jnp.zeros((h, d), F32)]
            ops.append((jnp.concatenate(q_parts, axis=0).astype(BF16),
                        jnp.concatenate(k_parts, axis=0).astype(BF16)))
        level_ops.append(ops)

        pairs = []
        for i0 in range(0, c, pair):
            q_i = q[i0:i0 + pair]
            c_i = cum[i0:i0 + pair]
            a_pair = jnp.zeros((pair, d), F32)
            for j in range(sub):
                srcs = [cp_ref[r0 + i0 + s + j:r0 + i0 + s + j + 1, :] for s in range(0, pair, sub)]
                if len(srcs) == 1:
                    cp_j = srcs[0]
                else:
                    cp_j = jnp.concatenate([jnp.broadcast_to(s, (sub, d)) for s in srcs], axis=0)
                arg = c_i - cp_j
                if j:
                    arg = jnp.where(pair_row >= j, arg, NEG_BIG)
                s_j = jnp.sum(q_i * jnp.exp2(arg), axis=-1, keepdims=True)
                a_pair = jnp.where(pair_lane == j, s_j, a_pair)
            pairs.append(a_pair)
        a_diags.append(jnp.concatenate(pairs, axis=0)[:, :c])

    lows = [[_dot_nt(q_l, k_l) for q_l, k_l in ops] for ops in level_ops]
    updates = [_dot_tn(v_ref[rows, :], k_end) for rows, k_end in zip(chunks, k_ends)]

    state = state_ref[...]
    states = []
    for decay, upd in zip(decays, updates):
        states.append(state.astype(BF16))
        state = decay * state + upd
    state_ref[...] = state

    for n, rows in enumerate(chunks):
        attn = jnp.where((a_row // sub) == (a_col // sub), a_diags[n], 0.0)
        for h, low in zip(halves, lows[n]):
            if 2 * h == c:
                attn = attn + low
            else:
                attn = attn + jnp.where((a_row // (2 * h)) == (a_col // (2 * h)), low, 0.0)
        o = _dot(attn.astype(BF16), v_ref[rows, :]) + _dot_nt(q_ins[n], states[n])
        y = o * lax.rsqrt(jnp.mean(o * o, axis=-1, keepdims=True) + EPS)
        out_ref[rows, :] = (y * gain * gate_ref[rows, :].astype(F32)).astype(BF16)


def _hgrn(hq, hk, hv, hlf, hg, gain, batch, seq):
    t, width = hq.shape
    tb, c, d, sub = HGRN_TB, HGRN_CHUNK, HGRN_HEAD_DIM, HGRN_SUB
    assert seq % tb == 0 and tb % c == 0 and c % BF16_TILE_ROWS == 0 and BF16_TILE_ROWS % sub == 0
    assert c <= d
    nblk = seq // tb
    blk = pl.BlockSpec((tb, d), lambda b, h, n: (b * nblk + n, h))
    return pl.pallas_call(
        functools.partial(_hgrn_kernel, tb=tb, c=c, sub=sub),
        grid=(batch, width // d, nblk),
        in_specs=[blk, blk, blk, blk, blk, pl.BlockSpec((1, d), lambda b, h, n: (0, h))],
        out_specs=blk,
        out_shape=jax.ShapeDtypeStruct((t, width), BF16),
        scratch_shapes=[pltpu.VMEM((d, d), F32), pltpu.VMEM((tb, d), F32)],
        compiler_params=pltpu.CompilerParams(
            dimension_semantics=("parallel", "parallel", "arbitrary"),
            vmem_limit_bytes=VMEM_LIMIT_BYTES),
        name="hgrn2",
    )(hq, hk, hv, hlf, hg, gain.reshape(1, width))


@jax.jit
def kernel(x, ffn1_norm, ffn1_w_gate, ffn1_w_up, ffn1_w_down, mix_norm, w_in, ret_norm_g, hgrn_lb_logits, hgrn_norm_g, w_out, ffn2_norm, ffn2_w_gate, ffn2_w_up, ffn2_w_down, final_norm):
    batch, seq, d = x.shape
    assert ffn1_norm.shape[0] == 1, "single-layer stack"
    t = batch * seq
    ret_width = RET_HEADS * RET_HEAD_DIM
    bf = lambda w: w[0].astype(BF16)

    inv = jnp.power(ROPE_BASE, -jnp.arange(0, RET_HEAD_DIM, 2, dtype=F32) / RET_HEAD_DIM)
    ang = jnp.arange(seq, dtype=F32)[:, None] * inv[None, :]
    cos, sin = jnp.cos(ang), jnp.sin(ang)

    x0 = x.reshape(t, d)
    x1 = _ffn(x0, ffn1_norm[0], bf(ffn1_w_gate), bf(ffn1_w_up), bf(ffn1_w_down), tm=FFN1_TM)
    rq, rk, rv, rg, hq, hk, hlf, hi, hg = _inproj(
        x1, mix_norm[0], bf(w_in), hgrn_lb_logits.astype(F32), cos, sin, seq)
    ret = _retention(rq, rk, rv, rg, ret_norm_g[0].astype(F32), batch, seq)
    hgo = _hgrn(hq, hk, hi, hlf, hg, hgrn_norm_g[0].astype(F32), batch, seq)
    wo = bf(w_out)
    out = _ffn(x1, ffn2_norm[0], bf(ffn2_w_gate), bf(ffn2_w_up), bf(ffn2_w_down), tm=FFN2_TM,
               pre=(ret, hgo, wo[:ret_width], wo[ret_width:]), final_g=final_norm)
    return out.reshape(batch, seq, d)
```

```python
import functools
import math

import jax
import jax.numpy as jnp
from jax import lax
from jax.experimental import pallas as pl
from jax.experimental.pallas import tpu as pltpu

F32 = jnp.float32
BF16 = jnp.bfloat16

RET_HEADS = 4
RET_HEAD_DIM = 256
HGRN_HEADS = 8
HGRN_HEAD_DIM = 128
ROPE_BASE = 10000.0
EPS = 1e-6
FFN_RESIDUAL_WEIGHT = 0.5

VMEM_LIMIT_BYTES = 56 * 1024 * 1024
BF16_TILE_ROWS = 16

FFN_TM = 1024
FFN_TF = 512
INPROJ_TM = 1024
OUTPROJ_TM = 512
RET_TB = 512
RET_CHUNK = 128
HGRN_TB = 512
HGRN_CHUNK = 64
HGRN_SUB = 8


def _dot(a, b):
    return jnp.dot(a, b, preferred_element_type=F32)


def _dot_nt(a, b):
    return lax.dot_general(a, b, (((1,), (1,)), ((), ())), preferred_element_type=F32)


def _dot_tn(a, b):
    return lax.dot_general(a, b, (((0,), (0,)), ((), ())), preferred_element_type=F32)


def _rmsnorm(x, g):
    return x * lax.rsqrt(jnp.mean(x * x, axis=-1, keepdims=True) + EPS) * g


def _silu(x):
    return x * jax.nn.sigmoid(x)


def _ffn_kernel(*refs, final_norm, n_ff_steps):
    if final_norm:
        x_ref, g_ref, wg_ref, wu_ref, wd_ref, fg_ref, out_ref, h_ref = refs
    else:
        x_ref, g_ref, wg_ref, wu_ref, wd_ref, out_ref, h_ref = refs

    j = pl.program_id(1)

    @pl.when(j == 0)
    def _():
        x = x_ref[...]
        out_ref[...] = x
        h_ref[...] = _rmsnorm(x, g_ref[...]).astype(BF16)

    h = h_ref[...]
    gate = _dot(h, wg_ref[...])
    up = _dot(h, wu_ref[...])
    act = (_silu(gate) * up * FFN_RESIDUAL_WEIGHT).astype(BF16)
    out_ref[...] += _dot(act, wd_ref[...])

    if final_norm:
        @pl.when(j == n_ff_steps - 1)
        def _():
            out_ref[...] = _rmsnorm(out_ref[...], fg_ref[...])


def _ffn(x, norm_g, w_gate, w_up, w_down, *, name, final_g=None):
    t, d = x.shape
    d_ff = w_gate.shape[1]
    tm, tf = FFN_TM, FFN_TF
    assert t % tm == 0 and d_ff % tf == 0
    n_ff_steps = d_ff // tf
    row = lambda i, j: (i, 0)
    const = lambda i, j: (0, 0)
    in_specs = [pl.BlockSpec((tm, d), row),
                pl.BlockSpec((1, d), const),
                pl.BlockSpec((d, tf), lambda i, j: (0, j)),
                pl.BlockSpec((d, tf), lambda i, j: (0, j)),
                pl.BlockSpec((tf, d), lambda i, j: (j, 0))]
    args = [x, norm_g.reshape(1, d), w_gate, w_up, w_down]
    if final_g is not None:
        in_specs.append(pl.BlockSpec((1, d), const))
        args.append(final_g.reshape(1, d))
    kern = functools.partial(_ffn_kernel, final_norm=final_g is not None, n_ff_steps=n_ff_steps)
    return pl.pallas_call(
        kern,
        grid=(t // tm, n_ff_steps),
        in_specs=in_specs,
        out_specs=pl.BlockSpec((tm, d), row),
        out_shape=jax.ShapeDtypeStruct((t, d), F32),
        scratch_shapes=[pltpu.VMEM((tm, d), BF16)],
        compiler_params=pltpu.CompilerParams(
            dimension_semantics=("parallel", "arbitrary"), vmem_limit_bytes=VMEM_LIMIT_BYTES),
        name=name,
    )(*args)


def _outproj_kernel(x_ref, ret_ref, hgo_ref, wo_r_ref, wo_h_ref, out_ref):
    out_ref[...] = (x_ref[...] + _dot(ret_ref[...], wo_r_ref[...])
                    + _dot(hgo_ref[...], wo_h_ref[...]))


def _outproj(x1, ret, hgo, wo_r, wo_h):
    t, d = x1.shape
    tm = OUTPROJ_TM
    assert t % tm == 0
    row = lambda i: (i, 0)
    const = lambda i: (0, 0)
    return pl.pallas_call(
        _outproj_kernel,
        grid=(t // tm,),
        in_specs=[pl.BlockSpec((tm, d), row),
                  pl.BlockSpec((tm, ret.shape[1]), row), pl.BlockSpec((tm, hgo.shape[1]), row),
                  pl.BlockSpec(wo_r.shape, const), pl.BlockSpec(wo_h.shape, const)],
        out_specs=pl.BlockSpec((tm, d), row),
        out_shape=jax.ShapeDtypeStruct((t, d), F32),
        compiler_params=pltpu.CompilerParams(
            dimension_semantics=("parallel",), vmem_limit_bytes=VMEM_LIMIT_BYTES),
        name="outproj",
    )(x1, ret, hgo, wo_r, wo_h)


def _inproj_kernel(x_ref, g_ref, cos_ref, sin_ref, lbl_ref,
                   w_rq, w_rk, w_rv, w_rg, w_hq, w_hf, w_hi, w_hg,
                   rq_ref, rk_ref, rv_ref, rg_ref, hq_ref, hk_ref, hlf_ref, hi_ref, hg_ref,
                   h_ref):
    @pl.when(pl.program_id(1) == 0)
    def _():
        h_ref[...] = _rmsnorm(x_ref[...], g_ref[...]).astype(BF16)

    h = h_ref[...]
    cos = cos_ref[...]
    sin = sin_ref[...]
    half = RET_HEAD_DIM // 2

    def rope(p):
        x1, x2 = p[:, :half], p[:, half:]
        return jnp.concatenate([x1 * cos - x2 * sin, x2 * cos + x1 * sin], axis=-1)

    rq_ref[...] = (rope(_dot(h, w_rq[...])) * (RET_HEAD_DIM ** -0.5)).astype(BF16)
    rk_ref[...] = rope(_dot(h, w_rk[...])).astype(BF16)
    rv_ref[...] = _dot(h, w_rv[...]).astype(BF16)
    rg_ref[...] = _silu(_dot(h, w_rg[...])).astype(BF16)
    hq_ref[...] = _silu(_dot(h, w_hq[...])).astype(BF16)

    lbl = lbl_ref[...]
    m = jnp.maximum(lbl, 0.0)
    e_l = jnp.exp(lbl - m)
    lb = e_l / (e_l + jnp.exp(-m))
    z = _dot(h, w_hf[...])
    f = lb + (1.0 - lb) * jax.nn.sigmoid(z)
    hk_ref[...] = ((1.0 - lb) * jax.nn.sigmoid(-z)).astype(BF16)
    hlf_ref[...] = jnp.log2(f)
    hi_ref[...] = _dot(h, w_hi[...]).astype(BF16)
    hg_ref[...] = _silu(_dot(h, w_hg[...])).astype(BF16)


def _inproj(x1, norm_g, w_in, lb_logits, cos, sin, seq):
    t, d = x1.shape
    tm = INPROJ_TM
    tn = RET_HEAD_DIM
    width = RET_HEADS * RET_HEAD_DIM
    assert w_in.shape[1] == 8 * width and width == HGRN_HEADS * HGRN_HEAD_DIM
    assert t % tm == 0 and seq % tm == 0
    nj = width // tn
    pos_blocks = seq // tm
    row = lambda i, j: (i, 0)
    const = lambda i, j: (0, 0)

    def w_spec(group):
        return pl.BlockSpec((d, tn), lambda i, j, g=group: (0, g * nj + j))

    out_spec = pl.BlockSpec((tm, tn), lambda i, j: (i, j))
    out_bf = jax.ShapeDtypeStruct((t, width), BF16)
    out_f32 = jax.ShapeDtypeStruct((t, width), F32)
    return pl.pallas_call(
        _inproj_kernel,
        grid=(t // tm, nj),
        in_specs=[pl.BlockSpec((tm, d), row), pl.BlockSpec((1, d), const),
                  pl.BlockSpec((tm, tn // 2), lambda i, j: (i % pos_blocks, 0)),
                  pl.BlockSpec((tm, tn // 2), lambda i, j: (i % pos_blocks, 0)),
                  pl.BlockSpec((1, tn), lambda i, j: (0, j))] + [w_spec(g) for g in range(8)],
        out_specs=[out_spec] * 9,
        out_shape=[out_bf, out_bf, out_bf, out_bf, out_bf, out_bf, out_f32, out_bf, out_bf],
        scratch_shapes=[pltpu.VMEM((tm, d), BF16)],
        compiler_params=pltpu.CompilerParams(
            dimension_semantics=("parallel", "arbitrary"), vmem_limit_bytes=VMEM_LIMIT_BYTES),
        name="inproj",
    )(x1, norm_g.reshape(1, d), cos, sin, lb_logits, *([w_in] * 8))


def _ret_kernel(q_ref, k_ref, v_ref, gate_ref, gain_ref, out_ref, state_ref, *, tb, c):
    @pl.when(pl.program_id(1) == 0)
    def _():
        state_ref[...] = jnp.zeros_like(state_ref)

    dk = RET_HEAD_DIM
    row_cc = lax.broadcasted_iota(jnp.int32, (c, c), 0)
    col_cc = lax.broadcasted_iota(jnp.int32, (c, c), 1)
    rel = (row_cc - col_cc).astype(F32)
    row_cd = lax.broadcasted_iota(jnp.int32, (c, dk), 0).astype(F32)

    for h in range(RET_HEADS):
        lg = math.log(1.0 - 2.0 ** (-5.0 - h))
        decay = jnp.where(rel >= 0, jnp.exp(lg * jnp.maximum(rel, 0.0)), 0.0)
        q_dec = jnp.exp(lg * (row_cd + 1.0))
        k_dec = jnp.exp(lg * (c - 1.0 - row_cd))
        g_chunk = math.exp(lg * c)
        cols = slice(h * dk, (h + 1) * dk)
        gain = gain_ref[:, cols]
        for ci in range(tb // c):
            rows = slice(ci * c, (ci + 1) * c)
            q = q_ref[rows, cols]
            k = k_ref[rows, cols]
            v = v_ref[rows, cols]
            state = state_ref[h]
            scores = _dot_nt(q, k) * decay
            o = _dot(scores.astype(BF16), v) + _dot(q, state.astype(BF16)) * q_dec
            k_scaled = (k.astype(F32) * k_dec).astype(BF16)
            state_ref[h] = g_chunk * state + _dot_tn(k_scaled, v)
            mu = jnp.mean(o, axis=-1, keepdims=True)
            oc = o - mu
            var = jnp.mean(oc * oc, axis=-1, keepdims=True)
            y = oc * lax.rsqrt(var + EPS) * gain * gate_ref[rows, cols].astype(F32)
            out_ref[rows, cols] = y.astype(BF16)


def _retention(rq, rk, rv, rg, gain, batch, seq):
    t, width = rq.shape
    tb, c = RET_TB, RET_CHUNK
    assert seq % tb == 0 and tb % c == 0
    nblk = seq // tb
    blk = pl.BlockSpec((tb, width), lambda b, n: (b * nblk + n, 0))
    return pl.pallas_call(
        functools.partial(_ret_kernel, tb=tb, c=c),
        grid=(batch, nblk),
        in_specs=[blk, blk, blk, blk, pl.BlockSpec((1, width), lambda b, n: (0, 0))],
        out_specs=blk,
        out_shape=jax.ShapeDtypeStruct((t, width), BF16),
        scratch_shapes=[pltpu.VMEM((RET_HEADS, RET_HEAD_DIM, RET_HEAD_DIM), F32)],
        compiler_params=pltpu.CompilerParams(
            dimension_semantics=("parallel", "arbitrary"), vmem_limit_bytes=VMEM_LIMIT_BYTES),
        name="retention",
    )(rq, rk, rv, rg, gain.reshape(1, width))


def _hgrn_kernel(q_ref, k_ref, v_ref, lf_ref, gate_ref, gain_ref, out_ref,
                 state_ref, cp_ref, *, tb, c, sub):
    @pl.when(pl.program_id(2) == 0)
    def _():
        state_ref[...] = jnp.zeros_like(state_ref)

    d = HGRN_HEAD_DIM
    pair = BF16_TILE_ROWS
    chunks = [slice(n * c, (n + 1) * c) for n in range(tb // c)]
    tri = (lax.broadcasted_iota(jnp.int32, (c, c), 0) >=
           lax.broadcasted_iota(jnp.int32, (c, c), 1)).astype(BF16)
    pair_lane = lax.broadcasted_iota(jnp.int32, (pair, d), 1) % sub
    a_row = lax.broadcasted_iota(jnp.int32, (c, c), 0)
    a_col = lax.broadcasted_iota(jnp.int32, (c, c), 1)
    diag_mask = ((a_row // sub) == (a_col // sub)) & (a_col <= a_row)
    halves = []
    h = c // 2
    while h >= sub:
        halves.append(h)
        h //= 2
    gain = gain_ref[...]

    cums = []
    for rows in chunks:
        lf = lf_ref[rows, :]
        lf1 = lf.astype(BF16)
        r1 = lf - lf1.astype(F32)
        lf2 = r1.astype(BF16)
        lf3 = (r1 - lf2.astype(F32)).astype(BF16)
        cum3 = _dot(tri, jnp.concatenate([lf1, lf2, lf3], axis=1))
        cums.append(cum3[:, :d] + cum3[:, d:2 * d] + cum3[:, 2 * d:])

    level_ops, a_diags, q_ins, k_ends, decays = [], [], [], [], []
    for rows, cum in zip(chunks, cums):
        r0 = rows.start
        q = q_ref[rows, :].astype(F32)
        k = k_ref[rows, :].astype(F32)
        cp = cum - jnp.log2(k)
        cp_ref[rows, :] = cp
        total = cum[c - 1:c, :]
        q_ins.append((q * jnp.exp2(cum)).astype(BF16))
        k_ends.append(jnp.exp2(total - cp).astype(BF16))
        decays.append(jnp.exp2(total))

        ops = []
        for h in halves:
            q_parts, k_parts = [], []
            for p0 in range(0, c, 2 * h):
                lo, up = slice(p0, p0 + h), slice(p0 + h, p0 + 2 * h)
                c_b = cum[p0 + h - 1:p0 + h, :]
                q_parts += [jnp.zeros((h, d), F32), q[up] * jnp.exp2(cum[up] - c_b)]
                k_parts += [jnp.exp2(c_b - cp[lo]), jnp.zeros((h, d), F32)]
            ops.append((jnp.concatenate(q_parts, axis=0).astype(BF16),
                        jnp.concatenate(k_parts, axis=0).astype(BF16)))
        level_ops.append(ops)

        pairs = []
        for i0 in range(0, c, pair):
            q_i = q[i0:i0 + pair]
            c_i = cum[i0:i0 + pair]
            a_pair = jnp.zeros((pair, d), F32)
            for j in range(sub):
                srcs = [cp_ref[r0 + i0 + s + j:r0 + i0 + s + j + 1, :] for s in range(0, pair, sub)]
                if len(srcs) == 1:
                    cp_j = srcs[0]
                else:
                    cp_j = jnp.concatenate([jnp.broadcast_to(s, (sub, d)) for s in srcs], axis=0)
                s_j = jnp.sum(q_i * jnp.exp2(c_i - cp_j), axis=-1, keepdims=True)
                a_pair = jnp.where(pair_lane == j, s_j, a_pair)
            pairs.append(a_pair)
        a_diags.append(jnp.concatenate(pairs, axis=0)[:, :c])

    lows = [[_dot_nt(q_l, k_l) for q_l, k_l in ops] for ops in level_ops]
    updates = [_dot_tn(v_ref[rows, :], k_end) for rows, k_end in zip(chunks, k_ends)]

    state = state_ref[...]
    states = []
    for decay, upd in zip(decays, updates):
        states.append(state.astype(BF16))
        state = decay * state + upd
    state_ref[...] = state

    for n, rows in enumerate(chunks):
        attn = jnp.where(diag_mask, a_diags[n], 0.0)
        for h, low in zip(halves, lows[n]):
            if 2 * h == c:
                attn = attn + low
            else:
                attn = attn + jnp.where((a_row // (2 * h)) == (a_col // (2 * h)), low, 0.0)
        o = _dot(attn.astype(BF16), v_ref[rows, :]) + _dot_nt(q_ins[n], states[n])
        y = o * lax.rsqrt(jnp.mean(o * o, axis=-1, keepdims=True) + EPS)
        out_ref[rows, :] = (y * gain * gate_ref[rows, :].astype(F32)).astype(BF16)


def _hgrn(hq, hk, hv, hlf, hg, gain, batch, seq):
    t, width = hq.shape
    tb, c, d, sub = HGRN_TB, HGRN_CHUNK, HGRN_HEAD_DIM, HGRN_SUB
    assert seq % tb == 0 and tb % c == 0 and c % BF16_TILE_ROWS == 0 and BF16_TILE_ROWS % sub == 0
    assert c <= d
    nblk = seq // tb
    blk = pl.BlockSpec((tb, d), lambda b, h, n: (b * nblk + n, h))
    return pl.pallas_call(
        functools.partial(_hgrn_kernel, tb=tb, c=c, sub=sub),
        grid=(batch, width // d, nblk),
        in_specs=[blk, blk, blk, blk, blk, pl.BlockSpec((1, d), lambda b, h, n: (0, h))],
        out_specs=blk,
        out_shape=jax.ShapeDtypeStruct((t, width), BF16),
        scratch_shapes=[pltpu.VMEM((d, d), F32), pltpu.VMEM((tb, d), F32)],
        compiler_params=pltpu.CompilerParams(
            dimension_semantics=("parallel", "parallel", "arbitrary"),
            vmem_limit_bytes=VMEM_LIMIT_BYTES),
        name="hgrn2",
    )(hq, hk, hv, hlf, hg, gain.reshape(1, width))


@jax.jit
def kernel(x, ffn1_norm, ffn1_w_gate, ffn1_w_up, ffn1_w_down, mix_norm, w_in, ret_norm_g, hgrn_lb_logits, hgrn_norm_g, w_out, ffn2_norm, ffn2_w_gate, ffn2_w_up, ffn2_w_down, final_norm):
    batch, seq, d = x.shape
    assert ffn1_norm.shape[0] == 1, "single-layer stack"
    t = batch * seq
    ret_width = RET_HEADS * RET_HEAD_DIM
    bf = lambda w: w[0].astype(BF16)

    inv = jnp.power(ROPE_BASE, -jnp.arange(0, RET_HEAD_DIM, 2, dtype=F32) / RET_HEAD_DIM)
    ang = jnp.arange(seq, dtype=F32)[:, None] * inv[None, :]
    cos, sin = jnp.cos(ang), jnp.sin(ang)

    x0 = x.reshape(t, d)
    x1 = _ffn(x0, ffn1_norm[0], bf(ffn1_w_gate), bf(ffn1_w_up), bf(ffn1_w_down), name="ffn1")
    rq, rk, rv, rg, hq, hk, hlf, hi, hg = _inproj(
        x1, mix_norm[0], bf(w_in), hgrn_lb_logits.astype(F32), cos, sin, seq)
    ret = _retention(rq, rk, rv, rg, ret_norm_g[0].astype(F32), batch, seq)
    hgo = _hgrn(hq, hk, hi, hlf, hg, hgrn_norm_g[0].astype(F32), batch, seq)
    wo = bf(w_out)
    x2 = _outproj(x1, ret, hgo, wo[:ret_width], wo[ret_width:])
    out = _ffn(x2, ffn2_norm[0], bf(ffn2_w_gate), bf(ffn2_w_up), bf(ffn2_w_down), name="ffn2",
               final_g=final_norm)
    return out.reshape(batch, seq, d)
```

```python
import functools

import jax
import jax.numpy as jnp
from jax import lax
from jax.experimental import pallas as pl
from jax.experimental.pallas import tpu as pltpu

F32 = jnp.float32
BF16 = jnp.bfloat16

RET_HEADS = 4
RET_HEAD_DIM = 256
HGRN_HEADS = 8
HGRN_HEAD_DIM = 128
ROPE_BASE = 10000.0
EPS = 1e-6
FFN_RESIDUAL_WEIGHT = 0.5

VMEM_LIMIT_BYTES = 56 * 1024 * 1024
BF16_TILE_ROWS = 16

FFN_TM = 1024
FFN_TF = 512
MIXER_TM = 512
OUTPROJ_TM = 512
RET_CHUNK = 128
HGRN_CHUNK = 64
HGRN_SUB = 8


def _dot(a, b):
    return jnp.dot(a, b, preferred_element_type=F32)


def _dot_nt(a, b):
    return lax.dot_general(a, b, (((1,), (1,)), ((), ())), preferred_element_type=F32)


def _dot_tn(a, b):
    return lax.dot_general(a, b, (((0,), (0,)), ((), ())), preferred_element_type=F32)


def _rmsnorm(x, g):
    return x * lax.rsqrt(jnp.mean(x * x, axis=-1, keepdims=True) + EPS) * g


def _silu(x):
    return x * jax.nn.sigmoid(x)


def _ffn_kernel(*refs, final_norm, n_ff_steps):
    if final_norm:
        x_ref, g_ref, wg_ref, wu_ref, wd_ref, fg_ref, out_ref, h_ref = refs
    else:
        x_ref, g_ref, wg_ref, wu_ref, wd_ref, out_ref, h_ref = refs

    j = pl.program_id(1)

    @pl.when(j == 0)
    def _():
        x = x_ref[...]
        out_ref[...] = x
        h_ref[...] = _rmsnorm(x, g_ref[...]).astype(BF16)

    h = h_ref[...]
    gate = _dot(h, wg_ref[...])
    up = _dot(h, wu_ref[...])
    act = (_silu(gate) * up * FFN_RESIDUAL_WEIGHT).astype(BF16)
    out_ref[...] += _dot(act, wd_ref[...])

    if final_norm:
        @pl.when(j == n_ff_steps - 1)
        def _():
            out_ref[...] = _rmsnorm(out_ref[...], fg_ref[...])


def _ffn(x, norm_g, w_gate, w_up, w_down, *, name, final_g=None):
    t, d = x.shape
    d_ff = w_gate.shape[1]
    tm, tf = FFN_TM, FFN_TF
    assert t % tm == 0 and d_ff % tf == 0
    n_ff_steps = d_ff // tf
    row = lambda i, j: (i, 0)
    const = lambda i, j: (0, 0)
    in_specs = [pl.BlockSpec((tm, d), row),
                pl.BlockSpec((1, d), const),
                pl.BlockSpec((d, tf), lambda i, j: (0, j)),
                pl.BlockSpec((d, tf), lambda i, j: (0, j)),
                pl.BlockSpec((tf, d), lambda i, j: (j, 0))]
    args = [x, norm_g.reshape(1, d), w_gate, w_up, w_down]
    if final_g is not None:
        in_specs.append(pl.BlockSpec((1, d), const))
        args.append(final_g.reshape(1, d))
    kern = functools.partial(_ffn_kernel, final_norm=final_g is not None, n_ff_steps=n_ff_steps)
    return pl.pallas_call(
        kern,
        grid=(t // tm, n_ff_steps),
        in_specs=in_specs,
        out_specs=pl.BlockSpec((tm, d), row),
        out_shape=jax.ShapeDtypeStruct((t, d), F32),
        scratch_shapes=[pltpu.VMEM((tm, d), BF16)],
        compiler_params=pltpu.CompilerParams(
            dimension_semantics=("parallel", "arbitrary"), vmem_limit_bytes=VMEM_LIMIT_BYTES),
        name=name,
    )(*args)


def _outproj_kernel(x_ref, ret_ref, hgo_ref, wo_r_ref, wo_h_ref, out_ref):
    out_ref[...] = (x_ref[...] + _dot(ret_ref[...], wo_r_ref[...])
                    + _dot(hgo_ref[...], wo_h_ref[...]))


def _outproj(x1, ret, hgo, wo_r, wo_h):
    t, d = x1.shape
    tm = OUTPROJ_TM
    assert t % tm == 0
    row = lambda i: (i, 0)
    const = lambda i: (0, 0)
    return pl.pallas_call(
        _outproj_kernel,
        grid=(t // tm,),
        in_specs=[pl.BlockSpec((tm, d), row),
                  pl.BlockSpec((tm, ret.shape[1]), row), pl.BlockSpec((tm, hgo.shape[1]), row),
                  pl.BlockSpec(wo_r.shape, const), pl.BlockSpec(wo_h.shape, const)],
        out_specs=pl.BlockSpec((tm, d), row),
        out_shape=jax.ShapeDtypeStruct((t, d), F32),
        compiler_params=pltpu.CompilerParams(
            dimension_semantics=("parallel",), vmem_limit_bytes=VMEM_LIMIT_BYTES),
        name="outproj",
    )(x1, ret, hgo, wo_r, wo_h)


class _RetentionHead:
    def __init__(self, q, k, v, head, c):
        tb, dk = q.shape
        self.c = c
        self.chunks = [slice(n * c, (n + 1) * c) for n in range(tb // c)]
        self.q, self.k, self.v = q, k, v

        def log_gamma(shape):
            return jnp.log(1.0 - jnp.exp2(-5.0 - jnp.full(shape, head, jnp.int32).astype(F32)))

        row = lax.broadcasted_iota(jnp.int32, (c, dk), 0).astype(F32)
        rel = (lax.broadcasted_iota(jnp.int32, (c, c), 0) -
               lax.broadcasted_iota(jnp.int32, (c, c), 1)).astype(F32)
        self.decay = jnp.where(rel >= 0, jnp.exp(log_gamma((c, c)) * jnp.maximum(rel, 0.0)), 0.0)
        self.q_dec = jnp.exp(log_gamma((c, dk)) * (row + 1.0))
        self.k_dec = jnp.exp(log_gamma((c, dk)) * (c - 1.0 - row))
        self.g_chunk = jnp.exp(log_gamma((1, dk)) * float(c))

    def free_dots(self):
        self.scores = [_dot_nt(self.q[r], self.k[r]) for r in self.chunks]
        self.updates = [_dot_tn((self.k[r].astype(F32) * self.k_dec).astype(BF16), self.v[r])
                        for r in self.chunks]

    def outputs(self, state):
        states = []
        for upd in self.updates:
            states.append(state.astype(BF16))
            state = self.g_chunk * state + upd
        outs = []
        for r, s, st in zip(self.chunks, self.scores, states):
            inner = _dot((s * self.decay).astype(BF16), self.v[r])
            outs.append(inner + _dot(self.q[r], st) * self.q_dec)
        return outs, state


class _HgrnConsts:
    def __init__(self, c, sub):
        d = HGRN_HEAD_DIM
        pair = BF16_TILE_ROWS
        self.c, self.sub, self.pair = c, sub, pair
        self.tri = (lax.broadcasted_iota(jnp.int32, (c, c), 0) >=
                    lax.broadcasted_iota(jnp.int32, (c, c), 1)).astype(BF16)
        self.pair_lane = lax.broadcasted_iota(jnp.int32, (pair, d), 1) % sub
        a_row = lax.broadcasted_iota(jnp.int32, (c, c), 0)
        a_col = lax.broadcasted_iota(jnp.int32, (c, c), 1)
        self.diag_mask = ((a_row // sub) == (a_col // sub)) & (a_col <= a_row)
        self.halves = []
        h = c // 2
        while h >= sub:
            self.halves.append(h)
            h //= 2
        self.level_masks = [(a_row // (2 * h)) == (a_col // (2 * h)) for h in self.halves]


class _HgrnHead:
    def __init__(self, q, k, v, lf, cp_ref, consts):
        self.q, self.k, self.v, self.lf, self.cp_ref, self.cs = q, k, v, lf, cp_ref, consts
        c = consts.c
        self.chunks = [slice(n * c, (n + 1) * c) for n in range(q.shape[0] // c)]

    def cumsum_dots(self):
        d = HGRN_HEAD_DIM
        self.cums = []
        for rows in self.chunks:
            lf = self.lf[rows]
            lf1 = lf.astype(BF16)
            r1 = lf - lf1.astype(F32)
            lf2 = r1.astype(BF16)
            lf3 = (r1 - lf2.astype(F32)).astype(BF16)
            cum3 = _dot(self.cs.tri, jnp.concatenate([lf1, lf2, lf3], axis=1))
            self.cums.append(cum3[:, :d] + cum3[:, d:2 * d] + cum3[:, 2 * d:])

    def elementwise(self):
        cs, cp_ref = self.cs, self.cp_ref
        c, sub, pair, d = cs.c, cs.sub, cs.pair, HGRN_HEAD_DIM
        self.level_ops, self.a_diags, self.q_ins, self.k_ends, self.decays = [], [], [], [], []
        for rows, cum in zip(self.chunks, self.cums):
            r0 = rows.start
            q = self.q[rows].astype(F32)
            k = self.k[rows].astype(F32)
            cp = cum - jnp.log2(k)
            cp_ref[rows, :] = cp
            total = cum[c - 1:c, :]
            self.q_ins.append((q * jnp.exp2(cum)).astype(BF16))
            self.k_ends.append(jnp.exp2(total - cp).astype(BF16))
            self.decays.append(jnp.exp2(total))

            ops = []
            for h in cs.halves:
                q_parts, k_parts = [], []
                for p0 in range(0, c, 2 * h):
                    lo, up = slice(p0, p0 + h), slice(p0 + h, p0 + 2 * h)
                    c_b = cum[p0 + h - 1:p0 + h, :]
                    q_parts += [jnp.zeros((h, d), F32), q[up] * jnp.exp2(cum[up] - c_b)]
                    k_parts += [jnp.exp2(c_b - cp[lo]), jnp.zeros((h, d), F32)]
                ops.append((jnp.concatenate(q_parts, axis=0).astype(BF16),
                            jnp.concatenate(k_parts, axis=0).astype(BF16)))
            self.level_ops.append(ops)

            pairs = []
            for i0 in range(0, c, pair):
                q_i = q[i0:i0 + pair]
                c_i = cum[i0:i0 + pair]
                a_pair = jnp.zeros((pair, d), F32)
                for j in range(sub):
                    srcs = [cp_ref[r0 + i0 + s + j:r0 + i0 + s + j + 1, :]
                            for s in range(0, pair, sub)]
                    if len(srcs) == 1:
                        cp_j = srcs[0]
                    else:
                        cp_j = jnp.concatenate([jnp.broadcast_to(s, (sub, d)) for s in srcs], axis=0)
                    s_j = jnp.sum(q_i * jnp.exp2(c_i - cp_j), axis=-1, keepdims=True)
                    a_pair = jnp.where(cs.pair_lane == j, s_j, a_pair)
                pairs.append(a_pair)
            self.a_diags.append(jnp.concatenate(pairs, axis=0)[:, :c])

    def free_dots(self):
        self.lows = [[_dot_nt(q_l, k_l) for q_l, k_l in ops] for ops in self.level_ops]
        self.updates = [_dot_tn(self.v[rows], k_end) for rows, k_end in zip(self.chunks, self.k_ends)]

    def outputs(self, state):
        cs = self.cs
        states = []
        for decay, upd in zip(self.decays, self.updates):
            states.append(state.astype(BF16))
            state = decay * state + upd
        outs = []
        for n, rows in enumerate(self.chunks):
            attn = jnp.where(cs.diag_mask, self.a_diags[n], 0.0)
            for h, mask, low in zip(cs.halves, cs.level_masks, self.lows[n]):
                attn = attn + (low if 2 * h == cs.c else jnp.where(mask, low, 0.0))
            outs.append(_dot(attn.astype(BF16), self.v[rows]) + _dot_nt(self.q_ins[n], states[n]))
        return outs, state


def _mixer_kernel(x_ref, g_ref, cos_ref, sin_ref, lbl_ref, rgain_ref, hgain_ref,
                  w_rq, w_rk, w_rv, w_rg, w_hq, w_hf, w_hi, w_hg,
                  ret_ref, hgo_ref,
                  h_ref, ret_state_ref, hgrn_state_ref, cp_ref, *, blocks_per_seq):
    i = pl.program_id(0)
    j = pl.program_id(1)
    dk, dh = RET_HEAD_DIM, HGRN_HEAD_DIM
    heads_per_step = dk // dh

    @pl.when(j == 0)
    def _():
        h_ref[...] = _rmsnorm(x_ref[...], g_ref[...]).astype(BF16)

    @pl.when(i % blocks_per_seq == 0)
    def _():
        ret_state_ref[j] = jnp.zeros((dk, dk), F32)
        for a in range(heads_per_step):
            hgrn_state_ref[heads_per_step * j + a] = jnp.zeros((dh, dh), F32)

    h = h_ref[...]
    tm = h.shape[0]
    cos = cos_ref[...]
    sin = sin_ref[...]
    half = dk // 2

    def rope(p):
        x1, x2 = p[:, :half], p[:, half:]
        return jnp.concatenate([x1 * cos - x2 * sin, x2 * cos + x1 * sin], axis=-1)

    hq = _silu(_dot(h, w_hq[...])).astype(BF16)
    lbl = lbl_ref[...]
    m = jnp.maximum(lbl, 0.0)
    e_l = jnp.exp(lbl - m)
    lb = e_l / (e_l + jnp.exp(-m))
    z = _dot(h, w_hf[...])
    hk = ((1.0 - lb) * jax.nn.sigmoid(-z)).astype(BF16)
    hlf = jnp.log2(lb + (1.0 - lb) * jax.nn.sigmoid(z))
    hv = _dot(h, w_hi[...]).astype(BF16)

    consts = _HgrnConsts(HGRN_CHUNK, HGRN_SUB)
    heads = []
    for a in range(heads_per_step):
        cols = slice(a * dh, (a + 1) * dh)
        heads.append(_HgrnHead(hq[:, cols], hk[:, cols], hv[:, cols], hlf[:, cols],
                               cp_ref.at[a], consts))
    for hd in heads:
        hd.cumsum_dots()
    rq = (rope(_dot(h, w_rq[...])) * (dk ** -0.5)).astype(BF16)
    for hd in heads:
        hd.elementwise()
    rk = rope(_dot(h, w_rk[...])).astype(BF16)
    for hd in heads:
        hd.free_dots()
    rv = _dot(h, w_rv[...]).astype(BF16)
    hgate = _silu(_dot(h, w_hg[...]))
    for a, hd in enumerate(heads):
        cols = slice(a * dh, (a + 1) * dh)
        outs, state = hd.outputs(hgrn_state_ref[heads_per_step * j + a])
        hgrn_state_ref[heads_per_step * j + a] = state
        gain = hgain_ref[:, cols]
        for rows, o in zip(hd.chunks, outs):
            y = o * lax.rsqrt(jnp.mean(o * o, axis=-1, keepdims=True) + EPS)
            hgo_ref[rows, cols] = (y * gain * hgate[rows, cols]).astype(BF16)

    rgate = _silu(_dot(h, w_rg[...]))
    ret = _RetentionHead(rq, rk, rv, j, RET_CHUNK)
    ret.free_dots()
    outs, state = ret.outputs(ret_state_ref[j])
    ret_state_ref[j] = state
    gain = rgain_ref[...]
    for rows, o in zip(ret.chunks, outs):
        mu = jnp.mean(o, axis=-1, keepdims=True)
        oc = o - mu
        var = jnp.mean(oc * oc, axis=-1, keepdims=True)
        ret_ref[rows, :] = (oc * lax.rsqrt(var + EPS) * gain * rgate[rows, :]).astype(BF16)


def _mixer(x1, norm_g, w_in, lb_logits, ret_gain, hgrn_gain, cos, sin, seq):
    t, d = x1.shape
    tm = MIXER_TM
    tn = RET_HEAD_DIM
    dh = HGRN_HEAD_DIM
    width = RET_HEADS * RET_HEAD_DIM
    assert w_in.shape[1] == 8 * width and width == HGRN_HEADS * dh and tn % dh == 0
    assert t % tm == 0 and seq % tm == 0
    assert tm % RET_CHUNK == 0 and tm % HGRN_CHUNK == 0
    assert HGRN_CHUNK % BF16_TILE_ROWS == 0 and BF16_TILE_ROWS % HGRN_SUB == 0
    nj = width // tn
    blocks_per_seq = seq // tm
    row = lambda i, j: (i, 0)
    const = lambda i, j: (0, 0)
    col = lambda i, j: (0, j)
    pos = lambda i, j: (i % blocks_per_seq, 0)

    def w_spec(group):
        return pl.BlockSpec((d, tn), lambda i, j, g=group: (0, g * nj + j))

    out_spec = pl.BlockSpec((tm, tn), lambda i, j: (i, j))
    out_bf = jax.ShapeDtypeStruct((t, width), BF16)
    return pl.pallas_call(
        functools.partial(_mixer_kernel, blocks_per_seq=blocks_per_seq),
        grid=(t // tm, nj),
        in_specs=[pl.BlockSpec((tm, d), row), pl.BlockSpec((1, d), const),
                  pl.BlockSpec((tm, tn // 2), pos), pl.BlockSpec((tm, tn // 2), pos),
                  pl.BlockSpec((1, tn), col), pl.BlockSpec((1, tn), col), pl.BlockSpec((1, tn), col)]
                 + [w_spec(g) for g in range(8)],
        out_specs=[out_spec, out_spec],
        out_shape=[out_bf, out_bf],
        scratch_shapes=[pltpu.VMEM((tm, d), BF16),
                        pltpu.VMEM((RET_HEADS, RET_HEAD_DIM, RET_HEAD_DIM), F32),
                        pltpu.VMEM((HGRN_HEADS, dh, dh), F32),
                        pltpu.VMEM((tn // dh, tm, dh), F32)],
        compiler_params=pltpu.CompilerParams(
            dimension_semantics=("arbitrary", "arbitrary"), vmem_limit_bytes=VMEM_LIMIT_BYTES),
        name="mixer",
    )(x1, norm_g.reshape(1, d), cos, sin, lb_logits, ret_gain.reshape(1, width),
      hgrn_gain.reshape(1, width), *([w_in] * 8))


@jax.jit
def kernel(x, ffn1_norm, ffn1_w_gate, ffn1_w_up, ffn1_w_down, mix_norm, w_in, ret_norm_g, hgrn_lb_logits, hgrn_norm_g, w_out, ffn2_norm, ffn2_w_gate, ffn2_w_up, ffn2_w_down, final_norm):
    batch, seq, d = x.shape
    assert ffn1_norm.shape[0] == 1, "single-layer stack"
    t = batch * seq
    ret_width = RET_HEADS * RET_HEAD_DIM
    bf = lambda w: w[0].astype(BF16)

    inv = jnp.power(ROPE_BASE, -jnp.arange(0, RET_HEAD_DIM, 2, dtype=F32) / RET_HEAD_DIM)
    ang = jnp.arange(seq, dtype=F32)[:, None] * inv[None, :]
    cos, sin = jnp.cos(ang), jnp.sin(ang)

    x0 = x.reshape(t, d)
    x1 = _ffn(x0, ffn1_norm[0], bf(ffn1_w_gate), bf(ffn1_w_up), bf(ffn1_w_down), name="ffn1")
    ret, hgo = _mixer(x1, mix_norm[0], bf(w_in), hgrn_lb_logits.astype(F32),
                      ret_norm_g[0].astype(F32), hgrn_norm_g[0].astype(F32), cos, sin, seq)
    wo = bf(w_out)
    x2 = _outproj(x1, ret, hgo, wo[:ret_width], wo[ret_width:])
    out = _ffn(x2, ffn2_norm[0], bf(ffn2_w_gate), bf(ffn2_w_up), bf(ffn2_w_down), name="ffn2",
               final_g=final_norm)
    return out.reshape(batch, seq, d)
```

```python
import functools

import jax
import jax.numpy as jnp
from jax import lax
from jax.experimental import pallas as pl
from jax.experimental.pallas import tpu as pltpu

F32 = jnp.float32
BF16 = jnp.bfloat16

RET_HEADS = 4
RET_HEAD_DIM = 256
HGRN_HEADS = 8
HGRN_HEAD_DIM = 128
ROPE_BASE = 10000.0
EPS = 1e-6
FFN_RESIDUAL_WEIGHT = 0.5

VMEM_LIMIT_BYTES = 56 * 1024 * 1024
BF16_TILE_ROWS = 16

FFN_TM = 1024
FFN_TF = 512
MIXER_TM = 512
OUTPROJ_TM = 512
RET_CHUNK = 128
HGRN_CHUNK = 64
HGRN_SUB = 8


def _dot(a, b):
    return jnp.dot(a, b, preferred_element_type=F32)


def _dot_nt(a, b):
    return lax.dot_general(a, b, (((1,), (1,)), ((), ())), preferred_element_type=F32)


def _dot_tn(a, b):
    return lax.dot_general(a, b, (((0,), (0,)), ((), ())), preferred_element_type=F32)


def _rmsnorm(x, g):
    return x * lax.rsqrt(jnp.mean(x * x, axis=-1, keepdims=True) + EPS) * g


def _silu(x):
    return x * jax.nn.sigmoid(x)


def _ffn_kernel(*refs, final_norm, n_ff_steps):
    if final_norm:
        x_ref, g_ref, wg_ref, wu_ref, wd_ref, fg_ref, out_ref, h_ref = refs
    else:
        x_ref, g_ref, wg_ref, wu_ref, wd_ref, out_ref, h_ref = refs

    j = pl.program_id(1)

    @pl.when(j == 0)
    def _():
        x = x_ref[...]
        out_ref[...] = x
        h_ref[...] = _rmsnorm(x, g_ref[...]).astype(BF16)

    h = h_ref[...]
    gate = _dot(h, wg_ref[...])
    up = _dot(h, wu_ref[...])
    act = (_silu(gate) * up * FFN_RESIDUAL_WEIGHT).astype(BF16)
    out_ref[...] += _dot(act, wd_ref[...])

    if final_norm:
        @pl.when(j == n_ff_steps - 1)
        def _():
            out_ref[...] = _rmsnorm(out_ref[...], fg_ref[...])


def _ffn(x, norm_g, w_gate, w_up, w_down, *, name, final_g=None):
    t, d = x.shape
    d_ff = w_gate.shape[1]
    tm, tf = FFN_TM, FFN_TF
    assert t % tm == 0 and d_ff % tf == 0
    n_ff_steps = d_ff // tf
    row = lambda i, j: (i, 0)
    const = lambda i, j: (0, 0)
    in_specs = [pl.BlockSpec((tm, d), row),
                pl.BlockSpec((1, d), const),
                pl.BlockSpec((d, tf), lambda i, j: (0, j)),
                pl.BlockSpec((d, tf), lambda i, j: (0, j)),
                pl.BlockSpec((tf, d), lambda i, j: (j, 0))]
    args = [x, norm_g.reshape(1, d), w_gate, w_up, w_down]
    if final_g is not None:
        in_specs.append(pl.BlockSpec((1, d), const))
        args.append(final_g.reshape(1, d))
    kern = functools.partial(_ffn_kernel, final_norm=final_g is not None, n_ff_steps=n_ff_steps)
    return pl.pallas_call(
        kern,
        grid=(t // tm, n_ff_steps),
        in_specs=in_specs,
        out_specs=pl.BlockSpec((tm, d), row),
        out_shape=jax.ShapeDtypeStruct((t, d), F32),
        scratch_shapes=[pltpu.VMEM((tm, d), BF16)],
        compiler_params=pltpu.CompilerParams(
            dimension_semantics=("parallel", "arbitrary"), vmem_limit_bytes=VMEM_LIMIT_BYTES),
        name=name,
    )(*args)


def _outproj_kernel(x_ref, ret_ref, hgo_ref, wo_r_ref, wo_h_ref, out_ref):
    out_ref[...] = (x_ref[...] + _dot(ret_ref[...], wo_r_ref[...])
                    + _dot(hgo_ref[...], wo_h_ref[...]))


def _outproj(x1, ret, hgo, wo_r, wo_h):
    t, d = x1.shape
    tm = OUTPROJ_TM
    assert t % tm == 0
    row = lambda i: (i, 0)
    const = lambda i: (0, 0)
    return pl.pallas_call(
        _outproj_kernel,
        grid=(t // tm,),
        in_specs=[pl.BlockSpec((tm, d), row),
                  pl.BlockSpec((tm, ret.shape[1]), row), pl.BlockSpec((tm, hgo.shape[1]), row),
                  pl.BlockSpec(wo_r.shape, const), pl.BlockSpec(wo_h.shape, const)],
        out_specs=pl.BlockSpec((tm, d), row),
        out_shape=jax.ShapeDtypeStruct((t, d), F32),
        compiler_params=pltpu.CompilerParams(
            dimension_semantics=("parallel",), vmem_limit_bytes=VMEM_LIMIT_BYTES),
        name="outproj",
    )(x1, ret, hgo, wo_r, wo_h)


class _RetentionHead:
    def __init__(self, q, k, v, head, c):
        tb, dk = q.shape
        self.c = c
        self.chunks = [slice(n * c, (n + 1) * c) for n in range(tb // c)]
        self.q, self.k, self.v = q, k, v

        def log_gamma(shape):
            return jnp.log(1.0 - jnp.exp2(-5.0 - jnp.full(shape, head, jnp.int32).astype(F32)))

        row = lax.broadcasted_iota(jnp.int32, (c, dk), 0).astype(F32)
        rel = (lax.broadcasted_iota(jnp.int32, (c, c), 0) -
               lax.broadcasted_iota(jnp.int32, (c, c), 1)).astype(F32)
        self.decay = jnp.where(rel >= 0, jnp.exp(log_gamma((c, c)) * jnp.maximum(rel, 0.0)), 0.0)
        self.q_dec = jnp.exp(log_gamma((c, dk)) * (row + 1.0))
        self.k_dec = jnp.exp(log_gamma((c, dk)) * (c - 1.0 - row))
        self.g_chunk = jnp.exp(log_gamma((1, dk)) * float(c))

    def free_dots(self):
        self.scores = [_dot_nt(self.q[r], self.k[r]) for r in self.chunks]
        self.updates = [_dot_tn((self.k[r].astype(F32) * self.k_dec).astype(BF16), self.v[r])
                        for r in self.chunks]

    def outputs(self, state):
        states = []
        for upd in self.updates:
            states.append(state.astype(BF16))
            state = self.g_chunk * state + upd
        outs = []
        for r, s, st in zip(self.chunks, self.scores, states):
            inner = _dot((s * self.decay).astype(BF16), self.v[r])
            outs.append(inner + _dot(self.q[r], st) * self.q_dec)
        return outs, state


class _HgrnConsts:
    def __init__(self, c, sub):
        d = HGRN_HEAD_DIM
        pair = BF16_TILE_ROWS
        self.c, self.sub, self.pair = c, sub, pair
        self.tri = (lax.broadcasted_iota(jnp.int32, (c, c), 0) >=
                    lax.broadcasted_iota(jnp.int32, (c, c), 1)).astype(BF16)
        self.pair_lane = lax.broadcasted_iota(jnp.int32, (pair, d), 1) % sub
        a_row = lax.broadcasted_iota(jnp.int32, (c, c), 0)
        a_col = lax.broadcasted_iota(jnp.int32, (c, c), 1)
        self.diag_mask = ((a_row // sub) == (a_col // sub)) & (a_col <= a_row)
        self.halves = []
        h = c // 2
        while h >= sub:
            self.halves.append(h)
            h //= 2
        self.level_masks = [(a_row // (2 * h)) == (a_col // (2 * h)) for h in self.halves]


class _HgrnHead:
    def __init__(self, q, k, v, lf, cp_ref, consts):
        self.q, self.k, self.v, self.lf, self.cp_ref, self.cs = q, k, v, lf, cp_ref, consts
        c = consts.c
        self.chunks = [slice(n * c, (n + 1) * c) for n in range(q.shape[0] // c)]

    def cumsum_dots(self):
        d = HGRN_HEAD_DIM
        self.cums = []
        for rows in self.chunks:
            lf = self.lf[rows]
            lf1 = lf.astype(BF16)
            r1 = lf - lf1.astype(F32)
            lf2 = r1.astype(BF16)
            lf3 = (r1 - lf2.astype(F32)).astype(BF16)
            cum3 = _dot(self.cs.tri, jnp.concatenate([lf1, lf2, lf3], axis=1))
            self.cums.append(cum3[:, :d] + cum3[:, d:2 * d] + cum3[:, 2 * d:])

    def elementwise(self):
        cs, cp_ref = self.cs, self.cp_ref
        c, sub, pair, d = cs.c, cs.sub, cs.pair, HGRN_HEAD_DIM
        self.level_ops, self.a_diags, self.q_ins, self.k_ends, self.decays = [], [], [], [], []
        for rows, cum in zip(self.chunks, self.cums):
            r0 = rows.start
            q = self.q[rows].astype(F32)
            k = self.k[rows].astype(F32)
            cp = cum - jnp.log2(k)
            cp_ref[rows, :] = cp
            total = cum[c - 1:c, :]
            self.q_ins.append((q * jnp.exp2(cum)).astype(BF16))
            self.k_ends.append(jnp.exp2(total - cp).astype(BF16))
            self.decays.append(jnp.exp2(total))

            ops = []
            for h in cs.halves:
                q_parts, k_parts = [], []
                for p0 in range(0, c, 2 * h):
                    lo, up = slice(p0, p0 + h), slice(p0 + h, p0 + 2 * h)
                    c_b = cum[p0 + h - 1:p0 + h, :]
                    q_parts += [jnp.zeros((h, d), F32), q[up] * jnp.exp2(cum[up] - c_b)]
                    k_parts += [jnp.exp2(c_b - cp[lo]), jnp.zeros((h, d), F32)]
                ops.append((jnp.concatenate(q_parts, axis=0).astype(BF16),
                            jnp.concatenate(k_parts, axis=0).astype(BF16)))
            self.level_ops.append(ops)

            pairs = []
            for i0 in range(0, c, pair):
                q_i = q[i0:i0 + pair]
                c_i = cum[i0:i0 + pair]
                a_pair = jnp.zeros((pair, d), F32)
                for j in range(sub):
                    srcs = [cp_ref[r0 + i0 + s + j:r0 + i0 + s + j + 1, :]
                            for s in range(0, pair, sub)]
                    if len(srcs) == 1:
                        cp_j = srcs[0]
                    else:
                        cp_j = jnp.concatenate([jnp.broadcast_to(s, (sub, d)) for s in srcs], axis=0)
                    s_j = jnp.sum(q_i * jnp.exp2(c_i - cp_j), axis=-1, keepdims=True)
                    a_pair = jnp.where(cs.pair_lane == j, s_j, a_pair)
                pairs.append(a_pair)
            self.a_diags.append(jnp.concatenate(pairs, axis=0)[:, :c])

    def free_dots(self):
        self.lows = [[_dot_nt(q_l, k_l) for q_l, k_l in ops] for ops in self.level_ops]
        self.updates = [_dot_tn(self.v[rows], k_end) for rows, k_end in zip(self.chunks, self.k_ends)]

    def outputs(self, state):
        cs = self.cs
        states = []
        for decay, upd in zip(self.decays, self.updates):
            states.append(state.astype(BF16))
            state = decay * state + upd
        outs = []
        for n, rows in enumerate(self.chunks):
            attn = jnp.where(cs.diag_mask, self.a_diags[n], 0.0)
            for h, mask, low in zip(cs.halves, cs.level_masks, self.lows[n]):
                attn = attn + (low if 2 * h == cs.c else jnp.where(mask, low, 0.0))
            outs.append(_dot(attn.astype(BF16), self.v[rows]) + _dot_nt(self.q_ins[n], states[n]))
        return outs, state


_RQ, _RK, _RV, _RG, _HQ, _HK, _HV, _HG = range(8)


def _mixer_kernel(x_ref, g_ref, cos_ref, sin_ref, lbl_ref, rgain_ref, hgain_ref,
                  w_rq, w_rk, w_rv, w_rg, w_hq, w_hf, w_hi, w_hg,
                  ret_ref, hgo_ref,
                  h_ref, act_ref, lf_ref, ret_state_ref, hgrn_state_ref, cp_ref,
                  *, n_col_blocks, blocks_per_seq):
    s = pl.program_id(0)
    nj = n_col_blocks
    dk, dh = RET_HEAD_DIM, HGRN_HEAD_DIM
    heads_per_step = dk // dh
    cons = jnp.maximum(s - 1, 0)
    cj = cons % nj

    @pl.when(s == 0)
    def _():
        act_ref[...] = jnp.zeros_like(act_ref)
        lf_ref[...] = jnp.zeros_like(lf_ref)

    @pl.when(s % nj == 0)
    def _():
        h_ref[...] = _rmsnorm(x_ref[...], g_ref[...]).astype(BF16)

    @pl.when((cons // nj) % blocks_per_seq == 0)
    def _():
        ret_state_ref[cj] = jnp.zeros((dk, dk), F32)
        for a in range(heads_per_step):
            hgrn_state_ref[heads_per_step * cj + a] = jnp.zeros((dh, dh), F32)

    h = h_ref[...]
    cos = cos_ref[...]
    sin = sin_ref[...]
    half = dk // 2
    dst = act_ref.at[s % 2]
    src = act_ref.at[(s + 1) % 2]
    src_lf = lf_ref.at[(s + 1) % 2]

    def rope(p):
        x1, x2 = p[:, :half], p[:, half:]
        return jnp.concatenate([x1 * cos - x2 * sin, x2 * cos + x1 * sin], axis=-1)

    consts = _HgrnConsts(HGRN_CHUNK, HGRN_SUB)
    hq, hk, hv, hlf = src[_HQ], src[_HK], src[_HV], src_lf[...]
    heads = []
    for a in range(heads_per_step):
        cols = slice(a * dh, (a + 1) * dh)
        heads.append(_HgrnHead(hq[:, cols], hk[:, cols], hv[:, cols], hlf[:, cols],
                               cp_ref.at[a], consts))
    ret = _RetentionHead(src[_RQ], src[_RK], src[_RV], cj, RET_CHUNK)

    for hd in heads:
        hd.cumsum_dots()
    dst[_HQ] = _silu(_dot(h, w_hq[...])).astype(BF16)
    for hd in heads:
        hd.elementwise()

    lbl = lbl_ref[...]
    m = jnp.maximum(lbl, 0.0)
    e_l = jnp.exp(lbl - m)
    lb = e_l / (e_l + jnp.exp(-m))
    z = _dot(h, w_hf[...])
    dst[_HK] = ((1.0 - lb) * jax.nn.sigmoid(-z)).astype(BF16)
    lf_ref[s % 2] = jnp.log2(lb + (1.0 - lb) * jax.nn.sigmoid(z))

    for hd in heads:
        hd.free_dots()
    ret.free_dots()
    dst[_HV] = _dot(h, w_hi[...]).astype(BF16)
    dst[_HG] = _silu(_dot(h, w_hg[...])).astype(BF16)

    hgate = src[_HG]
    for a, hd in enumerate(heads):
        cols = slice(a * dh, (a + 1) * dh)
        outs, state = hd.outputs(hgrn_state_ref[heads_per_step * cj + a])
        hgrn_state_ref[heads_per_step * cj + a] = state
        gain = hgain_ref[:, cols]
        for rows, o in zip(hd.chunks, outs):
            y = o * lax.rsqrt(jnp.mean(o * o, axis=-1, keepdims=True) + EPS)
            hgo_ref[rows, cols] = (y * gain * hgate[rows, cols].astype(F32)).astype(BF16)

    dst[_RQ] = (rope(_dot(h, w_rq[...])) * (dk ** -0.5)).astype(BF16)
    dst[_RK] = rope(_dot(h, w_rk[...])).astype(BF16)

    rgate = src[_RG]
    outs, state = ret.outputs(ret_state_ref[cj])
    ret_state_ref[cj] = state
    gain = rgain_ref[...]
    for rows, o in zip(ret.chunks, outs):
        mu = jnp.mean(o, axis=-1, keepdims=True)
        oc = o - mu
        var = jnp.mean(oc * oc, axis=-1, keepdims=True)
        ret_ref[rows, :] = (oc * lax.rsqrt(var + EPS) * gain * rgate[rows, :].astype(F32)).astype(BF16)

    dst[_RV] = _dot(h, w_rv[...]).astype(BF16)
    dst[_RG] = _silu(_dot(h, w_rg[...])).astype(BF16)


def _mixer(x1, norm_g, w_in, lb_logits, ret_gain, hgrn_gain, cos, sin, seq):
    t, d = x1.shape
    tm = MIXER_TM
    tn = RET_HEAD_DIM
    dh = HGRN_HEAD_DIM
    width = RET_HEADS * RET_HEAD_DIM
    assert w_in.shape[1] == 8 * width and width == HGRN_HEADS * dh and tn % dh == 0
    assert t % tm == 0 and seq % tm == 0
    assert tm % RET_CHUNK == 0 and tm % HGRN_CHUNK == 0
    assert HGRN_CHUNK % BF16_TILE_ROWS == 0 and BF16_TILE_ROWS % HGRN_SUB == 0
    nj = width // tn
    blocks_per_seq = seq // tm
    n_units = (t // tm) * nj
    prod = lambda s: jnp.minimum(s, n_units - 1)
    cons = lambda s: jnp.maximum(s - 1, 0)
    const = lambda s: (0, 0)
    pos = lambda s: ((prod(s) // nj) % blocks_per_seq, 0)
    cons_col = lambda s: (0, cons(s) % nj)

    def w_spec(group):
        return pl.BlockSpec((d, tn), lambda s, g=group: (0, g * nj + prod(s) % nj))

    out_spec = pl.BlockSpec((tm, tn), lambda s: (cons(s) // nj, cons(s) % nj))
    out_bf = jax.ShapeDtypeStruct((t, width), BF16)
    return pl.pallas_call(
        functools.partial(_mixer_kernel, n_col_blocks=nj, blocks_per_seq=blocks_per_seq),
        grid=(n_units + 1,),
        in_specs=[pl.BlockSpec((tm, d), lambda s: (prod(s) // nj, 0)), pl.BlockSpec((1, d), const),
                  pl.BlockSpec((tm, tn // 2), pos), pl.BlockSpec((tm, tn // 2), pos),
                  pl.BlockSpec((1, tn), lambda s: (0, prod(s) % nj)),
                  pl.BlockSpec((1, tn), cons_col), pl.BlockSpec((1, tn), cons_col)]
                 + [w_spec(g) for g in range(8)],
        out_specs=[out_spec, out_spec],
        out_shape=[out_bf, out_bf],
        scratch_shapes=[pltpu.VMEM((tm, d), BF16),
                        pltpu.VMEM((2, 8, tm, tn), BF16),
                        pltpu.VMEM((2, tm, tn), F32),
                        pltpu.VMEM((RET_HEADS, RET_HEAD_DIM, RET_HEAD_DIM), F32),
                        pltpu.VMEM((HGRN_HEADS, dh, dh), F32),
                        pltpu.VMEM((tn // dh, tm, dh), F32)],
        compiler_params=pltpu.CompilerParams(
            dimension_semantics=("arbitrary",), vmem_limit_bytes=VMEM_LIMIT_BYTES),
        name="mixer",
    )(x1, norm_g.reshape(1, d), cos, sin, lb_logits, ret_gain.reshape(1, width),
      hgrn_gain.reshape(1, width), *([w_in] * 8))


@jax.jit
def kernel(x, ffn1_norm, ffn1_w_gate, ffn1_w_up, ffn1_w_down, mix_norm, w_in, ret_norm_g, hgrn_lb_logits, hgrn_norm_g, w_out, ffn2_norm, ffn2_w_gate, ffn2_w_up, ffn2_w_down, final_norm):
    batch, seq, d = x.shape
    assert ffn1_norm.shape[0] == 1, "single-layer stack"
    t = batch * seq
    ret_width = RET_HEADS * RET_HEAD_DIM
    bf = lambda w: w[0].astype(BF16)

    inv = jnp.power(ROPE_BASE, -jnp.arange(0, RET_HEAD_DIM, 2, dtype=F32) / RET_HEAD_DIM)
    ang = jnp.arange(seq, dtype=F32)[:, None] * inv[None, :]
    cos, sin = jnp.cos(ang), jnp.sin(ang)

    x0 = x.reshape(t, d)
    x1 = _ffn(x0, ffn1_norm[0], bf(ffn1_w_gate), bf(ffn1_w_up), bf(ffn1_w_down), name="ffn1")
    ret, hgo = _mixer(x1, mix_norm[0], bf(w_in), hgrn_lb_logits.astype(F32),
                      ret_norm_g[0].astype(F32), hgrn_norm_g[0].astype(F32), cos, sin, seq)
    wo = bf(w_out)
    x2 = _outproj(x1, ret, hgo, wo[:ret_width], wo[ret_width:])
    out = _ffn(x2, ffn2_norm[0], bf(ffn2_w_gate), bf(ffn2_w_up), bf(ffn2_w_down), name="ffn2",
               final_g=final_norm)
    return out.reshape(batch, seq, d)
```

```python
import functools

import jax
import jax.numpy as jnp
from jax import lax
from jax.experimental import pallas as pl
from jax.experimental.pallas import tpu as pltpu

F32 = jnp.float32
BF16 = jnp.bfloat16

RET_HEADS = 4
RET_HEAD_DIM = 256
HGRN_HEADS = 8
HGRN_HEAD_DIM = 128
ROPE_BASE = 10000.0
EPS = 1e-6
FFN_RESIDUAL_WEIGHT = 0.5

VMEM_LIMIT_BYTES = 56 * 1024 * 1024
BF16_TILE_ROWS = 16

FFN_TM = 1024
FFN_TF = 512
MIXER_TM = 512
OUTPROJ_TM = 512
RET_CHUNK = 256
HGRN_CHUNK = 64
HGRN_SUB = 8


def _dot(a, b):
    return jnp.dot(a, b, preferred_element_type=F32)


def _dot_nt(a, b):
    return lax.dot_general(a, b, (((1,), (1,)), ((), ())), preferred_element_type=F32)


def _dot_tn(a, b):
    return lax.dot_general(a, b, (((0,), (0,)), ((), ())), preferred_element_type=F32)


def _rmsnorm(x, g):
    return x * lax.rsqrt(jnp.mean(x * x, axis=-1, keepdims=True) + EPS) * g


def _silu(x):
    return x * jax.nn.sigmoid(x)


def _ffn_kernel(*refs, final_norm, cast_side, n_ff_steps):
    refs = list(refs)
    x_ref, g_ref, wg_ref, wu_ref, wd_ref = refs[:5]
    rest = refs[5:]
    fg_ref = rest.pop(0) if final_norm else None
    side_in_ref = rest.pop(0) if cast_side else None
    out_ref = rest.pop(0)
    side_out_ref = rest.pop(0) if cast_side else None
    (h_ref,) = rest

    j = pl.program_id(1)

    if cast_side:
        tn = side_out_ref.shape[2]
        for jj in range(side_out_ref.shape[0]):
            side_out_ref[jj] = side_in_ref[:, jj * tn:(jj + 1) * tn].astype(BF16)

    @pl.when(j == 0)
    def _():
        x = x_ref[...]
        out_ref[...] = x
        h_ref[...] = _rmsnorm(x, g_ref[...]).astype(BF16)

    h = h_ref[...]
    gate = _dot(h, wg_ref[...])
    up = _dot(h, wu_ref[...])
    act = (_silu(gate) * up * FFN_RESIDUAL_WEIGHT).astype(BF16)
    out_ref[...] += _dot(act, wd_ref[...])

    if final_norm:
        @pl.when(j == n_ff_steps - 1)
        def _():
            out_ref[...] = _rmsnorm(out_ref[...], fg_ref[...])


def _ffn(x, norm_g, w_gate, w_up, w_down, *, name, final_g=None, w_in_f32=None):
    t, d = x.shape
    d_ff = w_gate.shape[1]
    tm, tf = FFN_TM, FFN_TF
    assert t % tm == 0 and d_ff % tf == 0
    n_blocks = t // tm
    n_ff_steps = d_ff // tf
    row = lambda i, j: (i, 0)
    const = lambda i, j: (0, 0)
    in_specs = [pl.BlockSpec((tm, d), row),
                pl.BlockSpec((1, d), const),
                pl.BlockSpec((d, tf), lambda i, j: (0, j)),
                pl.BlockSpec((d, tf), lambda i, j: (0, j)),
                pl.BlockSpec((tf, d), lambda i, j: (j, 0))]
    args = [x, norm_g.reshape(1, d), w_gate, w_up, w_down]
    out_specs = [pl.BlockSpec((tm, d), row)]
    out_shape = [jax.ShapeDtypeStruct((t, d), F32)]
    if final_g is not None:
        in_specs.append(pl.BlockSpec((1, d), const))
        args.append(final_g.reshape(1, d))
    if w_in_f32 is not None:
        n_groups = 8
        tn = RET_HEAD_DIM
        width = w_in_f32.shape[1] // n_groups
        nj = width // tn
        rows = d // n_blocks
        assert d % n_blocks == 0 and rows % BF16_TILE_ROWS == 0 and n_ff_steps >= n_groups
        group = lambda i, j: (i, jnp.minimum(j, n_groups - 1))
        in_specs.append(pl.BlockSpec((rows, width), group))
        args.append(w_in_f32)
        out_specs.append(pl.BlockSpec((nj, rows, tn), lambda i, j: (0,) + group(i, j)))
        out_shape.append(jax.ShapeDtypeStruct((nj, d, n_groups * tn), BF16))
    kern = functools.partial(_ffn_kernel, final_norm=final_g is not None,
                             cast_side=w_in_f32 is not None, n_ff_steps=n_ff_steps)
    outs = pl.pallas_call(
        kern,
        grid=(n_blocks, n_ff_steps),
        in_specs=in_specs,
        out_specs=out_specs,
        out_shape=out_shape,
        scratch_shapes=[pltpu.VMEM((tm, d), BF16)],
        compiler_params=pltpu.CompilerParams(
            dimension_semantics=("arbitrary", "arbitrary"), vmem_limit_bytes=VMEM_LIMIT_BYTES),
        name=name,
    )(*args)
    return outs if w_in_f32 is not None else outs[0]


def _outproj_kernel(x_ref, ret_ref, hgo_ref, wo_r_ref, wo_h_ref, out_ref):
    out_ref[...] = (x_ref[...] + _dot(ret_ref[...], wo_r_ref[...])
                    + _dot(hgo_ref[...], wo_h_ref[...]))


def _outproj(x1, ret, hgo, w_out):
    t, d = x1.shape
    tm = OUTPROJ_TM
    half = ret.shape[1]
    assert t % tm == 0 and hgo.shape[1] == half and w_out.shape == (2 * half, d)
    row = lambda i: (i, 0)
    return pl.pallas_call(
        _outproj_kernel,
        grid=(t // tm,),
        in_specs=[pl.BlockSpec((tm, d), row),
                  pl.BlockSpec((tm, half), row), pl.BlockSpec((tm, half), row),
                  pl.BlockSpec((half, d), lambda i: (0, 0)), pl.BlockSpec((half, d), lambda i: (1, 0))],
        out_specs=pl.BlockSpec((tm, d), row),
        out_shape=jax.ShapeDtypeStruct((t, d), F32),
        compiler_params=pltpu.CompilerParams(
            dimension_semantics=("parallel",), vmem_limit_bytes=VMEM_LIMIT_BYTES),
        name="outproj",
    )(x1, ret, hgo, w_out, w_out)


class _RetentionHead:
    def __init__(self, q, k, v, head, c):
        tb, dk = q.shape
        self.c = c
        self.chunks = [slice(n * c, (n + 1) * c) for n in range(tb // c)]
        self.q, self.k, self.v = q, k, v

        def log_gamma(shape):
            return jnp.log(1.0 - jnp.exp2(-5.0 - jnp.full(shape, head, jnp.int32).astype(F32)))

        row = lax.broadcasted_iota(jnp.int32, (c, dk), 0).astype(F32)
        rel = (lax.broadcasted_iota(jnp.int32, (c, c), 0) -
               lax.broadcasted_iota(jnp.int32, (c, c), 1)).astype(F32)
        self.decay = jnp.where(rel >= 0, jnp.exp(log_gamma((c, c)) * jnp.maximum(rel, 0.0)), 0.0)
        self.q_dec = jnp.exp(log_gamma((c, dk)) * (row + 1.0))
        self.k_dec = jnp.exp(log_gamma((c, dk)) * (c - 1.0 - row))
        self.g_chunk = jnp.exp(log_gamma((1, dk)) * float(c))

    def free_dots(self):
        self.scores = [_dot_nt(self.q[r], self.k[r]) for r in self.chunks]
        self.updates = [_dot_tn((self.k[r].astype(F32) * self.k_dec).astype(BF16), self.v[r])
                        for r in self.chunks]

    def outputs(self, state):
        states = []
        for upd in self.updates:
            states.append(state.astype(BF16))
            state = self.g_chunk * state + upd
        outs = []
        for r, s, st in zip(self.chunks, self.scores, states):
            inner = _dot((s * self.decay).astype(BF16), self.v[r])
            outs.append(inner + _dot(self.q[r], st) * self.q_dec)
        return outs, state


class _HgrnConsts:
    def __init__(self, c, sub):
        d = HGRN_HEAD_DIM
        pair = BF16_TILE_ROWS
        self.c, self.sub, self.pair = c, sub, pair
        self.tri = (lax.broadcasted_iota(jnp.int32, (c, c), 0) >=
                    lax.broadcasted_iota(jnp.int32, (c, c), 1)).astype(BF16)
        self.pair_lane = lax.broadcasted_iota(jnp.int32, (pair, d), 1) % sub
        a_row = lax.broadcasted_iota(jnp.int32, (c, c), 0)
        a_col = lax.broadcasted_iota(jnp.int32, (c, c), 1)
        self.diag_mask = ((a_row // sub) == (a_col // sub)) & (a_col <= a_row)
        self.halves = []
        h = c // 2
        while h >= sub:
            self.halves.append(h)
            h //= 2
        self.level_masks = [(a_row // (2 * h)) == (a_col // (2 * h)) for h in self.halves]


class _HgrnHead:
    def __init__(self, q, k, v, lf, cp_ref, consts):
        self.q, self.k, self.v, self.lf, self.cp_ref, self.cs = q, k, v, lf, cp_ref, consts
        c = consts.c
        self.chunks = [slice(n * c, (n + 1) * c) for n in range(q.shape[0] // c)]

    def cumsum_dots(self):
        d = HGRN_HEAD_DIM
        self.cums = []
        for rows in self.chunks:
            lf = self.lf[rows]
            lf1 = lf.astype(BF16)
            r1 = lf - lf1.astype(F32)
            lf2 = r1.astype(BF16)
            lf3 = (r1 - lf2.astype(F32)).astype(BF16)
            cum3 = _dot(self.cs.tri, jnp.concatenate([lf1, lf2, lf3], axis=1))
            self.cums.append(cum3[:, :d] + cum3[:, d:2 * d] + cum3[:, 2 * d:])

    def elementwise(self):
        cs, cp_ref = self.cs, self.cp_ref
        c, sub, pair, d = cs.c, cs.sub, cs.pair, HGRN_HEAD_DIM
        self.level_ops, self.a_diags, self.q_ins, self.k_ends, self.decays = [], [], [], [], []
        for rows, cum in zip(self.chunks, self.cums):
            r0 = rows.start
            q = self.q[rows].astype(F32)
            k = self.k[rows].astype(F32)
            cp = cum - jnp.log2(k)
            cp_ref[rows, :] = cp
            total = cum[c - 1:c, :]
            self.q_ins.append((q * jnp.exp2(cum)).astype(BF16))
            self.k_ends.append(jnp.exp2(total - cp).astype(BF16))
            self.decays.append(jnp.exp2(total))

            ops = []
            for h in cs.halves:
                q_parts, k_parts = [], []
                for p0 in range(0, c, 2 * h):
                    lo, up = slice(p0, p0 + h), slice(p0 + h, p0 + 2 * h)
                    c_b = cum[p0 + h - 1:p0 + h, :]
                    q_parts += [jnp.zeros((h, d), F32), q[up] * jnp.exp2(cum[up] - c_b)]
                    k_parts += [jnp.exp2(c_b - cp[lo]), jnp.zeros((h, d), F32)]
                ops.append((jnp.concatenate(q_parts, axis=0).astype(BF16),
                            jnp.concatenate(k_parts, axis=0).astype(BF16)))
            self.level_ops.append(ops)

            pairs = []
            for i0 in range(0, c, pair):
                q_i = q[i0:i0 + pair]
                c_i = cum[i0:i0 + pair]
                a_pair = jnp.zeros((pair, d), F32)
                for j in range(sub):
                    srcs = [cp_ref[r0 + i0 + s + j:r0 + i0 + s + j + 1, :]
                            for s in range(0, pair, sub)]
                    if len(srcs) == 1:
                        cp_j = srcs[0]
                    else:
                        cp_j = jnp.concatenate([jnp.broadcast_to(s, (sub, d)) for s in srcs], axis=0)
                    s_j = jnp.sum(q_i * jnp.exp2(c_i - cp_j), axis=-1, keepdims=True)
                    a_pair = jnp.where(cs.pair_lane == j, s_j, a_pair)
                pairs.append(a_pair)
            self.a_diags.append(jnp.concatenate(pairs, axis=0)[:, :c])

    def free_dots(self):
        self.lows = [[_dot_nt(q_l, k_l) for q_l, k_l in ops] for ops in self.level_ops]
        self.updates = [_dot_tn(self.v[rows], k_end) for rows, k_end in zip(self.chunks, self.k_ends)]

    def outputs(self, state):
        cs = self.cs
        states = []
        for decay, upd in zip(self.decays, self.updates):
            states.append(state.astype(BF16))
            state = decay * state + upd
        outs = []
        for n, rows in enumerate(self.chunks):
            attn = jnp.where(cs.diag_mask, self.a_diags[n], 0.0)
            for h, mask, low in zip(cs.halves, cs.level_masks, self.lows[n]):
                attn = attn + (low if 2 * h == cs.c else jnp.where(mask, low, 0.0))
            outs.append(_dot(attn.astype(BF16), self.v[rows]) + _dot_nt(self.q_ins[n], states[n]))
        return outs, state


_RQ, _RK, _RV, _RG, _HQ, _HK, _HV, _HG = range(8)


def _mixer_kernel(x_ref, g_ref, cos_ref, sin_ref, lbl_ref, rgain_ref, hgain_ref,
                  w_ref, *refs, n_col_blocks, blocks_per_seq, n_side):
    side_in = refs[:n_side]
    ret_ref, hgo_ref = refs[n_side:n_side + 2]
    side_out = refs[n_side + 2:2 * n_side + 2]
    h_ref, act_ref, lf_ref, ret_state_ref, hgrn_state_ref, cp_ref = refs[2 * n_side + 2:]
    for i_ref, o_ref in zip(side_in, side_out):
        o_ref[...] = i_ref[...].astype(BF16)

    s = pl.program_id(0)
    nj = n_col_blocks
    dk, dh = RET_HEAD_DIM, HGRN_HEAD_DIM
    heads_per_step = dk // dh
    cons = jnp.maximum(s - 1, 0)
    cj = cons % nj

    @pl.when(s == 0)
    def _():
        act_ref[...] = jnp.zeros_like(act_ref)
        lf_ref[...] = jnp.zeros_like(lf_ref)

    @pl.when(s % nj == 0)
    def _():
        h_ref[...] = _rmsnorm(x_ref[...], g_ref[...]).astype(BF16)

    @pl.when((cons // nj) % blocks_per_seq == 0)
    def _():
        ret_state_ref[cj] = jnp.zeros((dk, dk), F32)
        for a in range(heads_per_step):
            hgrn_state_ref[heads_per_step * cj + a] = jnp.zeros((dh, dh), F32)

    h = h_ref[...]
    cos = cos_ref[...]
    sin = sin_ref[...]
    half = dk // 2
    dst = act_ref.at[s % 2]
    src = act_ref.at[(s + 1) % 2]
    src_lf = lf_ref.at[(s + 1) % 2]

    def rope(p):
        x1, x2 = p[:, :half], p[:, half:]
        return jnp.concatenate([x1 * cos - x2 * sin, x2 * cos + x1 * sin], axis=-1)

    consts = _HgrnConsts(HGRN_CHUNK, HGRN_SUB)
    hq, hk, hv, hlf = src[_HQ], src[_HK], src[_HV], src_lf[...]
    heads = []
    for a in range(heads_per_step):
        cols = slice(a * dh, (a + 1) * dh)
        heads.append(_HgrnHead(hq[:, cols], hk[:, cols], hv[:, cols], hlf[:, cols],
                               cp_ref.at[a], consts))
    ret = _RetentionHead(src[_RQ], src[_RK], src[_RV], cj, RET_CHUNK)

    def project_pair(first_group):
        p = _dot(h, w_ref[:, first_group * dk:(first_group + 2) * dk])
        return p[:, :dk], p[:, dk:]

    for hd in heads:
        hd.cumsum_dots()
    p_hq, z = project_pair(_HQ)
    dst[_HQ] = _silu(p_hq).astype(BF16)
    lbl = lbl_ref[...]
    m = jnp.maximum(lbl, 0.0)
    e_l = jnp.exp(lbl - m)
    lb = e_l / (e_l + jnp.exp(-m))
    dst[_HK] = ((1.0 - lb) * jax.nn.sigmoid(-z)).astype(BF16)
    lf_ref[s % 2] = jnp.log2(lb + (1.0 - lb) * jax.nn.sigmoid(z))
    for hd in heads:
        hd.elementwise()

    for hd in heads:
        hd.free_dots()
    ret.free_dots()
    p_hv, p_hg = project_pair(_HV)
    dst[_HV] = p_hv.astype(BF16)
    dst[_HG] = _silu(p_hg).astype(BF16)

    hgate = src[_HG]
    for a, hd in enumerate(heads):
        cols = slice(a * dh, (a + 1) * dh)
        outs, state = hd.outputs(hgrn_state_ref[heads_per_step * cj + a])
        hgrn_state_ref[heads_per_step * cj + a] = state
        gain = hgain_ref[:, cols]
        for rows, o in zip(hd.chunks, outs):
            y = o * lax.rsqrt(jnp.mean(o * o, axis=-1, keepdims=True) + EPS)
            hgo_ref[rows, cols] = (y * gain * hgate[rows, cols].astype(F32)).astype(BF16)

    p_rq, p_rk = project_pair(_RQ)
    dst[_RQ] = (rope(p_rq) * (dk ** -0.5)).astype(BF16)
    dst[_RK] = rope(p_rk).astype(BF16)

    rgate = src[_RG]
    outs, state = ret.outputs(ret_state_ref[cj])
    ret_state_ref[cj] = state
    gain = rgain_ref[...]
    for rows, o in zip(ret.chunks, outs):
        mu = jnp.mean(o, axis=-1, keepdims=True)
        oc = o - mu
        var = jnp.mean(oc * oc, axis=-1, keepdims=True)
        ret_ref[rows, :] = (oc * lax.rsqrt(var + EPS) * gain * rgate[rows, :].astype(F32)).astype(BF16)

    p_rv, p_rg = project_pair(_RV)
    dst[_RV] = p_rv.astype(BF16)
    dst[_RG] = _silu(p_rg).astype(BF16)


def _mixer(x1, norm_g, w_units, lb_logits, ret_gain, hgrn_gain, cos, sin, seq, side_f32):
    t, d = x1.shape
    tm = MIXER_TM
    tn = RET_HEAD_DIM
    dh = HGRN_HEAD_DIM
    width = RET_HEADS * RET_HEAD_DIM
    assert w_units.shape == (width // tn, d, 8 * tn) and width == HGRN_HEADS * dh and tn % dh == 0
    assert t % tm == 0 and seq % tm == 0
    assert tm % RET_CHUNK == 0 and tm % HGRN_CHUNK == 0
    assert HGRN_CHUNK % BF16_TILE_ROWS == 0 and BF16_TILE_ROWS % HGRN_SUB == 0
    nj = width // tn
    blocks_per_seq = seq // tm
    n_units = (t // tm) * nj
    prod = lambda s: jnp.minimum(s, n_units - 1)
    cons = lambda s: jnp.maximum(s - 1, 0)
    const = lambda s: (0, 0)
    pos = lambda s: ((prod(s) // nj) % blocks_per_seq, 0)
    cons_col = lambda s: (0, cons(s) % nj)

    side_specs = []
    for w in side_f32:
        rows = w.shape[0] // n_units
        assert w.shape[0] % n_units == 0 and rows % BF16_TILE_ROWS == 0
        side_specs.append(pl.BlockSpec((rows, w.shape[1]), lambda s: (prod(s), 0)))

    out_spec = pl.BlockSpec((tm, tn), lambda s: (cons(s) // nj, cons(s) % nj))
    out_bf = jax.ShapeDtypeStruct((t, width), BF16)
    outs = pl.pallas_call(
        functools.partial(_mixer_kernel, n_col_blocks=nj, blocks_per_seq=blocks_per_seq,
                          n_side=len(side_f32)),
        grid=(n_units + 1,),
        in_specs=[pl.BlockSpec((tm, d), lambda s: (prod(s) // nj, 0)), pl.BlockSpec((1, d), const),
                  pl.BlockSpec((tm, tn // 2), pos), pl.BlockSpec((tm, tn // 2), pos),
                  pl.BlockSpec((1, tn), lambda s: (0, prod(s) % nj)),
                  pl.BlockSpec((1, tn), cons_col), pl.BlockSpec((1, tn), cons_col),
                  pl.BlockSpec((None, d, 8 * tn), lambda s: (prod(s) % nj, 0, 0))] + side_specs,
        out_specs=[out_spec, out_spec] + side_specs,
        out_shape=[out_bf, out_bf] + [jax.ShapeDtypeStruct(w.shape, BF16) for w in side_f32],
        scratch_shapes=[pltpu.VMEM((tm, d), BF16),
                        pltpu.VMEM((2, 8, tm, tn), BF16),
                        pltpu.VMEM((2, tm, tn), F32),
                        pltpu.VMEM((RET_HEADS, RET_HEAD_DIM, RET_HEAD_DIM), F32),
                        pltpu.VMEM((HGRN_HEADS, dh, dh), F32),
                        pltpu.VMEM((tn // dh, tm, dh), F32)],
        compiler_params=pltpu.CompilerParams(
            dimension_semantics=("arbitrary",), vmem_limit_bytes=VMEM_LIMIT_BYTES),
        name="mixer",
    )(x1, norm_g.reshape(1, d), cos, sin, lb_logits, ret_gain.reshape(1, width),
      hgrn_gain.reshape(1, width), w_units, *side_f32)
    return outs[0], outs[1], outs[2:]


@jax.jit
def kernel(x, ffn1_norm, ffn1_w_gate, ffn1_w_up, ffn1_w_down, mix_norm, w_in, ret_norm_g, hgrn_lb_logits, hgrn_norm_g, w_out, ffn2_norm, ffn2_w_gate, ffn2_w_up, ffn2_w_down, final_norm):
    batch, seq, d = x.shape
    assert ffn1_norm.shape[0] == 1, "single-layer stack"
    t = batch * seq
    bf = lambda w: w[0].astype(BF16)

    inv = jnp.power(ROPE_BASE, -jnp.arange(0, RET_HEAD_DIM, 2, dtype=F32) / RET_HEAD_DIM)
    ang = jnp.arange(seq, dtype=F32)[:, None] * inv[None, :]
    cos, sin = jnp.cos(ang), jnp.sin(ang)

    x0 = x.reshape(t, d)
    x1, w_units = _ffn(x0, ffn1_norm[0], bf(ffn1_w_gate), bf(ffn1_w_up), bf(ffn1_w_down),
                       name="ffn1", w_in_f32=w_in[0].astype(F32))
    d_ff = ffn2_w_gate.shape[2]
    side = [ffn2_w_gate[0].astype(F32), ffn2_w_up[0].astype(F32),
            ffn2_w_down[0].astype(F32).reshape(d, d_ff),
            w_out[0].astype(F32)]
    ret, hgo, (wg2, wu2, wd2, wo) = _mixer(
        x1, mix_norm[0], w_units, hgrn_lb_logits.astype(F32),
        ret_norm_g[0].astype(F32), hgrn_norm_g[0].astype(F32), cos, sin, seq, side)
    x2 = _outproj(x1, ret, hgo, wo)
    out = _ffn(x2, ffn2_norm[0], wg2, wu2, wd2.reshape(d_ff, d), name="ffn2", final_g=final_norm)
    return out.reshape(batch, seq, d)
```

```python
import functools

import jax
import jax.numpy as jnp
from jax import lax
from jax.experimental import pallas as pl
from jax.experimental.pallas import tpu as pltpu

F32 = jnp.float32
BF16 = jnp.bfloat16

RET_HEADS = 4
RET_HEAD_DIM = 256
HGRN_HEADS = 8
HGRN_HEAD_DIM = 128
ROPE_BASE = 10000.0
EPS = 1e-6
FFN_RESIDUAL_WEIGHT = 0.5

VMEM_LIMIT_BYTES = 56 * 1024 * 1024
BF16_TILE_ROWS = 16

FFN_TM = 1024
FFN_TF = 512
MIXER_TM = 512
OUTPROJ_TM = 512
RET_CHUNK = 256
HGRN_CHUNK = 64
HGRN_SUB = 8


def _dot(a, b):
    return jnp.dot(a, b, preferred_element_type=F32)


def _dot_nt(a, b):
    return lax.dot_general(a, b, (((1,), (1,)), ((), ())), preferred_element_type=F32)


def _dot_tn(a, b):
    return lax.dot_general(a, b, (((0,), (0,)), ((), ())), preferred_element_type=F32)


def _rmsnorm(x, g):
    return x * lax.rsqrt(jnp.mean(x * x, axis=-1, keepdims=True) + EPS) * g


def _silu(x):
    return x * jax.nn.sigmoid(x)


def _ffn_kernel(*refs, final_norm, cast_side, n_ff_steps):
    refs = list(refs)
    x_ref, g_ref, wg_ref, wu_ref, wd_ref = refs[:5]
    rest = refs[5:]
    fg_ref = rest.pop(0) if final_norm else None
    side_in_ref = rest.pop(0) if cast_side else None
    out_ref = rest.pop(0)
    side_out_ref = rest.pop(0) if cast_side else None
    (h_ref,) = rest

    j = pl.program_id(1)

    if cast_side:
        tn = side_out_ref.shape[2]
        for jj in range(side_out_ref.shape[0]):
            side_out_ref[jj] = side_in_ref[:, jj * tn:(jj + 1) * tn].astype(BF16)

    @pl.when(j == 0)
    def _():
        x = x_ref[...]
        out_ref[...] = x
        h_ref[...] = _rmsnorm(x, g_ref[...]).astype(BF16)

    h = h_ref[...]
    gate = _dot(h, wg_ref[...])
    up = _dot(h, wu_ref[...])
    act = (_silu(gate) * up * FFN_RESIDUAL_WEIGHT).astype(BF16)
    out_ref[...] += _dot(act, wd_ref[...])

    if final_norm:
        @pl.when(j == n_ff_steps - 1)
        def _():
            out_ref[...] = _rmsnorm(out_ref[...], fg_ref[...])


def _ffn(x, norm_g, w_gate, w_up, w_down, *, name, final_g=None, w_in_f32=None):
    t, d = x.shape
    d_ff = w_gate.shape[1]
    tm, tf = FFN_TM, FFN_TF
    assert t % tm == 0 and d_ff % tf == 0
    n_blocks = t // tm
    n_ff_steps = d_ff // tf
    row = lambda i, j: (i, 0)
    const = lambda i, j: (0, 0)
    in_specs = [pl.BlockSpec((tm, d), row),
                pl.BlockSpec((1, d), const),
                pl.BlockSpec((d, tf), lambda i, j: (0, j)),
                pl.BlockSpec((d, tf), lambda i, j: (0, j)),
                pl.BlockSpec((tf, d), lambda i, j: (j, 0))]
    args = [x, norm_g.reshape(1, d), w_gate, w_up, w_down]
    out_specs = [pl.BlockSpec((tm, d), row)]
    out_shape = [jax.ShapeDtypeStruct((t, d), F32)]
    if final_g is not None:
        in_specs.append(pl.BlockSpec((1, d), const))
        args.append(final_g.reshape(1, d))
    if w_in_f32 is not None:
        n_groups = 8
        tn = RET_HEAD_DIM
        width = w_in_f32.shape[1] // n_groups
        nj = width // tn
        rows = d // n_blocks
        assert d % n_blocks == 0 and rows % BF16_TILE_ROWS == 0 and n_ff_steps >= n_groups
        group = lambda i, j: (i, jnp.minimum(j, n_groups - 1))
        in_specs.append(pl.BlockSpec((rows, width), group))
        args.append(w_in_f32)
        out_specs.append(pl.BlockSpec((nj, rows, tn), lambda i, j: (0,) + group(i, j)))
        out_shape.append(jax.ShapeDtypeStruct((nj, d, n_groups * tn), BF16))
    kern = functools.partial(_ffn_kernel, final_norm=final_g is not None,
                             cast_side=w_in_f32 is not None, n_ff_steps=n_ff_steps)
    outs = pl.pallas_call(
        kern,
        grid=(n_blocks, n_ff_steps),
        in_specs=in_specs,
        out_specs=out_specs,
        out_shape=out_shape,
        scratch_shapes=[pltpu.VMEM((tm, d), BF16)],
        compiler_params=pltpu.CompilerParams(
            dimension_semantics=("arbitrary", "arbitrary"), vmem_limit_bytes=VMEM_LIMIT_BYTES),
        name=name,
    )(*args)
    return outs if w_in_f32 is not None else outs[0]


def _outproj_kernel(x_ref, ret_ref, hgo_ref, wo_r_ref, wo_h_ref, out_ref):
    out_ref[...] = (x_ref[...] + _dot(ret_ref[...], wo_r_ref[...])
                    + _dot(hgo_ref[...], wo_h_ref[...]))


def _outproj(x1, ret, hgo, w_out):
    t, d = x1.shape
    tm = OUTPROJ_TM
    half = ret.shape[1]
    assert t % tm == 0 and hgo.shape[1] == half and w_out.shape == (2 * half, d)
    row = lambda i: (i, 0)
    return pl.pallas_call(
        _outproj_kernel,
        grid=(t // tm,),
        in_specs=[pl.BlockSpec((tm, d), row),
                  pl.BlockSpec((tm, half), row), pl.BlockSpec((tm, half), row),
                  pl.BlockSpec((half, d), lambda i: (0, 0)), pl.BlockSpec((half, d), lambda i: (1, 0))],
        out_specs=pl.BlockSpec((tm, d), row),
        out_shape=jax.ShapeDtypeStruct((t, d), F32),
        compiler_params=pltpu.CompilerParams(
            dimension_semantics=("parallel",), vmem_limit_bytes=VMEM_LIMIT_BYTES),
        name="outproj",
    )(x1, ret, hgo, w_out, w_out)


class _RetentionHead:
    def __init__(self, q, k, v, head, c):
        tb, dk = q.shape
        self.c = c
        self.chunks = [slice(n * c, (n + 1) * c) for n in range(tb // c)]
        self.q, self.k, self.v = q, k, v

        def log_gamma(shape):
            return jnp.log(1.0 - jnp.exp2(-5.0 - jnp.full(shape, head, jnp.int32).astype(F32)))

        row = lax.broadcasted_iota(jnp.int32, (c, dk), 0).astype(F32)
        rel = (lax.broadcasted_iota(jnp.int32, (c, c), 0) -
               lax.broadcasted_iota(jnp.int32, (c, c), 1)).astype(F32)
        self.decay = jnp.where(rel >= 0, jnp.exp(log_gamma((c, c)) * jnp.maximum(rel, 0.0)), 0.0)
        self.q_dec = jnp.exp(log_gamma((c, dk)) * (row + 1.0))
        self.k_dec = jnp.exp(log_gamma((c, dk)) * (c - 1.0 - row))
        self.g_chunk = jnp.exp(log_gamma((1, dk)) * float(c))

    def free_dots(self):
        self.scores = [_dot_nt(self.q[r], self.k[r]) for r in self.chunks]
        self.updates = [_dot_tn((self.k[r].astype(F32) * self.k_dec).astype(BF16), self.v[r])
                        for r in self.chunks]

    def outputs(self, state):
        states = []
        for upd in self.updates:
            states.append(state.astype(BF16))
            state = self.g_chunk * state + upd
        outs = []
        for r, s, st in zip(self.chunks, self.scores, states):
            inner = _dot((s * self.decay).astype(BF16), self.v[r])
            outs.append(inner + _dot(self.q[r], st) * self.q_dec)
        return outs, state


class _HgrnConsts:
    def __init__(self, c, sub):
        d = HGRN_HEAD_DIM
        pair = BF16_TILE_ROWS
        self.c, self.sub, self.pair = c, sub, pair
        self.tri = (lax.broadcasted_iota(jnp.int32, (c, c), 0) >=
                    lax.broadcasted_iota(jnp.int32, (c, c), 1)).astype(BF16)
        self.pair_lane = lax.broadcasted_iota(jnp.int32, (pair, d), 1) % sub
        a_row = lax.broadcasted_iota(jnp.int32, (c, c), 0)
        a_col = lax.broadcasted_iota(jnp.int32, (c, c), 1)
        self.diag_mask = ((a_row // sub) == (a_col // sub)) & (a_col <= a_row)
        self.halves = []
        h = c // 2
        while h >= sub:
            self.halves.append(h)
            h //= 2
        self.level_masks = [(a_row // (2 * h)) == (a_col // (2 * h)) for h in self.halves]


class _HgrnHead:
    def __init__(self, q, k, v, lf, cp_ref, consts):
        self.q, self.k, self.v, self.lf, self.cp_ref, self.cs = q, k, v, lf, cp_ref, consts
        c = consts.c
        self.chunks = [slice(n * c, (n + 1) * c) for n in range(q.shape[0] // c)]

    def cumsum_dots(self):
        d = HGRN_HEAD_DIM
        self.cums = []
        for rows in self.chunks:
            lf = self.lf[rows]
            lf1 = lf.astype(BF16)
            r1 = lf - lf1.astype(F32)
            lf2 = r1.astype(BF16)
            lf3 = (r1 - lf2.astype(F32)).astype(BF16)
            cum3 = _dot(self.cs.tri, jnp.concatenate([lf1, lf2, lf3], axis=1))
            self.cums.append(cum3[:, :d] + cum3[:, d:2 * d] + cum3[:, 2 * d:])

    def elementwise(self):
        cs, cp_ref = self.cs, self.cp_ref
        c, sub, pair, d = cs.c, cs.sub, cs.pair, HGRN_HEAD_DIM
        self.level_ops, self.a_diags, self.q_ins, self.k_ends, self.decays = [], [], [], [], []
        for rows, cum in zip(self.chunks, self.cums):
            r0 = rows.start
            q = self.q[rows].astype(F32)
            k = self.k[rows].astype(F32)
            cp = cum - jnp.log2(k)
            cp_ref[rows, :] = cp
            total = cum[c - 1:c, :]
            self.q_ins.append((q * jnp.exp2(cum)).astype(BF16))
            self.k_ends.append(jnp.exp2(total - cp).astype(BF16))
            self.decays.append(jnp.exp2(total))

            ops = []
            for h in cs.halves:
                q_parts, k_parts = [], []
                for p0 in range(0, c, 2 * h):
                    lo, up = slice(p0, p0 + h), slice(p0 + h, p0 + 2 * h)
                    c_b = cum[p0 + h - 1:p0 + h, :]
                    q_parts += [jnp.zeros((h, d), F32), q[up] * jnp.exp2(cum[up] - c_b)]
                    k_parts += [jnp.exp2(c_b - cp[lo]), jnp.zeros((h, d), F32)]
                ops.append((jnp.concatenate(q_parts, axis=0).astype(BF16),
                            jnp.concatenate(k_parts, axis=0).astype(BF16)))
            self.level_ops.append(ops)

            pairs = []
            for i0 in range(0, c, pair):
                q_i = q[i0:i0 + pair]
                c_i = cum[i0:i0 + pair]
                a_pair = jnp.zeros((pair, d), F32)
                for j in range(sub):
                    srcs = [cp_ref[r0 + i0 + s + j:r0 + i0 + s + j + 1, :]
                            for s in range(0, pair, sub)]
                    if len(srcs) == 1:
                        cp_j = srcs[0]
                    else:
                        cp_j = jnp.concatenate([jnp.broadcast_to(s, (sub, d)) for s in srcs], axis=0)
                    s_j = jnp.sum(q_i * jnp.exp2(c_i - cp_j), axis=-1, keepdims=True)
                    a_pair = jnp.where(cs.pair_lane == j, s_j, a_pair)
                pairs.append(a_pair)
            self.a_diags.append(jnp.concatenate(pairs, axis=0)[:, :c])

    def free_dots(self):
        self.lows = [[_dot_nt(q_l, k_l) for q_l, k_l in ops] for ops in self.level_ops]
        self.updates = [_dot_tn(self.v[rows], k_end) for rows, k_end in zip(self.chunks, self.k_ends)]

    def outputs(self, state):
        cs = self.cs
        states = []
        for decay, upd in zip(self.decays, self.updates):
            states.append(state.astype(BF16))
            state = decay * state + upd
        outs = []
        for n, rows in enumerate(self.chunks):
            attn = jnp.where(cs.diag_mask, self.a_diags[n], 0.0)
            for h, mask, low in zip(cs.halves, cs.level_masks, self.lows[n]):
                attn = attn + (low if 2 * h == cs.c else jnp.where(mask, low, 0.0))
            outs.append(_dot(attn.astype(BF16), self.v[rows]) + _dot_nt(self.q_ins[n], states[n]))
        return outs, state


_RQ, _RK, _RV, _RG, _HQ, _HK, _HV, _HG = range(8)


def _mixer_kernel(x_ref, g_ref, cos_sin_ref, lbl_ref, rgain_ref, hgain_ref,
                  w_ref, *refs, n_col_blocks, blocks_per_seq, n_side):
    side_in = refs[:n_side]
    ret_ref, hgo_ref = refs[n_side:n_side + 2]
    side_out = refs[n_side + 2:2 * n_side + 2]
    h_ref, act_ref, lf_ref, ret_state_ref, hgrn_state_ref, cp_ref = refs[2 * n_side + 2:]
    for i_ref, o_ref in zip(side_in, side_out):
        o_ref[...] = i_ref[...].astype(BF16)

    s = pl.program_id(0)
    nj = n_col_blocks
    dk, dh = RET_HEAD_DIM, HGRN_HEAD_DIM
    heads_per_step = dk // dh
    cons = jnp.maximum(s - 1, 0)
    cj = cons % nj

    @pl.when(s == 0)
    def _():
        act_ref[...] = jnp.zeros_like(act_ref)
        lf_ref[...] = jnp.zeros_like(lf_ref)

    @pl.when(s % nj == 0)
    def _():
        h_ref[...] = _rmsnorm(x_ref[...], g_ref[...]).astype(BF16)

    @pl.when((cons // nj) % blocks_per_seq == 0)
    def _():
        ret_state_ref[cj] = jnp.zeros((dk, dk), F32)
        for a in range(heads_per_step):
            hgrn_state_ref[heads_per_step * cj + a] = jnp.zeros((dh, dh), F32)

    h = h_ref[...]
    half = dk // 2
    cos = cos_sin_ref[:, :half]
    sin = cos_sin_ref[:, half:]
    dst = act_ref.at[s % 2]
    src = act_ref.at[(s + 1) % 2]
    src_lf = lf_ref.at[(s + 1) % 2]

    def rope(p):
        x1, x2 = p[:, :half], p[:, half:]
        return jnp.concatenate([x1 * cos - x2 * sin, x2 * cos + x1 * sin], axis=-1)

    consts = _HgrnConsts(HGRN_CHUNK, HGRN_SUB)
    hq, hk, hv, hlf = src[_HQ], src[_HK], src[_HV], src_lf[...]
    heads = []
    for a in range(heads_per_step):
        cols = slice(a * dh, (a + 1) * dh)
        heads.append(_HgrnHead(hq[:, cols], hk[:, cols], hv[:, cols], hlf[:, cols],
                               cp_ref.at[a], consts))
    ret = _RetentionHead(src[_RQ], src[_RK], src[_RV], cj, RET_CHUNK)

    def project_pair(first_group):
        p = _dot(h, w_ref[:, first_group * dk:(first_group + 2) * dk])
        return p[:, :dk], p[:, dk:]

    for hd in heads:
        hd.cumsum_dots()
    p_hq, z = project_pair(_HQ)
    dst[_HQ] = _silu(p_hq).astype(BF16)
    lbl = lbl_ref[...]
    m = jnp.maximum(lbl, 0.0)
    e_l = jnp.exp(lbl - m)
    lb = e_l / (e_l + jnp.exp(-m))
    dst[_HK] = ((1.0 - lb) * jax.nn.sigmoid(-z)).astype(BF16)
    lf_ref[s % 2] = jnp.log2(lb + (1.0 - lb) * jax.nn.sigmoid(z))
    for hd in heads:
        hd.elementwise()

    for hd in heads:
        hd.free_dots()
    ret.free_dots()
    p_hv, p_hg = project_pair(_HV)
    dst[_HV] = p_hv.astype(BF16)
    dst[_HG] = _silu(p_hg).astype(BF16)

    hgate = src[_HG]
    for a, hd in enumerate(heads):
        cols = slice(a * dh, (a + 1) * dh)
        outs, state = hd.outputs(hgrn_state_ref[heads_per_step * cj + a])
        hgrn_state_ref[heads_per_step * cj + a] = state
        gain = hgain_ref[:, cols]
        for rows, o in zip(hd.chunks, outs):
            y = o * lax.rsqrt(jnp.mean(o * o, axis=-1, keepdims=True) + EPS)
            hgo_ref[rows, cols] = (y * gain * hgate[rows, cols].astype(F32)).astype(BF16)

    p_rq, p_rk = project_pair(_RQ)
    dst[_RQ] = (rope(p_rq) * (dk ** -0.5)).astype(BF16)
    dst[_RK] = rope(p_rk).astype(BF16)

    rgate = src[_RG]
    outs, state = ret.outputs(ret_state_ref[cj])
    ret_state_ref[cj] = state
    gain = rgain_ref[...]
    for rows, o in zip(ret.chunks, outs):
        mu = jnp.mean(o, axis=-1, keepdims=True)
        oc = o - mu
        var = jnp.mean(oc * oc, axis=-1, keepdims=True)
        ret_ref[rows, :] = (oc * lax.rsqrt(var + EPS) * gain * rgate[rows, :].astype(F32)).astype(BF16)

    p_rv, p_rg = project_pair(_RV)
    dst[_RV] = p_rv.astype(BF16)
    dst[_RG] = _silu(p_rg).astype(BF16)


def _mixer(x1, norm_g, w_units, lb_logits, ret_gain, hgrn_gain, cos_sin, seq, side_f32):
    t, d = x1.shape
    tm = MIXER_TM
    tn = RET_HEAD_DIM
    dh = HGRN_HEAD_DIM
    width = RET_HEADS * RET_HEAD_DIM
    assert w_units.shape == (width // tn, d, 8 * tn) and width == HGRN_HEADS * dh and tn % dh == 0
    assert t % tm == 0 and seq % tm == 0
    assert tm % RET_CHUNK == 0 and tm % HGRN_CHUNK == 0
    assert HGRN_CHUNK % BF16_TILE_ROWS == 0 and BF16_TILE_ROWS % HGRN_SUB == 0
    nj = width // tn
    blocks_per_seq = seq // tm
    n_units = (t // tm) * nj
    prod = lambda s: jnp.minimum(s, n_units - 1)
    cons = lambda s: jnp.maximum(s - 1, 0)
    const = lambda s: (0, 0)
    pos = lambda s: ((prod(s) // nj) % blocks_per_seq, 0)
    cons_col = lambda s: (0, cons(s) % nj)

    side_specs = []
    for w in side_f32:
        rows = next(r for r in range(BF16_TILE_ROWS, w.shape[0] + 1, BF16_TILE_ROWS)
                    if w.shape[0] % r == 0 and w.shape[0] // r <= n_units)
        side_specs.append(pl.BlockSpec(
            (rows, w.shape[1]), lambda s, last=w.shape[0] // rows - 1: (jnp.minimum(s, last), 0)))

    out_spec = pl.BlockSpec((tm, tn), lambda s: (cons(s) // nj, cons(s) % nj))
    out_bf = jax.ShapeDtypeStruct((t, width), BF16)
    outs = pl.pallas_call(
        functools.partial(_mixer_kernel, n_col_blocks=nj, blocks_per_seq=blocks_per_seq,
                          n_side=len(side_f32)),
        grid=(n_units + 1,),
        in_specs=[pl.BlockSpec((tm, d), lambda s: (prod(s) // nj, 0)), pl.BlockSpec((1, d), const),
                  pl.BlockSpec((tm, tn), pos),
                  pl.BlockSpec((1, tn), lambda s: (0, prod(s) % nj)),
                  pl.BlockSpec((1, tn), cons_col), pl.BlockSpec((1, tn), cons_col),
                  pl.BlockSpec((None, d, 8 * tn), lambda s: (prod(s) % nj, 0, 0))] + side_specs,
        out_specs=[out_spec, out_spec] + side_specs,
        out_shape=[out_bf, out_bf] + [jax.ShapeDtypeStruct(w.shape, BF16) for w in side_f32],
        scratch_shapes=[pltpu.VMEM((tm, d), BF16),
                        pltpu.VMEM((2, 8, tm, tn), BF16),
                        pltpu.VMEM((2, tm, tn), F32),
                        pltpu.VMEM((RET_HEADS, RET_HEAD_DIM, RET_HEAD_DIM), F32),
                        pltpu.VMEM((HGRN_HEADS, dh, dh), F32),
                        pltpu.VMEM((tn // dh, tm, dh), F32)],
        compiler_params=pltpu.CompilerParams(
            dimension_semantics=("arbitrary",), vmem_limit_bytes=VMEM_LIMIT_BYTES),
        name="mixer",
    )(x1, norm_g.reshape(1, d), cos_sin, lb_logits, ret_gain.reshape(1, width),
      hgrn_gain.reshape(1, width), w_units, *side_f32)
    return outs[0], outs[1], outs[2:]


@jax.jit
def kernel(x, ffn1_norm, ffn1_w_gate, ffn1_w_up, ffn1_w_down, mix_norm, w_in, ret_norm_g, hgrn_lb_logits, hgrn_norm_g, w_out, ffn2_norm, ffn2_w_gate, ffn2_w_up, ffn2_w_down, final_norm):
    batch, seq, d = x.shape
    assert ffn1_norm.shape[0] == 1, "single-layer stack"
    t = batch * seq
    bf = lambda w: w[0].astype(BF16)

    inv = jnp.power(ROPE_BASE, -jnp.arange(0, RET_HEAD_DIM, 2, dtype=F32) / RET_HEAD_DIM)
    ang = jnp.arange(seq, dtype=F32)[:, None] * inv[None, :]
    cos_sin = jnp.concatenate([jnp.cos(ang), jnp.sin(ang)], axis=1)

    x0 = x.reshape(t, d)
    x1, w_units = _ffn(x0, ffn1_norm[0], bf(ffn1_w_gate), bf(ffn1_w_up), bf(ffn1_w_down),
                       name="ffn1", w_in_f32=w_in[0].astype(F32))
    side = [ffn2_w_gate[0].astype(F32), ffn2_w_up[0].astype(F32), ffn2_w_down[0].astype(F32),
            w_out[0].astype(F32)]
    ret, hgo, (wg2, wu2, wd2, wo) = _mixer(
        x1, mix_norm[0], w_units, hgrn_lb_logits.astype(F32),
        ret_norm_g[0].astype(F32), hgrn_norm_g[0].astype(F32), cos_sin, seq, side)
    x2 = _outproj(x1, ret, hgo, wo)
    out = _ffn(x2, ffn2_norm[0], wg2, wu2, wd2, name="ffn2", final_g=final_norm)
    return out.reshape(batch, seq, d)
```

```python
import functools

import jax
import jax.numpy as jnp
import numpy as np
from jax import lax
from jax.experimental import pallas as pl
from jax.experimental.pallas import tpu as pltpu

F32 = jnp.float32
BF16 = jnp.bfloat16

RET_HEADS = 4
RET_HEAD_DIM = 256
HGRN_HEADS = 8
HGRN_HEAD_DIM = 128
ROPE_BASE = 10000.0
EPS = 1e-6
FFN_RESIDUAL_WEIGHT = 0.5

VMEM_LIMIT_BYTES = 56 * 1024 * 1024
BF16_TILE_ROWS = 16

FFN_TM = 1024
FFN_TF = 512
MIXER_TM = 512
OUTPROJ_TM = 512
RET_CHUNK = 256
HGRN_CHUNK = 64
HGRN_SUB = 8


def _dot(a, b):
    return jnp.dot(a, b, preferred_element_type=F32)


def _dot_nt(a, b):
    return lax.dot_general(a, b, (((1,), (1,)), ((), ())), preferred_element_type=F32)


def _dot_tn(a, b):
    return lax.dot_general(a, b, (((0,), (0,)), ((), ())), preferred_element_type=F32)


def _rmsnorm(x, g):
    return x * lax.rsqrt(jnp.mean(x * x, axis=-1, keepdims=True) + EPS) * g


def _silu(x):
    return x * jax.nn.sigmoid(x)


def _ffn_kernel(*refs, final_norm, cast_side, n_ff_steps):
    refs = list(refs)
    x_ref, g_ref, wg_ref, wu_ref, wd_ref = refs[:5]
    rest = refs[5:]
    fg_ref = rest.pop(0) if final_norm else None
    side_in_ref = rest.pop(0) if cast_side else None
    out_ref = rest.pop(0)
    side_out_ref = rest.pop(0) if cast_side else None
    (h_ref,) = rest

    j = pl.program_id(1)

    if cast_side:
        tn = side_out_ref.shape[2]
        for jj in range(side_out_ref.shape[0]):
            side_out_ref[jj] = side_in_ref[:, jj * tn:(jj + 1) * tn].astype(BF16)

    @pl.when(j == 0)
    def _():
        x = x_ref[...]
        out_ref[...] = x
        h_ref[...] = _rmsnorm(x, g_ref[...]).astype(BF16)

    h = h_ref[...]
    gate = _dot(h, wg_ref[...])
    up = _dot(h, wu_ref[...])
    act = (_silu(gate) * up * FFN_RESIDUAL_WEIGHT).astype(BF16)
    out_ref[...] += _dot(act, wd_ref[...])

    if final_norm:
        @pl.when(j == n_ff_steps - 1)
        def _():
            out_ref[...] = _rmsnorm(out_ref[...], fg_ref[...])


def _ffn(x, norm_g, w_gate, w_up, w_down, *, name, final_g=None, w_in_f32=None):
    t, d = x.shape
    d_ff = w_gate.shape[1]
    tm, tf = FFN_TM, FFN_TF
    assert t % tm == 0 and d_ff % tf == 0
    n_blocks = t // tm
    n_ff_steps = d_ff // tf
    row = lambda i, j: (i, 0)
    const = lambda i, j: (0, 0)
    in_specs = [pl.BlockSpec((tm, d), row),
                pl.BlockSpec((1, d), const),
                pl.BlockSpec((d, tf), lambda i, j: (0, j)),
                pl.BlockSpec((d, tf), lambda i, j: (0, j)),
                pl.BlockSpec((tf, d), lambda i, j: (j, 0))]
    args = [x, norm_g.reshape(1, d), w_gate, w_up, w_down]
    out_specs = [pl.BlockSpec((tm, d), row)]
    out_shape = [jax.ShapeDtypeStruct((t, d), F32)]
    if final_g is not None:
        in_specs.append(pl.BlockSpec((1, d), const))
        args.append(final_g.reshape(1, d))
    if w_in_f32 is not None:
        n_groups = 8
        tn = RET_HEAD_DIM
        width = w_in_f32.shape[1] // n_groups
        nj = width // tn
        rows = d // n_blocks
        assert d % n_blocks == 0 and rows % BF16_TILE_ROWS == 0 and n_ff_steps >= n_groups
        group = lambda i, j: (i, jnp.minimum(j, n_groups - 1))
        in_specs.append(pl.BlockSpec((rows, width), group))
        args.append(w_in_f32)
        out_specs.append(pl.BlockSpec((nj, None, rows, tn),
                                      lambda i, j: (0, jnp.minimum(j, n_groups - 1), i, 0)))
        out_shape.append(jax.ShapeDtypeStruct((nj, n_groups, d, tn), BF16))
    kern = functools.partial(_ffn_kernel, final_norm=final_g is not None,
                             cast_side=w_in_f32 is not None, n_ff_steps=n_ff_steps)
    outs = pl.pallas_call(
        kern,
        grid=(n_blocks, n_ff_steps),
        in_specs=in_specs,
        out_specs=out_specs,
        out_shape=out_shape,
        scratch_shapes=[pltpu.VMEM((tm, d), BF16)],
        compiler_params=pltpu.CompilerParams(
            dimension_semantics=("arbitrary", "arbitrary"), vmem_limit_bytes=VMEM_LIMIT_BYTES),
        name=name,
    )(*args)
    return outs if w_in_f32 is not None else outs[0]


def _outproj_kernel(x_ref, ret_ref, hgo_ref, wo_r_ref, wo_h_ref, out_ref):
    out_ref[...] = (x_ref[...] + _dot(ret_ref[...], wo_r_ref[...])
                    + _dot(hgo_ref[...], wo_h_ref[...]))


def _outproj(x1, ret, hgo, w_out):
    t, d = x1.shape
    tm = OUTPROJ_TM
    half = ret.shape[1]
    assert t % tm == 0 and hgo.shape[1] == half and w_out.shape == (2 * half, d)
    row = lambda i: (i, 0)
    return pl.pallas_call(
        _outproj_kernel,
        grid=(t // tm,),
        in_specs=[pl.BlockSpec((tm, d), row),
                  pl.BlockSpec((tm, half), row), pl.BlockSpec((tm, half), row),
                  pl.BlockSpec((half, d), lambda i: (0, 0)), pl.BlockSpec((half, d), lambda i: (1, 0))],
        out_specs=pl.BlockSpec((tm, d), row),
        out_shape=jax.ShapeDtypeStruct((t, d), F32),
        compiler_params=pltpu.CompilerParams(
            dimension_semantics=("parallel",), vmem_limit_bytes=VMEM_LIMIT_BYTES),
        name="outproj",
    )(x1, ret, hgo, w_out, w_out)


class _RetentionHead:
    def __init__(self, q, k, v, head, c):
        tb, dk = q.shape
        self.c = c
        self.chunks = [slice(n * c, (n + 1) * c) for n in range(tb // c)]
        self.q, self.k, self.v = q, k, v

        def log_gamma(shape):
            return jnp.log(1.0 - jnp.exp2(-5.0 - jnp.full(shape, head, jnp.int32).astype(F32)))

        row = lax.broadcasted_iota(jnp.int32, (c, dk), 0).astype(F32)
        rel = (lax.broadcasted_iota(jnp.int32, (c, c), 0) -
               lax.broadcasted_iota(jnp.int32, (c, c), 1)).astype(F32)
        self.decay = jnp.where(rel >= 0, jnp.exp(log_gamma((c, c)) * jnp.maximum(rel, 0.0)), 0.0)
        self.q_dec = jnp.exp(log_gamma((c, dk)) * (row + 1.0))
        self.k_dec = jnp.exp(log_gamma((c, dk)) * (c - 1.0 - row))
        self.g_chunk = jnp.exp(log_gamma((1, dk)) * float(c))

    def free_dots(self):
        self.scores = [_dot_nt(self.q[r], self.k[r]) for r in self.chunks]
        self.updates = [_dot_tn((self.k[r].astype(F32) * self.k_dec).astype(BF16), self.v[r])
                        for r in self.chunks]

    def outputs(self, state):
        states = []
        for upd in self.updates:
            states.append(state.astype(BF16))
            state = self.g_chunk * state + upd
        outs = []
        for r, s, st in zip(self.chunks, self.scores, states):
            inner = _dot((s * self.decay).astype(BF16), self.v[r])
            outs.append(inner + _dot(self.q[r], st) * self.q_dec)
        return outs, state


class _HgrnConsts:
    def __init__(self, c, sub):
        d = HGRN_HEAD_DIM
        pair = BF16_TILE_ROWS
        self.c, self.sub, self.pair = c, sub, pair
        self.tri = (lax.broadcasted_iota(jnp.int32, (c, c), 0) >=
                    lax.broadcasted_iota(jnp.int32, (c, c), 1)).astype(BF16)
        self.pair_lane = lax.broadcasted_iota(jnp.int32, (pair, d), 1) % sub
        a_row = lax.broadcasted_iota(jnp.int32, (c, c), 0)
        a_col = lax.broadcasted_iota(jnp.int32, (c, c), 1)
        self.diag_mask = ((a_row // sub) == (a_col // sub)) & (a_col <= a_row)
        self.halves = []
        h = c // 2
        while h >= sub:
            self.halves.append(h)
            h //= 2
        self.level_masks = [(a_row // (2 * h)) == (a_col // (2 * h)) for h in self.halves]


class _HgrnHead:
    def __init__(self, q, k, v, lf, cp_ref, consts):
        self.q, self.k, self.v, self.lf, self.cp_ref, self.cs = q, k, v, lf, cp_ref, consts
        c = consts.c
        self.chunks = [slice(n * c, (n + 1) * c) for n in range(q.shape[0] // c)]

    def cumsum_dots(self):
        d = HGRN_HEAD_DIM
        self.cums = []
        for rows in self.chunks:
            lf = self.lf[rows]
            lf1 = lf.astype(BF16)
            r1 = lf - lf1.astype(F32)
            lf2 = r1.astype(BF16)
            lf3 = (r1 - lf2.astype(F32)).astype(BF16)
            cum3 = _dot(self.cs.tri, jnp.concatenate([lf1, lf2, lf3], axis=1))
            self.cums.append(cum3[:, :d] + cum3[:, d:2 * d] + cum3[:, 2 * d:])

    def elementwise(self):
        cs, cp_ref = self.cs, self.cp_ref
        c, sub, pair, d = cs.c, cs.sub, cs.pair, HGRN_HEAD_DIM
        self.level_ops, self.a_diags, self.q_ins, self.k_ends, self.decays = [], [], [], [], []
        for rows, cum in zip(self.chunks, self.cums):
            r0 = rows.start
            q = self.q[rows].astype(F32)
            k = self.k[rows].astype(F32)
            cp = cum - jnp.log2(k)
            cp_ref[rows, :] = cp
            total = cum[c - 1:c, :]
            self.q_ins.append((q * jnp.exp2(cum)).astype(BF16))
            self.k_ends.append(jnp.exp2(total - cp).astype(BF16))
            self.decays.append(jnp.exp2(total))

            ops = []
            for h in cs.halves:
                q_parts, k_parts = [], []
                for p0 in range(0, c, 2 * h):
                    lo, up = slice(p0, p0 + h), slice(p0 + h, p0 + 2 * h)
                    c_b = cum[p0 + h - 1:p0 + h, :]
                    q_parts += [jnp.zeros((h, d), F32), q[up] * jnp.exp2(cum[up] - c_b)]
                    k_parts += [jnp.exp2(c_b - cp[lo]), jnp.zeros((h, d), F32)]
                ops.append((jnp.concatenate(q_parts, axis=0).astype(BF16),
                            jnp.concatenate(k_parts, axis=0).astype(BF16)))
            self.level_ops.append(ops)

            pairs = []
            for i0 in range(0, c, pair):
                q_i = q[i0:i0 + pair]
                c_i = cum[i0:i0 + pair]
                a_pair = jnp.zeros((pair, d), F32)
                for j in range(sub):
                    srcs = [cp_ref[r0 + i0 + s + j:r0 + i0 + s + j + 1, :]
                            for s in range(0, pair, sub)]
                    if len(srcs) == 1:
                        cp_j = srcs[0]
                    else:
                        cp_j = jnp.concatenate([jnp.broadcast_to(s, (sub, d)) for s in srcs], axis=0)
                    s_j = jnp.sum(q_i * jnp.exp2(c_i - cp_j), axis=-1, keepdims=True)
                    a_pair = jnp.where(cs.pair_lane == j, s_j, a_pair)
                pairs.append(a_pair)
            self.a_diags.append(jnp.concatenate(pairs, axis=0)[:, :c])

    def free_dots(self):
        self.lows = [[_dot_nt(q_l, k_l) for q_l, k_l in ops] for ops in self.level_ops]
        self.updates = [_dot_tn(self.v[rows], k_end) for rows, k_end in zip(self.chunks, self.k_ends)]

    def outputs(self, state):
        cs = self.cs
        states = []
        for decay, upd in zip(self.decays, self.updates):
            states.append(state.astype(BF16))
            state = decay * state + upd
        outs = []
        for n, rows in enumerate(self.chunks):
            attn = jnp.where(cs.diag_mask, self.a_diags[n], 0.0)
            for h, mask, low in zip(cs.halves, cs.level_masks, self.lows[n]):
                attn = attn + (low if 2 * h == cs.c else jnp.where(mask, low, 0.0))
            outs.append(_dot(attn.astype(BF16), self.v[rows]) + _dot_nt(self.q_ins[n], states[n]))
        return outs, state


_RQ, _RK, _RV, _RG, _HQ, _HK, _HV, _HG = range(8)


def _mixer_kernel(x_ref, g_ref, cos_sin_ref, lbl_ref, rgain_ref, hgain_ref,
                  w_ref, *refs, n_col_blocks, blocks_per_seq, n_side):
    side_in = refs[:n_side]
    ret_ref, hgo_ref = refs[n_side:n_side + 2]
    side_out = refs[n_side + 2:2 * n_side + 2]
    h_ref, act_ref, lf_ref, ret_state_ref, hgrn_state_ref, cp_ref = refs[2 * n_side + 2:]
    for i_ref, o_ref in zip(side_in, side_out):
        o_ref[...] = i_ref[...].astype(BF16)

    s = pl.program_id(0)
    nj = n_col_blocks
    dk, dh = RET_HEAD_DIM, HGRN_HEAD_DIM
    heads_per_step = dk // dh
    cons = jnp.maximum(s - 1, 0)
    cj = cons % nj

    @pl.when(s == 0)
    def _():
        act_ref[...] = jnp.zeros_like(act_ref)
        lf_ref[...] = jnp.zeros_like(lf_ref)

    @pl.when(s % nj == 0)
    def _():
        h_ref[...] = _rmsnorm(x_ref[...], g_ref[...]).astype(BF16)

    @pl.when((cons // nj) % blocks_per_seq == 0)
    def _():
        ret_state_ref[cj] = jnp.zeros((dk, dk), F32)
        for a in range(heads_per_step):
            hgrn_state_ref[heads_per_step * cj + a] = jnp.zeros((dh, dh), F32)

    h = h_ref[...]
    half = dk // 2
    cos = cos_sin_ref[:, :half]
    sin = cos_sin_ref[:, half:]
    dst = act_ref.at[s % 2]
    src = act_ref.at[(s + 1) % 2]
    src_lf = lf_ref.at[(s + 1) % 2]

    def rope(p):
        x1, x2 = p[:, :half], p[:, half:]
        return jnp.concatenate([x1 * cos - x2 * sin, x2 * cos + x1 * sin], axis=-1)

    consts = _HgrnConsts(HGRN_CHUNK, HGRN_SUB)
    hq, hk, hv, hlf = src[_HQ], src[_HK], src[_HV], src_lf[...]
    heads = []
    for a in range(heads_per_step):
        cols = slice(a * dh, (a + 1) * dh)
        heads.append(_HgrnHead(hq[:, cols], hk[:, cols], hv[:, cols], hlf[:, cols],
                               cp_ref.at[a], consts))
    ret = _RetentionHead(src[_RQ], src[_RK], src[_RV], cj, RET_CHUNK)

    def project_pair(first_group):
        return _dot(h, w_ref[first_group]), _dot(h, w_ref[first_group + 1])

    def project_hq_hf():
        p_hq, z = project_pair(_HQ)
        dst[_HQ] = _silu(p_hq).astype(BF16)
        lbl = lbl_ref[...]
        m = jnp.maximum(lbl, 0.0)
        e_l = jnp.exp(lbl - m)
        lb = e_l / (e_l + jnp.exp(-m))
        dst[_HK] = ((1.0 - lb) * jax.nn.sigmoid(-z)).astype(BF16)
        lf_ref[s % 2] = jnp.log2(lb + (1.0 - lb) * jax.nn.sigmoid(z))

    def project_hv_hg():
        p_hv, p_hg = project_pair(_HV)
        dst[_HV] = p_hv.astype(BF16)
        dst[_HG] = _silu(p_hg).astype(BF16)

    def project_rq_rk():
        p_rq, p_rk = project_pair(_RQ)
        dst[_RQ] = (rope(p_rq) * (dk ** -0.5)).astype(BF16)
        dst[_RK] = rope(p_rk).astype(BF16)

    def project_rv_rg():
        p_rv, p_rg = project_pair(_RV)
        dst[_RV] = p_rv.astype(BF16)
        dst[_RG] = _silu(p_rg).astype(BF16)

    def hgrn_outputs():
        hgate = src[_HG]
        for a, hd in enumerate(heads):
            cols = slice(a * dh, (a + 1) * dh)
            outs, state = hd.outputs(hgrn_state_ref[heads_per_step * cj + a])
            hgrn_state_ref[heads_per_step * cj + a] = state
            gain = hgain_ref[:, cols]
            for rows, o in zip(hd.chunks, outs):
                y = o * lax.rsqrt(jnp.mean(o * o, axis=-1, keepdims=True) + EPS)
                hgo_ref[rows, cols] = (y * gain * hgate[rows, cols].astype(F32)).astype(BF16)

    def retention_outputs():
        rgate = src[_RG]
        outs, state = ret.outputs(ret_state_ref[cj])
        ret_state_ref[cj] = state
        gain = rgain_ref[...]
        for rows, o in zip(ret.chunks, outs):
            mu = jnp.mean(o, axis=-1, keepdims=True)
            oc = o - mu
            var = jnp.mean(oc * oc, axis=-1, keepdims=True)
            ret_ref[rows, :] = (oc * lax.rsqrt(var + EPS) * gain
                                * rgate[rows, :].astype(F32)).astype(BF16)

    for hd in heads:
        hd.cumsum_dots()
    project_hq_hf()
    for hd in heads:
        hd.elementwise()
    for hd in heads:
        hd.free_dots()
    ret.free_dots()
    project_hv_hg()
    hgrn_outputs()
    project_rq_rk()
    retention_outputs()
    project_rv_rg()


def _mixer(x1, norm_g, w_units, lb_logits, ret_gain, hgrn_gain, cos_sin, seq, side_f32):
    t, d = x1.shape
    tm = MIXER_TM
    tn = RET_HEAD_DIM
    dh = HGRN_HEAD_DIM
    width = RET_HEADS * RET_HEAD_DIM
    assert w_units.shape == (width // tn, 8, d, tn) and width == HGRN_HEADS * dh and tn % dh == 0
    assert t % tm == 0 and seq % tm == 0
    assert tm % RET_CHUNK == 0 and tm % HGRN_CHUNK == 0
    assert HGRN_CHUNK % BF16_TILE_ROWS == 0 and BF16_TILE_ROWS % HGRN_SUB == 0
    nj = width // tn
    blocks_per_seq = seq // tm
    n_units = (t // tm) * nj
    prod = lambda s: jnp.minimum(s, n_units - 1)
    cons = lambda s: jnp.maximum(s - 1, 0)
    const = lambda s: (0, 0)
    pos = lambda s: ((prod(s) // nj) % blocks_per_seq, 0)
    cons_col = lambda s: (0, cons(s) % nj)

    side_specs = []
    for w in side_f32:
        rows = next(r for r in range(BF16_TILE_ROWS, w.shape[0] + 1, BF16_TILE_ROWS)
                    if w.shape[0] % r == 0 and w.shape[0] // r <= n_units)
        side_specs.append(pl.BlockSpec(
            (rows, w.shape[1]), lambda s, last=w.shape[0] // rows - 1: (jnp.minimum(s, last), 0)))

    out_spec = pl.BlockSpec((tm, tn), lambda s: (cons(s) // nj, cons(s) % nj))
    out_bf = jax.ShapeDtypeStruct((t, width), BF16)
    outs = pl.pallas_call(
        functools.partial(_mixer_kernel, n_col_blocks=nj, blocks_per_seq=blocks_per_seq,
                          n_side=len(side_f32)),
        grid=(n_units + 1,),
        in_specs=[pl.BlockSpec((tm, d), lambda s: (prod(s) // nj, 0)), pl.BlockSpec((1, d), const),
                  pl.BlockSpec((tm, tn), pos),
                  pl.BlockSpec((1, tn), lambda s: (0, prod(s) % nj)),
                  pl.BlockSpec((1, tn), cons_col), pl.BlockSpec((1, tn), cons_col),
                  pl.BlockSpec((None, 8, d, tn), lambda s: (prod(s) % nj, 0, 0, 0))] + side_specs,
        out_specs=[out_spec, out_spec] + side_specs,
        out_shape=[out_bf, out_bf] + [jax.ShapeDtypeStruct(w.shape, BF16) for w in side_f32],
        scratch_shapes=[pltpu.VMEM((tm, d), BF16),
                        pltpu.VMEM((2, 8, tm, tn), BF16),
                        pltpu.VMEM((2, tm, tn), F32),
                        pltpu.VMEM((RET_HEADS, RET_HEAD_DIM, RET_HEAD_DIM), F32),
                        pltpu.VMEM((HGRN_HEADS, dh, dh), F32),
                        pltpu.VMEM((tn // dh, tm, dh), F32)],
        compiler_params=pltpu.CompilerParams(
            dimension_semantics=("arbitrary",), vmem_limit_bytes=VMEM_LIMIT_BYTES),
        name="mixer",
    )(x1, norm_g.reshape(1, d), cos_sin, lb_logits, ret_gain.reshape(1, width),
      hgrn_gain.reshape(1, width), w_units, *side_f32)
    return outs[0], outs[1], outs[2:]


@jax.jit
def kernel(x, ffn1_norm, ffn1_w_gate, ffn1_w_up, ffn1_w_down, mix_norm, w_in, ret_norm_g, hgrn_lb_logits, hgrn_norm_g, w_out, ffn2_norm, ffn2_w_gate, ffn2_w_up, ffn2_w_down, final_norm):
    batch, seq, d = x.shape
    assert ffn1_norm.shape[0] == 1, "single-layer stack"
    t = batch * seq
    bf = lambda w: w[0].astype(BF16)

    inv = np.power(ROPE_BASE, -np.arange(0, RET_HEAD_DIM, 2, dtype=np.float64) / RET_HEAD_DIM)
    ang = np.arange(seq, dtype=np.float64)[:, None] * inv[None, :]
    cos_sin = jnp.asarray(np.concatenate([np.cos(ang), np.sin(ang)], axis=1), dtype=F32)

    x0 = x.reshape(t, d)
    x1, w_units = _ffn(x0, ffn1_norm[0], bf(ffn1_w_gate), bf(ffn1_w_up), bf(ffn1_w_down),
                       name="ffn1", w_in_f32=w_in[0].astype(F32))
    side = [ffn2_w_gate[0].astype(F32), ffn2_w_up[0].astype(F32), ffn2_w_down[0].astype(F32),
            w_out[0].astype(F32)]
    ret, hgo, (wg2, wu2, wd2, wo) = _mixer(
        x1, mix_norm[0], w_units, hgrn_lb_logits.astype(F32),
        ret_norm_g[0].astype(F32), hgrn_norm_g[0].astype(F32), cos_sin, seq, side)
    x2 = _outproj(x1, ret, hgo, wo)
    out = _ffn(x2, ffn2_norm[0], wg2, wu2, wd2, name="ffn2", final_g=final_norm)
    return out.reshape(batch, seq, d)
```

```python
import functools

import jax
import jax.numpy as jnp
import numpy as np
from jax import lax
from jax.experimental import pallas as pl
from jax.experimental.pallas import tpu as pltpu

F32 = jnp.float32
BF16 = jnp.bfloat16

RET_HEADS = 4
RET_HEAD_DIM = 256
HGRN_HEADS = 8
HGRN_HEAD_DIM = 128
ROPE_BASE = 10000.0
EPS = 1e-6
FFN_RESIDUAL_WEIGHT = 0.5

VMEM_LIMIT_BYTES = 56 * 1024 * 1024
BF16_TILE_ROWS = 16

FFN_TM = 1024
FFN_TF = 512
MIXER_TM = 512
OUTPROJ_TM = 512
RET_CHUNK = 256
HGRN_CHUNK = 64
HGRN_SUB = 8


def _dot(a, b):
    return jnp.dot(a, b, preferred_element_type=F32)


def _dot_nt(a, b):
    return lax.dot_general(a, b, (((1,), (1,)), ((), ())), preferred_element_type=F32)


def _dot_tn(a, b):
    return lax.dot_general(a, b, (((0,), (0,)), ((), ())), preferred_element_type=F32)


def _rmsnorm(x, g):
    return x * lax.rsqrt(jnp.mean(x * x, axis=-1, keepdims=True) + EPS) * g


def _silu(x):
    return x * jax.nn.sigmoid(x)


def _ffn_kernel(*refs, final_norm, cast_side, n_ff_steps):
    refs = list(refs)
    x_ref, g_ref, wg_ref, wu_ref, wd_ref = refs[:5]
    rest = refs[5:]
    fg_ref = rest.pop(0) if final_norm else None
    side_in_ref = rest.pop(0) if cast_side else None
    out_ref = rest.pop(0)
    side_out_ref = rest.pop(0) if cast_side else None
    (h_ref,) = rest

    j = pl.program_id(1)

    if cast_side:
        tn = side_out_ref.shape[2]
        for jj in range(side_out_ref.shape[0]):
            side_out_ref[jj] = side_in_ref[:, jj * tn:(jj + 1) * tn].astype(BF16)

    @pl.when(j == 0)
    def _():
        x = x_ref[...]
        out_ref[...] = x
        h_ref[...] = _rmsnorm(x, g_ref[...]).astype(BF16)

    h = h_ref[...]
    gate = _dot(h, wg_ref[...])
    up = _dot(h, wu_ref[...])
    act = (_silu(gate) * up * FFN_RESIDUAL_WEIGHT).astype(BF16)
    out_ref[...] += _dot(act, wd_ref[...])

    if final_norm:
        @pl.when(j == n_ff_steps - 1)
        def _():
            out_ref[...] = _rmsnorm(out_ref[...], fg_ref[...])


def _ffn(x, norm_g, w_gate, w_up, w_down, *, name, final_g=None, w_in_f32=None):
    t, d = x.shape
    d_ff = w_gate.shape[1]
    tm, tf = FFN_TM, FFN_TF
    assert t % tm == 0 and d_ff % tf == 0
    n_blocks = t // tm
    n_ff_steps = d_ff // tf
    row = lambda i, j: (i, 0)
    const = lambda i, j: (0, 0)
    in_specs = [pl.BlockSpec((tm, d), row),
                pl.BlockSpec((1, d), const),
                pl.BlockSpec((d, tf), lambda i, j: (0, j)),
                pl.BlockSpec((d, tf), lambda i, j: (0, j)),
                pl.BlockSpec((tf, d), lambda i, j: (j, 0))]
    args = [x, norm_g.reshape(1, d), w_gate, w_up, w_down]
    out_specs = [pl.BlockSpec((tm, d), row)]
    out_shape = [jax.ShapeDtypeStruct((t, d), F32)]
    if final_g is not None:
        in_specs.append(pl.BlockSpec((1, d), const))
        args.append(final_g.reshape(1, d))
    if w_in_f32 is not None:
        n_groups = 8
        tn = RET_HEAD_DIM
        width = w_in_f32.shape[1] // n_groups
        nj = width // tn
        rows = d // n_blocks
        assert d % n_blocks == 0 and rows % BF16_TILE_ROWS == 0 and n_ff_steps >= n_groups
        group = lambda i, j: (i, jnp.minimum(j, n_groups - 1))
        in_specs.append(pl.BlockSpec((rows, width), group))
        args.append(w_in_f32)
        out_specs.append(pl.BlockSpec((nj, None, rows, tn),
                                      lambda i, j: (0, jnp.minimum(j, n_groups - 1), i, 0)))
        out_shape.append(jax.ShapeDtypeStruct((nj, n_groups, d, tn), BF16))
    kern = functools.partial(_ffn_kernel, final_norm=final_g is not None,
                             cast_side=w_in_f32 is not None, n_ff_steps=n_ff_steps)
    outs = pl.pallas_call(
        kern,
        grid=(n_blocks, n_ff_steps),
        in_specs=in_specs,
        out_specs=out_specs,
        out_shape=out_shape,
        scratch_shapes=[pltpu.VMEM((tm, d), BF16)],
        compiler_params=pltpu.CompilerParams(
            dimension_semantics=("arbitrary", "arbitrary"), vmem_limit_bytes=VMEM_LIMIT_BYTES),
        name=name,
    )(*args)
    return outs if w_in_f32 is not None else outs[0]


def _outproj_kernel(x_ref, ret_ref, hgo_ref, wo_r_ref, wo_h_ref, out_ref):
    out_ref[...] = (x_ref[...] + _dot(ret_ref[...], wo_r_ref[...])
                    + _dot(hgo_ref[...], wo_h_ref[...]))


def _outproj(x1, ret, hgo, w_out):
    t, d = x1.shape
    tm = OUTPROJ_TM
    half = ret.shape[1]
    assert t % tm == 0 and hgo.shape[1] == half and w_out.shape == (2 * half, d)
    row = lambda i: (i, 0)
    return pl.pallas_call(
        _outproj_kernel,
        grid=(t // tm,),
        in_specs=[pl.BlockSpec((tm, d), row),
                  pl.BlockSpec((tm, half), row), pl.BlockSpec((tm, half), row),
                  pl.BlockSpec((half, d), lambda i: (0, 0)), pl.BlockSpec((half, d), lambda i: (1, 0))],
        out_specs=pl.BlockSpec((tm, d), row),
        out_shape=jax.ShapeDtypeStruct((t, d), F32),
        compiler_params=pltpu.CompilerParams(
            dimension_semantics=("parallel",), vmem_limit_bytes=VMEM_LIMIT_BYTES),
        name="outproj",
    )(x1, ret, hgo, w_out, w_out)


class _RetentionHead:
    def __init__(self, q, k, v, head, c):
        tb, dk = q.shape
        self.c = c
        self.chunks = [slice(n * c, (n + 1) * c) for n in range(tb // c)]
        self.q, self.k, self.v = q, k, v

        def log_gamma(shape):
            return jnp.log(1.0 - jnp.exp2(-5.0 - jnp.full(shape, head, jnp.int32).astype(F32)))

        row = lax.broadcasted_iota(jnp.int32, (c, dk), 0).astype(F32)
        rel = (lax.broadcasted_iota(jnp.int32, (c, c), 0) -
               lax.broadcasted_iota(jnp.int32, (c, c), 1)).astype(F32)
        self.decay = jnp.where(rel >= 0, jnp.exp(log_gamma((c, c)) * jnp.maximum(rel, 0.0)), 0.0)
        self.q_dec = jnp.exp(log_gamma((c, dk)) * (row + 1.0))
        self.k_dec = jnp.exp(log_gamma((c, dk)) * (c - 1.0 - row))
        self.g_chunk = jnp.exp(log_gamma((1, dk)) * float(c))

    def free_dots(self):
        self.scores = [_dot_nt(self.q[r], self.k[r]) for r in self.chunks]
        self.updates = [_dot_tn((self.k[r].astype(F32) * self.k_dec).astype(BF16), self.v[r])
                        for r in self.chunks]

    def outputs(self, state):
        states = []
        for upd in self.updates:
            states.append(state.astype(BF16))
            state = self.g_chunk * state + upd
        outs = []
        for r, s, st in zip(self.chunks, self.scores, states):
            inner = _dot((s * self.decay).astype(BF16), self.v[r])
            outs.append(inner + _dot(self.q[r], st) * self.q_dec)
        return outs, state


class _HgrnConsts:
    def __init__(self, c, sub):
        d = HGRN_HEAD_DIM
        pair = BF16_TILE_ROWS
        self.c, self.sub, self.pair = c, sub, pair
        self.tri = (lax.broadcasted_iota(jnp.int32, (c, c), 0) >=
                    lax.broadcasted_iota(jnp.int32, (c, c), 1)).astype(BF16)
        self.pair_lane = lax.broadcasted_iota(jnp.int32, (pair, d), 1) % sub
        a_row = lax.broadcasted_iota(jnp.int32, (c, c), 0)
        a_col = lax.broadcasted_iota(jnp.int32, (c, c), 1)
        self.diag_mask = ((a_row // sub) == (a_col // sub)) & (a_col <= a_row)
        self.halves = []
        h = c // 2
        while h >= sub:
            self.halves.append(h)
            h //= 2
        self.level_masks = [(a_row // (2 * h)) == (a_col // (2 * h)) for h in self.halves]


class _HgrnHead:
    def __init__(self, q, k, v, lf, cp_ref, consts):
        self.q, self.k, self.v, self.lf, self.cp_ref, self.cs = q, k, v, lf, cp_ref, consts
        c = consts.c
        self.chunks = [slice(n * c, (n + 1) * c) for n in range(q.shape[0] // c)]

    def cumsum_dots(self):
        d = HGRN_HEAD_DIM
        self.cums = []
        for rows in self.chunks:
            lf = self.lf[rows]
            lf1 = lf.astype(BF16)
            r1 = lf - lf1.astype(F32)
            lf2 = r1.astype(BF16)
            lf3 = (r1 - lf2.astype(F32)).astype(BF16)
            cum3 = _dot(self.cs.tri, jnp.concatenate([lf1, lf2, lf3], axis=1))
            self.cums.append(cum3[:, :d] + cum3[:, d:2 * d] + cum3[:, 2 * d:])

    def elementwise(self):
        cs, cp_ref = self.cs, self.cp_ref
        c, sub, pair, d = cs.c, cs.sub, cs.pair, HGRN_HEAD_DIM
        self.level_ops, self.a_diags, self.q_ins, self.k_ends, self.decays = [], [], [], [], []
        for rows, cum in zip(self.chunks, self.cums):
            r0 = rows.start
            q = self.q[rows].astype(F32)
            k = self.k[rows].astype(F32)
            cp = cum - jnp.log2(k)
            cp_ref[rows, :] = cp
            total = cum[c - 1:c, :]
            self.q_ins.append((q * jnp.exp2(cum)).astype(BF16))
            self.k_ends.append(jnp.exp2(total - cp).astype(BF16))
            self.decays.append(jnp.exp2(total))

            ops = []
            for h in cs.halves:
                q_parts, k_parts = [], []
                for p0 in range(0, c, 2 * h):
                    lo, up = slice(p0, p0 + h), slice(p0 + h, p0 + 2 * h)
                    c_b = cum[p0 + h - 1:p0 + h, :]
                    q_parts += [jnp.zeros((h, d), F32), q[up] * jnp.exp2(cum[up] - c_b)]
                    k_parts += [jnp.exp2(c_b - cp[lo]), jnp.zeros((h, d), F32)]
                ops.append((jnp.concatenate(q_parts, axis=0).astype(BF16),
                            jnp.concatenate(k_parts, axis=0).astype(BF16)))
            self.level_ops.append(ops)

            pairs = []
            for i0 in range(0, c, pair):
                q_i = q[i0:i0 + pair]
                c_i = cum[i0:i0 + pair]
                a_pair = jnp.zeros((pair, d), F32)
                for j in range(sub):
                    srcs = [cp_ref[r0 + i0 + s + j:r0 + i0 + s + j + 1, :]
                            for s in range(0, pair, sub)]
                    if len(srcs) == 1:
                        cp_j = srcs[0]
                    else:
                        cp_j = jnp.concatenate([jnp.broadcast_to(s, (sub, d)) for s in srcs], axis=0)
                    s_j = jnp.sum(q_i * jnp.exp2(c_i - cp_j), axis=-1, keepdims=True)
                    a_pair = jnp.where(cs.pair_lane == j, s_j, a_pair)
                pairs.append(a_pair)
            self.a_diags.append(jnp.concatenate(pairs, axis=0)[:, :c])

    def free_dots(self):
        self.lows = [[_dot_nt(q_l, k_l) for q_l, k_l in ops] for ops in self.level_ops]
        self.updates = [_dot_tn(self.v[rows], k_end) for rows, k_end in zip(self.chunks, self.k_ends)]

    def outputs(self, state):
        cs = self.cs
        states = []
        for decay, upd in zip(self.decays, self.updates):
            states.append(state.astype(BF16))
            state = decay * state + upd
        outs = []
        for n, rows in enumerate(self.chunks):
            attn = jnp.where(cs.diag_mask, self.a_diags[n], 0.0)
            for h, mask, low in zip(cs.halves, cs.level_masks, self.lows[n]):
                attn = attn + (low if 2 * h == cs.c else jnp.where(mask, low, 0.0))
            outs.append(_dot(attn.astype(BF16), self.v[rows]) + _dot_nt(self.q_ins[n], states[n]))
        return outs, state


_RQ, _RK, _RV, _RG, _HQ, _HK, _HV, _HG = range(8)


def _mixer_kernel(x_ref, g_ref, cos_sin_ref, lbl_ref, rgain_ref, hgain_ref,
                  w_ref, *refs, n_token_blocks, n_units, blocks_per_seq, n_side):
    side_in = refs[:n_side]
    ret_ref, hgo_ref = refs[n_side:n_side + 2]
    side_out = refs[n_side + 2:2 * n_side + 2]
    h_ref, act_ref, lf_ref, ret_state_ref, hgrn_state_ref, cp_ref = refs[2 * n_side + 2:]
    for i_ref, o_ref in zip(side_in, side_out):
        o_ref[...] = i_ref[...].astype(BF16)

    s = pl.program_id(0)
    nb = n_token_blocks
    dk, dh = RET_HEAD_DIM, HGRN_HEAD_DIM
    heads_per_step = dk // dh
    cons = jnp.clip(s - 2, 0, n_units - 1)
    cj = cons // nb

    @pl.when(s == 0)
    def _():
        h_ref[...] = jnp.zeros_like(h_ref)
        act_ref[...] = jnp.zeros_like(act_ref)
        lf_ref[...] = jnp.zeros_like(lf_ref)

    @pl.when((cons % nb) % blocks_per_seq == 0)
    def _():
        ret_state_ref[cj] = jnp.zeros((dk, dk), F32)
        for a in range(heads_per_step):
            hgrn_state_ref[heads_per_step * cj + a] = jnp.zeros((dh, dh), F32)

    h = h_ref[(s + 1) % 2]
    h_ref[s % 2] = _rmsnorm(x_ref[...], g_ref[...]).astype(BF16)
    half = dk // 2
    cos = cos_sin_ref[:, :half]
    sin = cos_sin_ref[:, half:]
    dst = act_ref.at[(s + 1) % 2]
    src = act_ref.at[s % 2]
    src_lf = lf_ref.at[s % 2]

    def rope(p):
        x1, x2 = p[:, :half], p[:, half:]
        return jnp.concatenate([x1 * cos - x2 * sin, x2 * cos + x1 * sin], axis=-1)

    consts = _HgrnConsts(HGRN_CHUNK, HGRN_SUB)
    hq, hk, hv, hlf = src[_HQ], src[_HK], src[_HV], src_lf[...]
    heads = []
    for a in range(heads_per_step):
        cols = slice(a * dh, (a + 1) * dh)
        heads.append(_HgrnHead(hq[:, cols], hk[:, cols], hv[:, cols], hlf[:, cols],
                               cp_ref.at[a], consts))
    ret = _RetentionHead(src[_RQ], src[_RK], src[_RV], cj, RET_CHUNK)

    def project_pair(first_group):
        return _dot(h, w_ref[first_group]), _dot(h, w_ref[first_group + 1])

    def project_hq_hf():
        p_hq, z = project_pair(_HQ)
        dst[_HQ] = _silu(p_hq).astype(BF16)
        lbl = lbl_ref[...]
        m = jnp.maximum(lbl, 0.0)
        e_l = jnp.exp(lbl - m)
        lb = e_l / (e_l + jnp.exp(-m))
        dst[_HK] = ((1.0 - lb) * jax.nn.sigmoid(-z)).astype(BF16)
        lf_ref[(s + 1) % 2] = jnp.log2(lb + (1.0 - lb) * jax.nn.sigmoid(z))

    def project_hv_hg():
        p_hv, p_hg = project_pair(_HV)
        dst[_HV] = p_hv.astype(BF16)
        dst[_HG] = _silu(p_hg).astype(BF16)

    def project_rq_rk():
        p_rq, p_rk = project_pair(_RQ)
        dst[_RQ] = (rope(p_rq) * (dk ** -0.5)).astype(BF16)
        dst[_RK] = rope(p_rk).astype(BF16)

    def project_rv_rg():
        p_rv, p_rg = project_pair(_RV)
        dst[_RV] = p_rv.astype(BF16)
        dst[_RG] = _silu(p_rg).astype(BF16)

    def hgrn_outputs():
        hgate = src[_HG]
        for a, hd in enumerate(heads):
            cols = slice(a * dh, (a + 1) * dh)
            outs, state = hd.outputs(hgrn_state_ref[heads_per_step * cj + a])
            hgrn_state_ref[heads_per_step * cj + a] = state
            gain = hgain_ref[:, cols]
            for rows, o in zip(hd.chunks, outs):
                y = o * lax.rsqrt(jnp.mean(o * o, axis=-1, keepdims=True) + EPS)
                hgo_ref[rows, cols] = (y * gain * hgate[rows, cols].astype(F32)).astype(BF16)

    def retention_outputs():
        rgate = src[_RG]
        outs, state = ret.outputs(ret_state_ref[cj])
        ret_state_ref[cj] = state
        gain = rgain_ref[...]
        for rows, o in zip(ret.chunks, outs):
            mu = jnp.mean(o, axis=-1, keepdims=True)
            oc = o - mu
            var = jnp.mean(oc * oc, axis=-1, keepdims=True)
            ret_ref[rows, :] = (oc * lax.rsqrt(var + EPS) * gain
                                * rgate[rows, :].astype(F32)).astype(BF16)

    for hd in heads:
        hd.cumsum_dots()
    project_hq_hf()
    for hd in heads:
        hd.elementwise()
    for hd in heads:
        hd.free_dots()
    ret.free_dots()
    project_hv_hg()
    hgrn_outputs()
    project_rq_rk()
    retention_outputs()
    project_rv_rg()


def _mixer(x1, norm_g, w_units, lb_logits, ret_gain, hgrn_gain, cos_sin, seq, side_f32):
    t, d = x1.shape
    tm = MIXER_TM
    tn = RET_HEAD_DIM
    dh = HGRN_HEAD_DIM
    width = RET_HEADS * RET_HEAD_DIM
    assert w_units.shape == (width // tn, 8, d, tn) and width == HGRN_HEADS * dh and tn % dh == 0
    assert t % tm == 0 and seq % tm == 0
    assert tm % RET_CHUNK == 0 and tm % HGRN_CHUNK == 0
    assert HGRN_CHUNK % BF16_TILE_ROWS == 0 and BF16_TILE_ROWS % HGRN_SUB == 0
    nj = width // tn
    nb = t // tm
    blocks_per_seq = seq // tm
    n_units = nb * nj
    unit = lambda u: jnp.clip(u, 0, n_units - 1)
    norm = lambda s: unit(s)
    prod = lambda s: unit(s - 1)
    cons = lambda s: unit(s - 2)
    const = lambda s: (0, 0)
    pos = lambda s: ((prod(s) % nb) % blocks_per_seq, 0)
    cons_col = lambda s: (0, cons(s) // nb)

    side_specs = []
    for w in side_f32:
        rows = next(r for r in range(BF16_TILE_ROWS, w.shape[0] + 1, BF16_TILE_ROWS)
                    if w.shape[0] % r == 0 and w.shape[0] // r <= n_units)
        side_specs.append(pl.BlockSpec(
            (rows, w.shape[1]), lambda s, last=w.shape[0] // rows - 1: (jnp.minimum(s, last), 0)))

    out_spec = pl.BlockSpec((tm, tn), lambda s: (cons(s) % nb, cons(s) // nb))
    out_bf = jax.ShapeDtypeStruct((t, width), BF16)
    outs = pl.pallas_call(
        functools.partial(_mixer_kernel, n_token_blocks=nb, n_units=n_units,
                          blocks_per_seq=blocks_per_seq, n_side=len(side_f32)),
        grid=(n_units + 2,),
        in_specs=[pl.BlockSpec((tm, d), lambda s: (norm(s) % nb, 0)), pl.BlockSpec((1, d), const),
                  pl.BlockSpec((tm, tn), pos),
                  pl.BlockSpec((1, tn), lambda s: (0, prod(s) // nb)),
                  pl.BlockSpec((1, tn), cons_col), pl.BlockSpec((1, tn), cons_col),
                  pl.BlockSpec((None, 8, d, tn), lambda s: (prod(s) // nb, 0, 0, 0))] + side_specs,
        out_specs=[out_spec, out_spec] + side_specs,
        out_shape=[out_bf, out_bf] + [jax.ShapeDtypeStruct(w.shape, BF16) for w in side_f32],
        scratch_shapes=[pltpu.VMEM((2, tm, d), BF16),
                        pltpu.VMEM((2, 8, tm, tn), BF16),
                        pltpu.VMEM((2, tm, tn), F32),
                        pltpu.VMEM((RET_HEADS, RET_HEAD_DIM, RET_HEAD_DIM), F32),
                        pltpu.VMEM((HGRN_HEADS, dh, dh), F32),
                        pltpu.VMEM((tn // dh, tm, dh), F32)],
        compiler_params=pltpu.CompilerParams(
            dimension_semantics=("arbitrary",), vmem_limit_bytes=VMEM_LIMIT_BYTES),
        name="mixer",
    )(x1, norm_g.reshape(1, d), cos_sin, lb_logits, ret_gain.reshape(1, width),
      hgrn_gain.reshape(1, width), w_units, *side_f32)
    return outs[0], outs[1], outs[2:]


@jax.jit
def kernel(x, ffn1_norm, ffn1_w_gate, ffn1_w_up, ffn1_w_down, mix_norm, w_in, ret_norm_g, hgrn_lb_logits, hgrn_norm_g, w_out, ffn2_norm, ffn2_w_gate, ffn2_w_up, ffn2_w_down, final_norm):
    batch, seq, d = x.shape
    assert ffn1_norm.shape[0] == 1, "single-layer stack"
    t = batch * seq
    bf = lambda w: w[0].astype(BF16)

    inv = np.power(ROPE_BASE, -np.arange(0, RET_HEAD_DIM, 2, dtype=np.float64) / RET_HEAD_DIM)
    ang = np.arange(seq, dtype=np.float64)[:, None] * inv[None, :]
    cos_sin = jnp.asarray(np.concatenate([np.cos(ang), np.sin(ang)], axis=1), dtype=F32)

    x0 = x.reshape(t, d)
    x1, w_units = _ffn(x0, ffn1_norm[0], bf(ffn1_w_gate), bf(ffn1_w_up), bf(ffn1_w_down),
                       name="ffn1", w_in_f32=w_in[0].astype(F32))
    side = [ffn2_w_gate[0].astype(F32), ffn2_w_up[0].astype(F32), ffn2_w_down[0].astype(F32),
            w_out[0].astype(F32)]
    ret, hgo, (wg2, wu2, wd2, wo) = _mixer(
        x1, mix_norm[0], w_units, hgrn_lb_logits.astype(F32),
        ret_norm_g[0].astype(F32), hgrn_norm_g[0].astype(F32), cos_sin, seq, side)
    x2 = _outproj(x1, ret, hgo, wo)
    out = _ffn(x2, ffn2_norm[0], wg2, wu2, wd2, name="ffn2", final_g=final_norm)
    return out.reshape(batch, seq, d)
```

```python
import functools

import jax
import jax.numpy as jnp
import numpy as np
from jax import lax
from jax.experimental import pallas as pl
from jax.experimental.pallas import tpu as pltpu

F32 = jnp.float32
BF16 = jnp.bfloat16

RET_HEADS = 4
RET_HEAD_DIM = 256
HGRN_HEADS = 8
HGRN_HEAD_DIM = 128
ROPE_BASE = 10000.0
EPS = 1e-6
FFN_RESIDUAL_WEIGHT = 0.5

VMEM_LIMIT_BYTES = 56 * 1024 * 1024
BF16_TILE_ROWS = 16

FFN_TM = 1024
FFN_TF = 512
MIXER_TM = 512
OUTPROJ_TM = 512
RET_CHUNK = 256
HGRN_CHUNK = 64
HGRN_SUB = 8


def _dot(a, b):
    return jnp.dot(a, b, preferred_element_type=F32)


def _dot_nt(a, b):
    return lax.dot_general(a, b, (((1,), (1,)), ((), ())), preferred_element_type=F32)


def _dot_tn(a, b):
    return lax.dot_general(a, b, (((0,), (0,)), ((), ())), preferred_element_type=F32)


def _rmsnorm(x, g):
    return x * lax.rsqrt(jnp.mean(x * x, axis=-1, keepdims=True) + EPS) * g


def _silu(x):
    return x * jax.nn.sigmoid(x)


def _ffn_kernel(*refs, final_norm, cast_side, n_ff_steps):
    refs = list(refs)
    x_ref, g_ref, wg_ref, wu_ref, wd_ref = refs[:5]
    rest = refs[5:]
    fg_ref = rest.pop(0) if final_norm else None
    side_in_ref = rest.pop(0) if cast_side else None
    out_ref = rest.pop(0)
    side_out_ref = rest.pop(0) if cast_side else None
    (h_ref,) = rest

    j = pl.program_id(1)

    @pl.when(j == 0)
    def _():
        x = x_ref[...]
        out_ref[...] = x
        h_ref[...] = _rmsnorm(x, g_ref[...]).astype(BF16)

    if cast_side:
        tn = side_out_ref.shape[2]
        for jj in range(side_out_ref.shape[0]):
            side_out_ref[jj] = side_in_ref[:, jj * tn:(jj + 1) * tn].astype(BF16)

    h = h_ref[...]
    gate = _dot(h, wg_ref[...])
    up = _dot(h, wu_ref[...])
    act = (_silu(gate) * up * FFN_RESIDUAL_WEIGHT).astype(BF16)
    out_ref[...] += _dot(act, wd_ref[...])

    if final_norm:
        @pl.when(j == n_ff_steps - 1)
        def _():
            out_ref[...] = _rmsnorm(out_ref[...], fg_ref[...])


def _ffn(x, norm_g, w_gate, w_up, w_down, *, name, final_g=None, w_in_f32=None):
    t, d = x.shape
    d_ff = w_gate.shape[1]
    tm, tf = FFN_TM, FFN_TF
    assert t % tm == 0 and d_ff % tf == 0
    n_blocks = t // tm
    n_ff_steps = d_ff // tf
    row = lambda i, j: (i, 0)
    const = lambda i, j: (0, 0)
    in_specs = [pl.BlockSpec((tm, d), row),
                pl.BlockSpec((1, d), const),
                pl.BlockSpec((d, tf), lambda i, j: (0, j)),
                pl.BlockSpec((d, tf), lambda i, j: (0, j)),
                pl.BlockSpec((tf, d), lambda i, j: (j, 0))]
    args = [x, norm_g.reshape(1, d), w_gate, w_up, w_down]
    out_specs = [pl.BlockSpec((tm, d), row)]
    out_shape = [jax.ShapeDtypeStruct((t, d), F32)]
    if final_g is not None:
        in_specs.append(pl.BlockSpec((1, d), const))
        args.append(final_g.reshape(1, d))
    if w_in_f32 is not None:
        n_groups = 8
        tn = RET_HEAD_DIM
        width = w_in_f32.shape[1] // n_groups
        nj = width // tn
        rows = d // n_blocks
        assert d % n_blocks == 0 and rows % BF16_TILE_ROWS == 0 and n_ff_steps >= n_groups
        group = lambda i, j: (i, jnp.minimum(j, n_groups - 1))
        in_specs.append(pl.BlockSpec((rows, width), group))
        args.append(w_in_f32)
        out_specs.append(pl.BlockSpec((nj, None, rows, tn),
                                      lambda i, j: (0, jnp.minimum(j, n_groups - 1), i, 0)))
        out_shape.append(jax.ShapeDtypeStruct((nj, n_groups, d, tn), BF16))
    kern = functools.partial(_ffn_kernel, final_norm=final_g is not None,
                             cast_side=w_in_f32 is not None, n_ff_steps=n_ff_steps)
    outs = pl.pallas_call(
        kern,
        grid=(n_blocks, n_ff_steps),
        in_specs=in_specs,
        out_specs=out_specs,
        out_shape=out_shape,
        scratch_shapes=[pltpu.VMEM((tm, d), BF16)],
        compiler_params=pltpu.CompilerParams(
            dimension_semantics=("parallel", "arbitrary"), vmem_limit_bytes=VMEM_LIMIT_BYTES),
        name=name,
    )(*args)
    return outs if w_in_f32 is not None else outs[0]


def _outproj_kernel(x_ref, ret_ref, hgo_ref, wo_r_ref, wo_h_ref, out_ref):
    out_ref[...] = (x_ref[...] + _dot(ret_ref[...], wo_r_ref[...])
                    + _dot(hgo_ref[...], wo_h_ref[...]))


def _outproj(x1, ret, hgo, w_out):
    t, d = x1.shape
    tm = OUTPROJ_TM
    half = ret.shape[1]
    assert t % tm == 0 and hgo.shape[1] == half and w_out.shape == (2 * half, d)
    row = lambda i: (i, 0)
    return pl.pallas_call(
        _outproj_kernel,
        grid=(t // tm,),
        in_specs=[pl.BlockSpec((tm, d), row),
                  pl.BlockSpec((tm, half), row), pl.BlockSpec((tm, half), row),
                  pl.BlockSpec((half, d), lambda i: (0, 0)), pl.BlockSpec((half, d), lambda i: (1, 0))],
        out_specs=pl.BlockSpec((tm, d), row),
        out_shape=jax.ShapeDtypeStruct((t, d), F32),
        compiler_params=pltpu.CompilerParams(
            dimension_semantics=("parallel",), vmem_limit_bytes=VMEM_LIMIT_BYTES),
        name="outproj",
    )(x1, ret, hgo, w_out, w_out)


class _RetentionHead:
    def __init__(self, q, k, v, head, c):
        tb, dk = q.shape
        self.c = c
        self.chunks = [slice(n * c, (n + 1) * c) for n in range(tb // c)]
        self.q, self.k, self.v = q, k, v

        def log_gamma(shape):
            return jnp.log(1.0 - jnp.exp2(-5.0 - jnp.full(shape, head, jnp.int32).astype(F32)))

        row = lax.broadcasted_iota(jnp.int32, (c, dk), 0).astype(F32)
        rel = (lax.broadcasted_iota(jnp.int32, (c, c), 0) -
               lax.broadcasted_iota(jnp.int32, (c, c), 1)).astype(F32)
        self.decay = jnp.where(rel >= 0, jnp.exp(log_gamma((c, c)) * jnp.maximum(rel, 0.0)), 0.0)
        self.q_dec = jnp.exp(log_gamma((c, dk)) * (row + 1.0))
        self.k_dec = jnp.exp(log_gamma((c, dk)) * (c - 1.0 - row))
        self.g_chunk = jnp.exp(log_gamma((1, dk)) * float(c))

    def free_dots(self):
        self.scores = [_dot_nt(self.q[r], self.k[r]) for r in self.chunks]
        self.updates = [_dot_tn((self.k[r].astype(F32) * self.k_dec).astype(BF16), self.v[r])
                        for r in self.chunks]

    def outputs(self, state):
        states = []
        for upd in self.updates:
            states.append(state.astype(BF16))
            state = self.g_chunk * state + upd
        outs = []
        for r, s, st in zip(self.chunks, self.scores, states):
            inner = _dot((s * self.decay).astype(BF16), self.v[r])
            outs.append(inner + _dot(self.q[r], st) * self.q_dec)
        return outs, state


class _HgrnConsts:
    def __init__(self, c, sub):
        d = HGRN_HEAD_DIM
        pair = BF16_TILE_ROWS
        self.c, self.sub, self.pair = c, sub, pair
        self.tri = (lax.broadcasted_iota(jnp.int32, (c, c), 0) >=
                    lax.broadcasted_iota(jnp.int32, (c, c), 1)).astype(BF16)
        self.pair_lane = lax.broadcasted_iota(jnp.int32, (pair, d), 1) % sub
        a_row = lax.broadcasted_iota(jnp.int32, (c, c), 0)
        a_col = lax.broadcasted_iota(jnp.int32, (c, c), 1)
        self.diag_mask = ((a_row // sub) == (a_col // sub)) & (a_col <= a_row)
        self.halves = []
        h = c // 2
        while h >= sub:
            self.halves.append(h)
            h //= 2
        self.level_masks = [(a_row // (2 * h)) == (a_col // (2 * h)) for h in self.halves]


class _HgrnHead:
    def __init__(self, q, k, v, lf, cp_ref, consts):
        self.q, self.k, self.v, self.lf, self.cp_ref, self.cs = q, k, v, lf, cp_ref, consts
        c = consts.c
        self.chunks = [slice(n * c, (n + 1) * c) for n in range(q.shape[0] // c)]

    def cumsum_dots(self):
        d = HGRN_HEAD_DIM
        self.cums = []
        for rows in self.chunks:
            lf = self.lf[rows]
            lf1 = lf.astype(BF16)
            r1 = lf - lf1.astype(F32)
            lf2 = r1.astype(BF16)
            lf3 = (r1 - lf2.astype(F32)).astype(BF16)
            cum3 = _dot(self.cs.tri, jnp.concatenate([lf1, lf2, lf3], axis=1))
            self.cums.append(cum3[:, :d] + cum3[:, d:2 * d] + cum3[:, 2 * d:])

    def elementwise(self):
        cs, cp_ref = self.cs, self.cp_ref
        c, sub, pair, d = cs.c, cs.sub, cs.pair, HGRN_HEAD_DIM
        self.level_ops, self.a_diags, self.q_ins, self.k_ends, self.decays = [], [], [], [], []
        for rows, cum in zip(self.chunks, self.cums):
            r0 = rows.start
            q = self.q[rows].astype(F32)
            k = self.k[rows].astype(F32)
            cp = cum - jnp.log2(k)
            cp_ref[rows, :] = cp
            total = cum[c - 1:c, :]
            self.q_ins.append((q * jnp.exp2(cum)).astype(BF16))
            self.k_ends.append(jnp.exp2(total - cp).astype(BF16))
            self.decays.append(jnp.exp2(total))

            ops = []
            for h in cs.halves:
                q_parts, k_parts = [], []
                for p0 in range(0, c, 2 * h):
                    lo, up = slice(p0, p0 + h), slice(p0 + h, p0 + 2 * h)
                    c_b = cum[p0 + h - 1:p0 + h, :]
                    q_parts += [jnp.zeros((h, d), F32), q[up] * jnp.exp2(cum[up] - c_b)]
                    k_parts += [jnp.exp2(c_b - cp[lo]), jnp.zeros((h, d), F32)]
                ops.append((jnp.concatenate(q_parts, axis=0).astype(BF16),
                            jnp.concatenate(k_parts, axis=0).astype(BF16)))
            self.level_ops.append(ops)

            pairs = []
            for i0 in range(0, c, pair):
                q_i = q[i0:i0 + pair]
                c_i = cum[i0:i0 + pair]
                a_pair = jnp.zeros((pair, d), F32)
                for j in range(sub):
                    srcs = [cp_ref[r0 + i0 + s + j:r0 + i0 + s + j + 1, :]
                            for s in range(0, pair, sub)]
                    if len(srcs) == 1:
                        cp_j = srcs[0]
                    else:
                        cp_j = jnp.concatenate([jnp.broadcast_to(s, (sub, d)) for s in srcs], axis=0)
                    s_j = jnp.sum(q_i * jnp.exp2(c_i - cp_j), axis=-1, keepdims=True)
                    a_pair = jnp.where(cs.pair_lane == j, s_j, a_pair)
                pairs.append(a_pair)
            self.a_diags.append(jnp.concatenate(pairs, axis=0)[:, :c])

    def free_dots(self):
        self.lows = [[_dot_nt(q_l, k_l) for q_l, k_l in ops] for ops in self.level_ops]
        self.updates = [_dot_tn(self.v[rows], k_end) for rows, k_end in zip(self.chunks, self.k_ends)]

    def outputs(self, state):
        cs = self.cs
        states = []
        for decay, upd in zip(self.decays, self.updates):
            states.append(state.astype(BF16))
            state = decay * state + upd
        outs = []
        for n, rows in enumerate(self.chunks):
            attn = jnp.where(cs.diag_mask, self.a_diags[n], 0.0)
            for h, mask, low in zip(cs.halves, cs.level_masks, self.lows[n]):
                attn = attn + (low if 2 * h == cs.c else jnp.where(mask, low, 0.0))
            outs.append(_dot(attn.astype(BF16), self.v[rows]) + _dot_nt(self.q_ins[n], states[n]))
        return outs, state


_RQ, _RK, _RV, _RG, _HQ, _HK, _HV, _HG = range(8)


def _mixer_kernel(x_ref, g_ref, cos_sin_ref, lbl_ref, rgain_ref, hgain_ref,
                  w_ref, *refs, n_col_blocks, blocks_per_seq, n_side):
    side_in = refs[:n_side]
    ret_ref, hgo_ref = refs[n_side:n_side + 2]
    side_out = refs[n_side + 2:2 * n_side + 2]
    h_ref, act_ref, lf_ref, ret_state_ref, hgrn_state_ref, cp_ref = refs[2 * n_side + 2:]

    s = pl.program_id(0)
    nj = n_col_blocks
    dk, dh = RET_HEAD_DIM, HGRN_HEAD_DIM
    heads_per_step = dk // dh
    cons = jnp.maximum(s - 1, 0)
    cj = cons % nj

    @pl.when(s == 0)
    def _():
        act_ref[...] = jnp.zeros_like(act_ref)
        lf_ref[...] = jnp.zeros_like(lf_ref)

    @pl.when(s % nj == 0)
    def _():
        h_ref[...] = _rmsnorm(x_ref[...], g_ref[...]).astype(BF16)

    @pl.when((cons // nj) % blocks_per_seq == 0)
    def _():
        ret_state_ref[cj] = jnp.zeros((dk, dk), F32)
        for a in range(heads_per_step):
            hgrn_state_ref[heads_per_step * cj + a] = jnp.zeros((dh, dh), F32)

    for i_ref, o_ref in zip(side_in, side_out):
        o_ref[...] = i_ref[...].astype(BF16)

    h = h_ref[...]
    half = dk // 2
    cos = cos_sin_ref[:, :half]
    sin = cos_sin_ref[:, half:]
    dst = act_ref.at[s % 2]
    src = act_ref.at[(s + 1) % 2]
    src_lf = lf_ref.at[(s + 1) % 2]

    def rope(p):
        x1, x2 = p[:, :half], p[:, half:]
        return jnp.concatenate([x1 * cos - x2 * sin, x2 * cos + x1 * sin], axis=-1)

    consts = _HgrnConsts(HGRN_CHUNK, HGRN_SUB)
    hq, hk, hv, hlf = src[_HQ], src[_HK], src[_HV], src_lf[...]
    heads = []
    for a in range(heads_per_step):
        cols = slice(a * dh, (a + 1) * dh)
        heads.append(_HgrnHead(hq[:, cols], hk[:, cols], hv[:, cols], hlf[:, cols],
                               cp_ref.at[a], consts))
    ret = _RetentionHead(src[_RQ], src[_RK], src[_RV], cj, RET_CHUNK)

    def project_pair(first_group):
        return _dot(h, w_ref[first_group]), _dot(h, w_ref[first_group + 1])

    def project_hq_hf():
        p_hq, z = project_pair(_HQ)
        dst[_HQ] = _silu(p_hq).astype(BF16)
        lbl = lbl_ref[...]
        m = jnp.maximum(lbl, 0.0)
        e_l = jnp.exp(lbl - m)
        lb = e_l / (e_l + jnp.exp(-m))
        dst[_HK] = ((1.0 - lb) * jax.nn.sigmoid(-z)).astype(BF16)
        lf_ref[s % 2] = jnp.log2(lb + (1.0 - lb) * jax.nn.sigmoid(z))

    def project_hv_hg():
        p_hv, p_hg = project_pair(_HV)
        dst[_HV] = p_hv.astype(BF16)
        dst[_HG] = _silu(p_hg).astype(BF16)

    def project_rq_rk():
        p_rq, p_rk = project_pair(_RQ)
        dst[_RQ] = (rope(p_rq) * (dk ** -0.5)).astype(BF16)
        dst[_RK] = rope(p_rk).astype(BF16)

    def project_rv_rg():
        p_rv, p_rg = project_pair(_RV)
        dst[_RV] = p_rv.astype(BF16)
        dst[_RG] = _silu(p_rg).astype(BF16)

    def hgrn_outputs():
        hgate = src[_HG]
        for a, hd in enumerate(heads):
            cols = slice(a * dh, (a + 1) * dh)
            outs, state = hd.outputs(hgrn_state_ref[heads_per_step * cj + a])
            hgrn_state_ref[heads_per_step * cj + a] = state
            gain = hgain_ref[:, cols]
            for rows, o in zip(hd.chunks, outs):
                y = o * lax.rsqrt(jnp.mean(o * o, axis=-1, keepdims=True) + EPS)
                hgo_ref[rows, cols] = (y * gain * hgate[rows, cols].astype(F32)).astype(BF16)

    def retention_outputs():
        rgate = src[_RG]
        outs, state = ret.outputs(ret_state_ref[cj])
        ret_state_ref[cj] = state
        gain = rgain_ref[...]
        for rows, o in zip(ret.chunks, outs):
            mu = jnp.mean(o, axis=-1, keepdims=True)
            oc = o - mu
            var = jnp.mean(oc * oc, axis=-1, keepdims=True)
            ret_ref[rows, :] = (oc * lax.rsqrt(var + EPS) * gain
                                * rgate[rows, :].astype(F32)).astype(BF16)

    for hd in heads:
        hd.cumsum_dots()
    project_hq_hf()
    for hd in heads:
        hd.elementwise()
    for hd in heads:
        hd.free_dots()
    ret.free_dots()
    project_hv_hg()
    hgrn_outputs()
    project_rq_rk()
    retention_outputs()
    project_rv_rg()


def _mixer(x1, norm_g, w_units, lb_logits, ret_gain, hgrn_gain, cos_sin, seq, side_f32):
    t, d = x1.shape
    tm = MIXER_TM
    tn = RET_HEAD_DIM
    dh = HGRN_HEAD_DIM
    width = RET_HEADS * RET_HEAD_DIM
    assert w_units.shape == (width // tn, 8, d, tn) and width == HGRN_HEADS * dh and tn % dh == 0
    assert t % tm == 0 and seq % tm == 0
    assert tm % RET_CHUNK == 0 and tm % HGRN_CHUNK == 0
    assert HGRN_CHUNK % BF16_TILE_ROWS == 0 and BF16_TILE_ROWS % HGRN_SUB == 0
    nj = width // tn
    blocks_per_seq = seq // tm
    n_units = (t // tm) * nj
    prod = lambda s: jnp.minimum(s, n_units - 1)
    cons = lambda s: jnp.maximum(s - 1, 0)
    const = lambda s: (0, 0)
    pos = lambda s: ((prod(s) // nj) % blocks_per_seq, 0)
    cons_col = lambda s: (0, cons(s) % nj)

    side_specs = []
    for w in side_f32:
        rows = next(r for r in range(BF16_TILE_ROWS, w.shape[0] + 1, BF16_TILE_ROWS)
                    if w.shape[0] % r == 0 and w.shape[0] // r <= n_units)
        side_specs.append(pl.BlockSpec(
            (rows, w.shape[1]), lambda s, last=w.shape[0] // rows - 1: (jnp.minimum(s, last), 0)))

    out_spec = pl.BlockSpec((tm, tn), lambda s: (cons(s) // nj, cons(s) % nj))
    out_bf = jax.ShapeDtypeStruct((t, width), BF16)
    outs = pl.pallas_call(
        functools.partial(_mixer_kernel, n_col_blocks=nj, blocks_per_seq=blocks_per_seq,
                          n_side=len(side_f32)),
        grid=(n_units + 1,),
        in_specs=[pl.BlockSpec((tm, d), lambda s: (prod(s) // nj, 0)), pl.BlockSpec((1, d), const),
                  pl.BlockSpec((tm, tn), pos),
                  pl.BlockSpec((1, tn), lambda s: (0, prod(s) % nj)),
                  pl.BlockSpec((1, tn), cons_col), pl.BlockSpec((1, tn), cons_col),
                  pl.BlockSpec((None, 8, d, tn), lambda s: (prod(s) % nj, 0, 0, 0))] + side_specs,
        out_specs=[out_spec, out_spec] + side_specs,
        out_shape=[out_bf, out_bf] + [jax.ShapeDtypeStruct(w.shape, BF16) for w in side_f32],
        scratch_shapes=[pltpu.VMEM((tm, d), BF16),
                        pltpu.VMEM((2, 8, tm, tn), BF16),
                        pltpu.VMEM((2, tm, tn), F32),
                        pltpu.VMEM((RET_HEADS, RET_HEAD_DIM, RET_HEAD_DIM), F32),
                        pltpu.VMEM((HGRN_HEADS, dh, dh), F32),
                        pltpu.VMEM((tn // dh, tm, dh), F32)],
        compiler_params=pltpu.CompilerParams(
            dimension_semantics=("arbitrary",), vmem_limit_bytes=VMEM_LIMIT_BYTES),
        name="mixer",
    )(x1, norm_g.reshape(1, d), cos_sin, lb_logits, ret_gain.reshape(1, width),
      hgrn_gain.reshape(1, width), w_units, *side_f32)
    return outs[0], outs[1], outs[2:]


@jax.jit
def kernel(x, ffn1_norm, ffn1_w_gate, ffn1_w_up, ffn1_w_down, mix_norm, w_in, ret_norm_g, hgrn_lb_logits, hgrn_norm_g, w_out, ffn2_norm, ffn2_w_gate, ffn2_w_up, ffn2_w_down, final_norm):
    batch, seq, d = x.shape
    assert ffn1_norm.shape[0] == 1, "single-layer stack"
    t = batch * seq
    bf = lambda w: w[0].astype(BF16)

    inv = np.power(ROPE_BASE, -np.arange(0, RET_HEAD_DIM, 2, dtype=np.float64) / RET_HEAD_DIM)
    ang = np.arange(seq, dtype=np.float64)[:, None] * inv[None, :]
    cos_sin = jnp.asarray(np.concatenate([np.cos(ang), np.sin(ang)], axis=1), dtype=F32)

    x0 = x.reshape(t, d)
    x1, w_units = _ffn(x0, ffn1_norm[0], bf(ffn1_w_gate), bf(ffn1_w_up), bf(ffn1_w_down),
                       name="ffn1", w_in_f32=w_in[0].astype(F32))
    side = [ffn2_w_gate[0].astype(F32), ffn2_w_up[0].astype(F32), ffn2_w_down[0].astype(F32),
            w_out[0].astype(F32)]
    ret, hgo, (wg2, wu2, wd2, wo) = _mixer(
        x1, mix_norm[0], w_units, hgrn_lb_logits.astype(F32),
        ret_norm_g[0].astype(F32), hgrn_norm_g[0].astype(F32), cos_sin, seq, side)
    x2 = _outproj(x1, ret, hgo, wo)
    out = _ffn(x2, ffn2_norm[0], wg2, wu2, wd2, name="ffn2", final_g=final_norm)
    return out.reshape(batch, seq, d)
```

```python
import functools

import jax
import jax.numpy as jnp
import numpy as np
from jax import lax
from jax.experimental import pallas as pl
from jax.experimental.pallas import tpu as pltpu

F32 = jnp.float32
BF16 = jnp.bfloat16

RET_HEADS = 4
RET_HEAD_DIM = 256
HGRN_HEADS = 8
HGRN_HEAD_DIM = 128
ROPE_BASE = 10000.0
EPS = 1e-6
FFN_RESIDUAL_WEIGHT = 0.5

VMEM_LIMIT_BYTES = 56 * 1024 * 1024
BF16_TILE_ROWS = 16

FFN_TM = 1024
FFN_TF = 512
FFN_EDGE_SPLIT = 2
MIXER_TM = 512
OUTPROJ_TM = 512
RET_CHUNK = 256
HGRN_CHUNK = 64
HGRN_SUB = 8


def _dot(a, b):
    return jnp.dot(a, b, preferred_element_type=F32)


def _dot_nt(a, b):
    return lax.dot_general(a, b, (((1,), (1,)), ((), ())), preferred_element_type=F32)


def _dot_tn(a, b):
    return lax.dot_general(a, b, (((0,), (0,)), ((), ())), preferred_element_type=F32)


def _rmsnorm(x, g):
    return x * lax.rsqrt(jnp.mean(x * x, axis=-1, keepdims=True) + EPS) * g


def _silu(x):
    return x * jax.nn.sigmoid(x)


def _ffn_kernel(*refs, final_norm, cast_side, n_ff_steps):
    refs = list(refs)
    x_ref, g_ref, wg_ref, wu_ref, wd_ref = refs[:5]
    rest = refs[5:]
    fg_ref = rest.pop(0) if final_norm else None
    side_in_ref = rest.pop(0) if cast_side else None
    out_ref = rest.pop(0)
    side_out_ref = rest.pop(0) if cast_side else None
    (h_ref,) = rest

    j = pl.program_id(1)
    tm = x_ref.shape[0]
    part = tm // FFN_EDGE_SPLIT
    parts = [slice(p * part, (p + 1) * part) for p in range(FFN_EDGE_SPLIT)]

    def ride_along_cast():
        if cast_side:
            tn = side_out_ref.shape[2]
            for jj in range(side_out_ref.shape[0]):
                side_out_ref[jj] = side_in_ref[:, jj * tn:(jj + 1) * tn].astype(BF16)

    def ffn_slice(h):
        gate = _dot(h, wg_ref[...])
        up = _dot(h, wu_ref[...])
        act = (_silu(gate) * up * FFN_RESIDUAL_WEIGHT).astype(BF16)
        return _dot(act, wd_ref[...])

    @pl.when(j == 0)
    def _():
        ride_along_cast()
        for rows in parts:
            x = x_ref[rows, :]
            h = _rmsnorm(x, g_ref[...]).astype(BF16)
            h_ref[rows, :] = h
            out_ref[rows, :] = x + ffn_slice(h)

    last = n_ff_steps - 1 if final_norm else n_ff_steps

    @pl.when((j > 0) & (j < last))
    def _():
        ride_along_cast()
        out_ref[...] += ffn_slice(h_ref[...])

    if final_norm:
        @pl.when(j == last)
        def _():
            ride_along_cast()
            for rows in parts:
                y = out_ref[rows, :] + ffn_slice(h_ref[rows, :])
                out_ref[rows, :] = _rmsnorm(y, fg_ref[...])


def _ffn(x, norm_g, w_gate, w_up, w_down, *, name, final_g=None, w_in_f32=None):
    t, d = x.shape
    d_ff = w_gate.shape[1]
    tm, tf = FFN_TM, FFN_TF
    assert t % tm == 0 and d_ff % tf == 0
    n_blocks = t // tm
    n_ff_steps = d_ff // tf
    row = lambda i, j: (i, 0)
    const = lambda i, j: (0, 0)
    in_specs = [pl.BlockSpec((tm, d), row),
                pl.BlockSpec((1, d), const),
                pl.BlockSpec((d, tf), lambda i, j: (0, j)),
                pl.BlockSpec((d, tf), lambda i, j: (0, j)),
                pl.BlockSpec((tf, d), lambda i, j: (j, 0))]
    args = [x, norm_g.reshape(1, d), w_gate, w_up, w_down]
    out_specs = [pl.BlockSpec((tm, d), row)]
    out_shape = [jax.ShapeDtypeStruct((t, d), F32)]
    if final_g is not None:
        in_specs.append(pl.BlockSpec((1, d), const))
        args.append(final_g.reshape(1, d))
    if w_in_f32 is not None:
        n_groups = 8
        tn = RET_HEAD_DIM
        width = w_in_f32.shape[1] // n_groups
        nj = width // tn
        rows = d // n_blocks
        assert d % n_blocks == 0 and rows % BF16_TILE_ROWS == 0 and n_ff_steps >= n_groups
        group = lambda i, j: (i, jnp.minimum(j, n_groups - 1))
        in_specs.append(pl.BlockSpec((rows, width), group))
        args.append(w_in_f32)
        out_specs.append(pl.BlockSpec((nj, None, rows, tn),
                                      lambda i, j: (0, jnp.minimum(j, n_groups - 1), i, 0)))
        out_shape.append(jax.ShapeDtypeStruct((nj, n_groups, d, tn), BF16))
    kern = functools.partial(_ffn_kernel, final_norm=final_g is not None,
                             cast_side=w_in_f32 is not None, n_ff_steps=n_ff_steps)
    outs = pl.pallas_call(
        kern,
        grid=(n_blocks, n_ff_steps),
        in_specs=in_specs,
        out_specs=out_specs,
        out_shape=out_shape,
        scratch_shapes=[pltpu.VMEM((tm, d), BF16)],
        compiler_params=pltpu.CompilerParams(
            dimension_semantics=("parallel", "arbitrary"), vmem_limit_bytes=VMEM_LIMIT_BYTES),
        name=name,
    )(*args)
    return outs if w_in_f32 is not None else outs[0]


def _outproj_kernel(x_ref, ret_ref, hgo_ref, wo_r_ref, wo_h_ref, out_ref):
    out_ref[...] = (x_ref[...] + _dot(ret_ref[...], wo_r_ref[...])
                    + _dot(hgo_ref[...], wo_h_ref[...]))


def _outproj(x1, ret, hgo, w_out):
    t, d = x1.shape
    tm = OUTPROJ_TM
    half = ret.shape[1]
    assert t % tm == 0 and hgo.shape[1] == half and w_out.shape == (2 * half, d)
    row = lambda i: (i, 0)
    return pl.pallas_call(
        _outproj_kernel,
        grid=(t // tm,),
        in_specs=[pl.BlockSpec((tm, d), row),
                  pl.BlockSpec((tm, half), row), pl.BlockSpec((tm, half), row),
                  pl.BlockSpec((half, d), lambda i: (0, 0)), pl.BlockSpec((half, d), lambda i: (1, 0))],
        out_specs=pl.BlockSpec((tm, d), row),
        out_shape=jax.ShapeDtypeStruct((t, d), F32),
        compiler_params=pltpu.CompilerParams(
            dimension_semantics=("parallel",), vmem_limit_bytes=VMEM_LIMIT_BYTES),
        name="outproj",
    )(x1, ret, hgo, w_out, w_out)


class _RetentionHead:
    def __init__(self, q, k, v, head, c):
        tb, dk = q.shape
        self.c = c
        self.chunks = [slice(n * c, (n + 1) * c) for n in range(tb // c)]
        self.q, self.k, self.v = q, k, v

        def log_gamma(shape):
            return jnp.log(1.0 - jnp.exp2(-5.0 - jnp.full(shape, head, jnp.int32).astype(F32)))

        row = lax.broadcasted_iota(jnp.int32, (c, dk), 0).astype(F32)
        rel = (lax.broadcasted_iota(jnp.int32, (c, c), 0) -
               lax.broadcasted_iota(jnp.int32, (c, c), 1)).astype(F32)
        self.decay = jnp.where(rel >= 0, jnp.exp(log_gamma((c, c)) * jnp.maximum(rel, 0.0)), 0.0)
        self.q_dec = jnp.exp(log_gamma((c, dk)) * (row + 1.0))
        self.k_dec = jnp.exp(log_gamma((c, dk)) * (c - 1.0 - row))
        self.g_chunk = jnp.exp(log_gamma((1, dk)) * float(c))

    def free_dots(self):
        self.scores = [_dot_nt(self.q[r], self.k[r]) for r in self.chunks]
        self.updates = [_dot_tn((self.k[r].astype(F32) * self.k_dec).astype(BF16), self.v[r])
                        for r in self.chunks]

    def outputs(self, state):
        states = []
        for upd in self.updates:
            states.append(state.astype(BF16))
            state = self.g_chunk * state + upd
        outs = []
        for r, s, st in zip(self.chunks, self.scores, states):
            inner = _dot((s * self.decay).astype(BF16), self.v[r])
            outs.append(inner + _dot(self.q[r], st) * self.q_dec)
        return outs, state


class _HgrnConsts:
    def __init__(self, c, sub):
        d = HGRN_HEAD_DIM
        pair = BF16_TILE_ROWS
        self.c, self.sub, self.pair = c, sub, pair
        self.tri = (lax.broadcasted_iota(jnp.int32, (c, c), 0) >=
                    lax.broadcasted_iota(jnp.int32, (c, c), 1)).astype(BF16)
        self.pair_lane = lax.broadcasted_iota(jnp.int32, (pair, d), 1) % sub
        a_row = lax.broadcasted_iota(jnp.int32, (c, c), 0)
        a_col = lax.broadcasted_iota(jnp.int32, (c, c), 1)
        self.diag_mask = ((a_row // sub) == (a_col // sub)) & (a_col <= a_row)
        self.halves = []
        h = c // 2
        while h >= sub:
            self.halves.append(h)
            h //= 2
        self.level_masks = [(a_row // (2 * h)) == (a_col // (2 * h)) for h in self.halves]


class _HgrnHead:
    def __init__(self, q, k, v, lf, cp_ref, consts):
        self.q, self.k, self.v, self.lf, self.cp_ref, self.cs = q, k, v, lf, cp_ref, consts
        c = consts.c
        self.chunks = [slice(n * c, (n + 1) * c) for n in range(q.shape[0] // c)]

    def cumsum_dots(self):
        d = HGRN_HEAD_DIM
        self.cums = []
        for rows in self.chunks:
            lf = self.lf[rows]
            lf1 = lf.astype(BF16)
            r1 = lf - lf1.astype(F32)
            lf2 = r1.astype(BF16)
            lf3 = (r1 - lf2.astype(F32)).astype(BF16)
            cum3 = _dot(self.cs.tri, jnp.concatenate([lf1, lf2, lf3], axis=1))
            self.cums.append(cum3[:, :d] + cum3[:, d:2 * d] + cum3[:, 2 * d:])

    def elementwise(self):
        cs, cp_ref = self.cs, self.cp_ref
        c, sub, pair, d = cs.c, cs.sub, cs.pair, HGRN_HEAD_DIM
        self.level_ops, self.a_diags, self.q_ins, self.k_ends, self.decays = [], [], [], [], []
        for rows, cum in zip(self.chunks, self.cums):
            r0 = rows.start
            q = self.q[rows].astype(F32)
            k = self.k[rows].astype(F32)
            cp = cum - jnp.log2(k)
            cp_ref[rows, :] = cp
            total = cum[c - 1:c, :]
            self.q_ins.append((q * jnp.exp2(cum)).astype(BF16))
            self.k_ends.append(jnp.exp2(total - cp).astype(BF16))
            self.decays.append(jnp.exp2(total))

            ops = []
            for h in cs.halves:
                q_parts, k_parts = [], []
                for p0 in range(0, c, 2 * h):
                    lo, up = slice(p0, p0 + h), slice(p0 + h, p0 + 2 * h)
                    c_b = cum[p0 + h - 1:p0 + h, :]
                    q_parts += [jnp.zeros((h, d), F32), q[up] * jnp.exp2(cum[up] - c_b)]
                    k_parts += [jnp.exp2(c_b - cp[lo]), jnp.zeros((h, d), F32)]
                ops.append((jnp.concatenate(q_parts, axis=0).astype(BF16),
                            jnp.concatenate(k_parts, axis=0).astype(BF16)))
            self.level_ops.append(ops)

            pairs = []
            for i0 in range(0, c, pair):
                q_i = q[i0:i0 + pair]
                c_i = cum[i0:i0 + pair]
                a_pair = jnp.zeros((pair, d), F32)
                for j in range(sub):
                    srcs = [cp_ref[r0 + i0 + s + j:r0 + i0 + s + j + 1, :]
                            for s in range(0, pair, sub)]
                    if len(srcs) == 1:
                        cp_j = srcs[0]
                    else:
                        cp_j = jnp.concatenate([jnp.broadcast_to(s, (sub, d)) for s in srcs], axis=0)
                    s_j = jnp.sum(q_i * jnp.exp2(c_i - cp_j), axis=-1, keepdims=True)
                    a_pair = jnp.where(cs.pair_lane == j, s_j, a_pair)
                pairs.append(a_pair)
            self.a_diags.append(jnp.concatenate(pairs, axis=0)[:, :c])

    def free_dots(self):
        self.lows = [[_dot_nt(q_l, k_l) for q_l, k_l in ops] for ops in self.level_ops]
        self.updates = [_dot_tn(self.v[rows], k_end) for rows, k_end in zip(self.chunks, self.k_ends)]

    def outputs(self, state):
        cs = self.cs
        states = []
        for decay, upd in zip(self.decays, self.updates):
            states.append(state.astype(BF16))
            state = decay * state + upd
        outs = []
        for n, rows in enumerate(self.chunks):
            attn = jnp.where(cs.diag_mask, self.a_diags[n], 0.0)
            for h, mask, low in zip(cs.halves, cs.level_masks, self.lows[n]):
                attn = attn + (low if 2 * h == cs.c else jnp.where(mask, low, 0.0))
            outs.append(_dot(attn.astype(BF16), self.v[rows]) + _dot_nt(self.q_ins[n], states[n]))
        return outs, state


_RQ, _RK, _RV, _RG, _HQ, _HK, _HV, _HG = range(8)


def _mixer_kernel(x_ref, g_ref, cos_sin_ref, lbl_ref, rgain_ref, hgain_ref,
                  w_ref, *refs, n_col_blocks, blocks_per_seq, n_side):
    side_in = refs[:n_side]
    ret_ref, hgo_ref = refs[n_side:n_side + 2]
    side_out = refs[n_side + 2:2 * n_side + 2]
    h_ref, act_ref, lf_ref, ret_state_ref, hgrn_state_ref, cp_ref = refs[2 * n_side + 2:]

    s = pl.program_id(0)
    nj = n_col_blocks
    dk, dh = RET_HEAD_DIM, HGRN_HEAD_DIM
    heads_per_step = dk // dh
    cons = jnp.maximum(s - 1, 0)
    cj = cons % nj

    @pl.when(s == 0)
    def _():
        act_ref[...] = jnp.zeros_like(act_ref)
        lf_ref[...] = jnp.zeros_like(lf_ref)

    @pl.when(s % nj == 0)
    def _():
        h_ref[...] = _rmsnorm(x_ref[...], g_ref[...]).astype(BF16)

    @pl.when((cons // nj) % blocks_per_seq == 0)
    def _():
        ret_state_ref[cj] = jnp.zeros((dk, dk), F32)
        for a in range(heads_per_step):
            hgrn_state_ref[heads_per_step * cj + a] = jnp.zeros((dh, dh), F32)

    for i_ref, o_ref in zip(side_in, side_out):
        o_ref[...] = i_ref[...].astype(BF16)

    h = h_ref[...]
    half = dk // 2
    cos = cos_sin_ref[:, :half]
    sin = cos_sin_ref[:, half:]
    dst = act_ref.at[s % 2]
    src = act_ref.at[(s + 1) % 2]
    src_lf = lf_ref.at[(s + 1) % 2]

    def rope(p):
        x1, x2 = p[:, :half], p[:, half:]
        return jnp.concatenate([x1 * cos - x2 * sin, x2 * cos + x1 * sin], axis=-1)

    consts = _HgrnConsts(HGRN_CHUNK, HGRN_SUB)
    hq, hk, hv, hlf = src[_HQ], src[_HK], src[_HV], src_lf[...]
    heads = []
    for a in range(heads_per_step):
        cols = slice(a * dh, (a + 1) * dh)
        heads.append(_HgrnHead(hq[:, cols], hk[:, cols], hv[:, cols], hlf[:, cols],
                               cp_ref.at[a], consts))
    ret = _RetentionHead(src[_RQ], src[_RK], src[_RV], cj, RET_CHUNK)

    def project_pair(first_group):
        return _dot(h, w_ref[first_group]), _dot(h, w_ref[first_group + 1])

    def project_hq_hf():
        p_hq, z = project_pair(_HQ)
        dst[_HQ] = _silu(p_hq).astype(BF16)
        lbl = lbl_ref[...]
        m = jnp.maximum(lbl, 0.0)
        e_l = jnp.exp(lbl - m)
        lb = e_l / (e_l + jnp.exp(-m))
        dst[_HK] = ((1.0 - lb) * jax.nn.sigmoid(-z)).astype(BF16)
        lf_ref[s % 2] = jnp.log2(lb + (1.0 - lb) * jax.nn.sigmoid(z))

    def project_hv_hg():
        p_hv, p_hg = project_pair(_HV)
        dst[_HV] = p_hv.astype(BF16)
        dst[_HG] = _silu(p_hg).astype(BF16)

    def project_rq_rk():
        p_rq, p_rk = project_pair(_RQ)
        dst[_RQ] = (rope(p_rq) * (dk ** -0.5)).astype(BF16)
        dst[_RK] = rope(p_rk).astype(BF16)

    def project_rv_rg():
        p_rv, p_rg = project_pair(_RV)
        dst[_RV] = p_rv.astype(BF16)
        dst[_RG] = _silu(p_rg).astype(BF16)

    def hgrn_outputs():
        hgate = src[_HG]
        for a, hd in enumerate(heads):
            cols = slice(a * dh, (a + 1) * dh)
            outs, state = hd.outputs(hgrn_state_ref[heads_per_step * cj + a])
            hgrn_state_ref[heads_per_step * cj + a] = state
            gain = hgain_ref[:, cols]
            for rows, o in zip(hd.chunks, outs):
                y = o * lax.rsqrt(jnp.mean(o * o, axis=-1, keepdims=True) + EPS)
                hgo_ref[rows, cols] = (y * gain * hgate[rows, cols].astype(F32)).astype(BF16)

    def retention_outputs():
        rgate = src[_RG]
        outs, state = ret.outputs(ret_state_ref[cj])
        ret_state_ref[cj] = state
        gain = rgain_ref[...]
        for rows, o in zip(ret.chunks, outs):
            mu = jnp.mean(o, axis=-1, keepdims=True)
            oc = o - mu
            var = jnp.mean(oc * oc, axis=-1, keepdims=True)
            ret_ref[rows, :] = (oc * lax.rsqrt(var + EPS) * gain
                                * rgate[rows, :].astype(F32)).astype(BF16)

    for hd in heads:
        hd.cumsum_dots()
    project_hq_hf()
    for hd in heads:
        hd.elementwise()
    for hd in heads:
        hd.free_dots()
    ret.free_dots()
    project_hv_hg()
    hgrn_outputs()
    project_rq_rk()
    retention_outputs()
    project_rv_rg()


def _mixer(x1, norm_g, w_units, lb_logits, ret_gain, hgrn_gain, cos_sin, seq, side_f32):
    t, d = x1.shape
    tm = MIXER_TM
    tn = RET_HEAD_DIM
    dh = HGRN_HEAD_DIM
    width = RET_HEADS * RET_HEAD_DIM
    assert w_units.shape == (width // tn, 8, d, tn) and width == HGRN_HEADS * dh and tn % dh == 0
    assert t % tm == 0 and seq % tm == 0
    assert tm % RET_CHUNK == 0 and tm % HGRN_CHUNK == 0
    assert HGRN_CHUNK % BF16_TILE_ROWS == 0 and BF16_TILE_ROWS % HGRN_SUB == 0
    nj = width // tn
    blocks_per_seq = seq // tm
    n_units = (t // tm) * nj
    prod = lambda s: jnp.minimum(s, n_units - 1)
    cons = lambda s: jnp.maximum(s - 1, 0)
    const = lambda s: (0, 0)
    pos = lambda s: ((prod(s) // nj) % blocks_per_seq, 0)
    cons_col = lambda s: (0, cons(s) % nj)

    side_specs = []
    for w in side_f32:
        rows = next(r for r in range(BF16_TILE_ROWS, w.shape[0] + 1, BF16_TILE_ROWS)
                    if w.shape[0] % r == 0 and w.shape[0] // r <= n_units)
        side_specs.append(pl.BlockSpec(
            (rows, w.shape[1]), lambda s, last=w.shape[0] // rows - 1: (jnp.minimum(s, last), 0)))

    out_spec = pl.BlockSpec((tm, tn), lambda s: (cons(s) // nj, cons(s) % nj))
    out_bf = jax.ShapeDtypeStruct((t, width), BF16)
    outs = pl.pallas_call(
        functools.partial(_mixer_kernel, n_col_blocks=nj, blocks_per_seq=blocks_per_seq,
                          n_side=len(side_f32)),
        grid=(n_units + 1,),
        in_specs=[pl.BlockSpec((tm, d), lambda s: (prod(s) // nj, 0)), pl.BlockSpec((1, d), const),
                  pl.BlockSpec((tm, tn), pos),
                  pl.BlockSpec((1, tn), lambda s: (0, prod(s) % nj)),
                  pl.BlockSpec((1, tn), cons_col), pl.BlockSpec((1, tn), cons_col),
                  pl.BlockSpec((None, 8, d, tn), lambda s: (prod(s) % nj, 0, 0, 0))] + side_specs,
        out_specs=[out_spec, out_spec] + side_specs,
        out_shape=[out_bf, out_bf] + [jax.ShapeDtypeStruct(w.shape, BF16) for w in side_f32],
        scratch_shapes=[pltpu.VMEM((tm, d), BF16),
                        pltpu.VMEM((2, 8, tm, tn), BF16),
                        pltpu.VMEM((2, tm, tn), F32),
                        pltpu.VMEM((RET_HEADS, RET_HEAD_DIM, RET_HEAD_DIM), F32),
                        pltpu.VMEM((HGRN_HEADS, dh, dh), F32),
                        pltpu.VMEM((tn // dh, tm, dh), F32)],
        compiler_params=pltpu.CompilerParams(
            dimension_semantics=("arbitrary",), vmem_limit_bytes=VMEM_LIMIT_BYTES),
        name="mixer",
    )(x1, norm_g.reshape(1, d), cos_sin, lb_logits, ret_gain.reshape(1, width),
      hgrn_gain.reshape(1, width), w_units, *side_f32)
    return outs[0], outs[1], outs[2:]


@jax.jit
def kernel(x, ffn1_norm, ffn1_w_gate, ffn1_w_up, ffn1_w_down, mix_norm, w_in, ret_norm_g, hgrn_lb_logits, hgrn_norm_g, w_out, ffn2_norm, ffn2_w_gate, ffn2_w_up, ffn2_w_down, final_norm):
    batch, seq, d = x.shape
    assert ffn1_norm.shape[0] == 1, "single-layer stack"
    t = batch * seq
    bf = lambda w: w[0].astype(BF16)

    inv = np.power(ROPE_BASE, -np.arange(0, RET_HEAD_DIM, 2, dtype=np.float64) / RET_HEAD_DIM)
    ang = np.arange(seq, dtype=np.float64)[:, None] * inv[None, :]
    cos_sin = jnp.asarray(np.concatenate([np.cos(ang), np.sin(ang)], axis=1), dtype=F32)

    x0 = x.reshape(t, d)
    x1, w_units = _ffn(x0, ffn1_norm[0], bf(ffn1_w_gate), bf(ffn1_w_up), bf(ffn1_w_down),
                       name="ffn1", w_in_f32=w_in[0].astype(F32))
    side = [ffn2_w_gate[0].astype(F32), ffn2_w_up[0].astype(F32), ffn2_w_down[0].astype(F32),
            w_out[0].astype(F32)]
    ret, hgo, (wg2, wu2, wd2, wo) = _mixer(
        x1, mix_norm[0], w_units, hgrn_lb_logits.astype(F32),
        ret_norm_g[0].astype(F32), hgrn_norm_g[0].astype(F32), cos_sin, seq, side)
    x2 = _outproj(x1, ret, hgo, wo)
    out = _ffn(x2, ffn2_norm[0], wg2, wu2, wd2, name="ffn2", final_g=final_norm)
    return out.reshape(batch, seq, d)
```

```python
import functools

import jax
import jax.numpy as jnp
import numpy as np
from jax import lax
from jax.experimental import pallas as pl
from jax.experimental.pallas import tpu as pltpu

F32 = jnp.float32
BF16 = jnp.bfloat16

RET_HEADS = 4
RET_HEAD_DIM = 256
HGRN_HEADS = 8
HGRN_HEAD_DIM = 128
ROPE_BASE = 10000.0
EPS = 1e-6
FFN_RESIDUAL_WEIGHT = 0.5

VMEM_LIMIT_BYTES = 56 * 1024 * 1024
BF16_TILE_ROWS = 16

FFN_TM = 1024
FFN_TF = 512
FFN_EDGE_SPLIT = 2
MIXER_TM = 512
OUTPROJ_TM = 512
RET_CHUNK = 256
HGRN_CHUNK = 64
HGRN_SUB = 8


def _dot(a, b):
    return jnp.dot(a, b, preferred_element_type=F32)


def _dot_nt(a, b):
    return lax.dot_general(a, b, (((1,), (1,)), ((), ())), preferred_element_type=F32)


def _dot_tn(a, b):
    return lax.dot_general(a, b, (((0,), (0,)), ((), ())), preferred_element_type=F32)


def _rmsnorm(x, g):
    return x * lax.rsqrt(jnp.mean(x * x, axis=-1, keepdims=True) + EPS) * g


def _silu(x):
    return x * jax.nn.sigmoid(x)


def _ffn_kernel(*refs, final_norm, cast_side, n_ff_steps):
    refs = list(refs)
    x_ref, g_ref, wg_ref, wu_ref, wd_ref = refs[:5]
    rest = refs[5:]
    fg_ref = rest.pop(0) if final_norm else None
    side_in_ref = rest.pop(0) if cast_side else None
    out_ref = rest.pop(0)
    side_out_ref = rest.pop(0) if cast_side else None
    (h_ref,) = rest

    j = pl.program_id(1)
    tm = x_ref.shape[0]
    part = tm // FFN_EDGE_SPLIT
    parts = [slice(p * part, (p + 1) * part) for p in range(FFN_EDGE_SPLIT)]

    def ride_along_cast():
        if cast_side:
            tn = side_out_ref.shape[2]
            for jj in range(side_out_ref.shape[0]):
                side_out_ref[jj] = side_in_ref[:, jj * tn:(jj + 1) * tn].astype(BF16)

    def ffn_slice(h):
        gate = _dot(h, wg_ref[...])
        up = _dot(h, wu_ref[...])
        act = (_silu(gate) * up * FFN_RESIDUAL_WEIGHT).astype(BF16)
        return _dot(act, wd_ref[...])

    @pl.when(j == 0)
    def _():
        ride_along_cast()
        for rows in parts:
            x = x_ref[rows, :]
            h = _rmsnorm(x, g_ref[...]).astype(BF16)
            h_ref[rows, :] = h
            out_ref[rows, :] = x + ffn_slice(h)

    last = n_ff_steps - 1 if final_norm else n_ff_steps

    @pl.when((j > 0) & (j < last))
    def _():
        ride_along_cast()
        out_ref[...] += ffn_slice(h_ref[...])

    if final_norm:
        @pl.when(j == last)
        def _():
            ride_along_cast()
            for rows in parts:
                y = out_ref[rows, :] + ffn_slice(h_ref[rows, :])
                out_ref[rows, :] = _rmsnorm(y, fg_ref[...])


def _ffn(x, norm_g, w_gate, w_up, w_down, *, name, final_g=None, w_in_f32=None):
    t, d = x.shape
    d_ff = w_gate.shape[1]
    tm, tf = FFN_TM, FFN_TF
    assert t % tm == 0 and d_ff % tf == 0
    n_blocks = t // tm
    n_ff_steps = d_ff // tf
    row = lambda i, j: (i, 0)
    const = lambda i, j: (0, 0)
    in_specs = [pl.BlockSpec((tm, d), row),
                pl.BlockSpec((1, d), const),
                pl.BlockSpec((d, tf), lambda i, j: (0, j)),
                pl.BlockSpec((d, tf), lambda i, j: (0, j)),
                pl.BlockSpec((tf, d), lambda i, j: (j, 0))]
    args = [x, norm_g.reshape(1, d), w_gate, w_up, w_down]
    out_specs = [pl.BlockSpec((tm, d), row)]
    out_shape = [jax.ShapeDtypeStruct((t, d), F32)]
    if final_g is not None:
        in_specs.append(pl.BlockSpec((1, d), const))
        args.append(final_g.reshape(1, d))
    if w_in_f32 is not None:
        n_groups = 8
        tn = RET_HEAD_DIM
        width = w_in_f32.shape[1] // n_groups
        nj = width // tn
        rows = d // n_blocks
        assert d % n_blocks == 0 and rows % BF16_TILE_ROWS == 0 and n_ff_steps >= n_groups
        group = lambda i, j: (i, jnp.minimum(j, n_groups - 1))
        in_specs.append(pl.BlockSpec((rows, width), group))
        args.append(w_in_f32)
        out_specs.append(pl.BlockSpec((nj, None, rows, tn),
                                      lambda i, j: (0, jnp.minimum(j, n_groups - 1), i, 0)))
        out_shape.append(jax.ShapeDtypeStruct((nj, n_groups, d, tn), BF16))
    kern = functools.partial(_ffn_kernel, final_norm=final_g is not None,
                             cast_side=w_in_f32 is not None, n_ff_steps=n_ff_steps)
    outs = pl.pallas_call(
        kern,
        grid=(n_blocks, n_ff_steps),
        in_specs=in_specs,
        out_specs=out_specs,
        out_shape=out_shape,
        scratch_shapes=[pltpu.VMEM((tm, d), BF16)],
        compiler_params=pltpu.CompilerParams(
            dimension_semantics=("parallel", "arbitrary"), vmem_limit_bytes=VMEM_LIMIT_BYTES,
            allow_input_fusion=[i in (2, 3, 4) and w_in_f32 is not None for i in range(len(args))]),
        name=name,
    )(*args)
    return outs if w_in_f32 is not None else outs[0]


def _outproj_kernel(x_ref, ret_ref, hgo_ref, wo_r_ref, wo_h_ref, out_ref):
    out_ref[...] = (x_ref[...] + _dot(ret_ref[...], wo_r_ref[...])
                    + _dot(hgo_ref[...], wo_h_ref[...]))


def _outproj(x1, ret, hgo, w_out):
    t, d = x1.shape
    tm = OUTPROJ_TM
    half = ret.shape[1]
    assert t % tm == 0 and hgo.shape[1] == half and w_out.shape == (2 * half, d)
    row = lambda i: (i, 0)
    return pl.pallas_call(
        _outproj_kernel,
        grid=(t // tm,),
        in_specs=[pl.BlockSpec((tm, d), row),
                  pl.BlockSpec((tm, half), row), pl.BlockSpec((tm, half), row),
                  pl.BlockSpec((half, d), lambda i: (0, 0)), pl.BlockSpec((half, d), lambda i: (1, 0))],
        out_specs=pl.BlockSpec((tm, d), row),
        out_shape=jax.ShapeDtypeStruct((t, d), F32),
        compiler_params=pltpu.CompilerParams(
            dimension_semantics=("parallel",), vmem_limit_bytes=VMEM_LIMIT_BYTES),
        name="outproj",
    )(x1, ret, hgo, w_out, w_out)


class _RetentionHead:
    def __init__(self, q, k, v, head, c):
        tb, dk = q.shape
        self.c = c
        self.chunks = [slice(n * c, (n + 1) * c) for n in range(tb // c)]
        self.q, self.k, self.v = q, k, v

        def log_gamma(shape):
            return jnp.log(1.0 - jnp.exp2(-5.0 - jnp.full(shape, head, jnp.int32).astype(F32)))

        row = lax.broadcasted_iota(jnp.int32, (c, dk), 0).astype(F32)
        rel = (lax.broadcasted_iota(jnp.int32, (c, c), 0) -
               lax.broadcasted_iota(jnp.int32, (c, c), 1)).astype(F32)
        self.decay = jnp.where(rel >= 0, jnp.exp(log_gamma((c, c)) * jnp.maximum(rel, 0.0)), 0.0)
        self.q_dec = jnp.exp(log_gamma((c, dk)) * (row + 1.0))
        self.k_dec = jnp.exp(log_gamma((c, dk)) * (c - 1.0 - row))
        self.g_chunk = jnp.exp(log_gamma((1, dk)) * float(c))

    def free_dots(self):
        self.scores = [_dot_nt(self.q[r], self.k[r]) for r in self.chunks]
        self.updates = [_dot_tn((self.k[r].astype(F32) * self.k_dec).astype(BF16), self.v[r])
                        for r in self.chunks]

    def outputs(self, state):
        states = []
        for upd in self.updates:
            states.append(state.astype(BF16))
            state = self.g_chunk * state + upd
        outs = []
        for r, s, st in zip(self.chunks, self.scores, states):
            inner = _dot((s * self.decay).astype(BF16), self.v[r])
            outs.append(inner + _dot(self.q[r], st) * self.q_dec)
        return outs, state


class _HgrnConsts:
    def __init__(self, c, sub):
        d = HGRN_HEAD_DIM
        pair = BF16_TILE_ROWS
        self.c, self.sub, self.pair = c, sub, pair
        self.tri = (lax.broadcasted_iota(jnp.int32, (c, c), 0) >=
                    lax.broadcasted_iota(jnp.int32, (c, c), 1)).astype(BF16)
        self.pair_lane = lax.broadcasted_iota(jnp.int32, (pair, d), 1) % sub
        a_row = lax.broadcasted_iota(jnp.int32, (c, c), 0)
        a_col = lax.broadcasted_iota(jnp.int32, (c, c), 1)
        self.diag_mask = ((a_row // sub) == (a_col // sub)) & (a_col <= a_row)
        self.halves = []
        h = c // 2
        while h >= sub:
            self.halves.append(h)
            h //= 2
        self.level_masks = [(a_row // (2 * h)) == (a_col // (2 * h)) for h in self.halves]


class _HgrnHead:
    def __init__(self, q, k, v, lf, cp_ref, consts):
        self.q, self.k, self.v, self.lf, self.cp_ref, self.cs = q, k, v, lf, cp_ref, consts
        c = consts.c
        self.chunks = [slice(n * c, (n + 1) * c) for n in range(q.shape[0] // c)]

    def cumsum_dots(self):
        d = HGRN_HEAD_DIM
        self.cums = []
        for rows in self.chunks:
            lf = self.lf[rows]
            lf1 = lf.astype(BF16)
            r1 = lf - lf1.astype(F32)
            lf2 = r1.astype(BF16)
            lf3 = (r1 - lf2.astype(F32)).astype(BF16)
            cum3 = _dot(self.cs.tri, jnp.concatenate([lf1, lf2, lf3], axis=1))
            self.cums.append(cum3[:, :d] + cum3[:, d:2 * d] + cum3[:, 2 * d:])

    def elementwise(self):
        cs, cp_ref = self.cs, self.cp_ref
        c, sub, pair, d = cs.c, cs.sub, cs.pair, HGRN_HEAD_DIM
        self.level_ops, self.a_diags, self.q_ins, self.k_ends, self.decays = [], [], [], [], []
        for rows, cum in zip(self.chunks, self.cums):
            r0 = rows.start
            q = self.q[rows].astype(F32)
            k = self.k[rows].astype(F32)
            cp = cum - jnp.log2(k)
            cp_ref[rows, :] = cp
            total = cum[c - 1:c, :]
            self.q_ins.append((q * jnp.exp2(cum)).astype(BF16))
            self.k_ends.append(jnp.exp2(total - cp).astype(BF16))
            self.decays.append(jnp.exp2(total))

            ops = []
            for h in cs.halves:
                q_parts, k_parts = [], []
                for p0 in range(0, c, 2 * h):
                    lo, up = slice(p0, p0 + h), slice(p0 + h, p0 + 2 * h)
                    c_b = cum[p0 + h - 1:p0 + h, :]
                    q_parts += [jnp.zeros((h, d), F32), q[up] * jnp.exp2(cum[up] - c_b)]
                    k_parts += [jnp.exp2(c_b - cp[lo]), jnp.zeros((h, d), F32)]
                ops.append((jnp.concatenate(q_parts, axis=0).astype(BF16),
                            jnp.concatenate(k_parts, axis=0).astype(BF16)))
            self.level_ops.append(ops)

            pairs = []
            for i0 in range(0, c, pair):
                q_i = q[i0:i0 + pair]
                c_i = cum[i0:i0 + pair]
                a_pair = jnp.zeros((pair, d), F32)
                for j in range(sub):
                    srcs = [cp_ref[r0 + i0 + s + j:r0 + i0 + s + j + 1, :]
                            for s in range(0, pair, sub)]
                    if len(srcs) == 1:
                        cp_j = srcs[0]
                    else:
                        cp_j = jnp.concatenate([jnp.broadcast_to(s, (sub, d)) for s in srcs], axis=0)
                    s_j = jnp.sum(q_i * jnp.exp2(c_i - cp_j), axis=-1, keepdims=True)
                    a_pair = jnp.where(cs.pair_lane == j, s_j, a_pair)
                pairs.append(a_pair)
            self.a_diags.append(jnp.concatenate(pairs, axis=0)[:, :c])

    def free_dots(self):
        self.lows = [[_dot_nt(q_l, k_l) for q_l, k_l in ops] for ops in self.level_ops]
        self.updates = [_dot_tn(self.v[rows], k_end) for rows, k_end in zip(self.chunks, self.k_ends)]

    def outputs(self, state):
        cs = self.cs
        states = []
        for decay, upd in zip(self.decays, self.updates):
            states.append(state.astype(BF16))
            state = decay * state + upd
        outs = []
        for n, rows in enumerate(self.chunks):
            attn = jnp.where(cs.diag_mask, self.a_diags[n], 0.0)
            for h, mask, low in zip(cs.halves, cs.level_masks, self.lows[n]):
                attn = attn + (low if 2 * h == cs.c else jnp.where(mask, low, 0.0))
            outs.append(_dot(attn.astype(BF16), self.v[rows]) + _dot_nt(self.q_ins[n], states[n]))
        return outs, state


_RQ, _RK, _RV, _RG, _HQ, _HK, _HV, _HG = range(8)


def _mixer_kernel(x_ref, g_ref, cos_sin_ref, lbl_ref, rgain_ref, hgain_ref,
                  w_ref, *refs, n_col_blocks, blocks_per_seq, n_side):
    side_in = refs[:n_side]
    ret_ref, hgo_ref = refs[n_side:n_side + 2]
    side_out = refs[n_side + 2:2 * n_side + 2]
    h_ref, act_ref, lf_ref, ret_state_ref, hgrn_state_ref, cp_ref = refs[2 * n_side + 2:]

    s = pl.program_id(0)
    nj = n_col_blocks
    dk, dh = RET_HEAD_DIM, HGRN_HEAD_DIM
    heads_per_step = dk // dh
    cons = jnp.maximum(s - 1, 0)
    cj = cons % nj

    @pl.when(s == 0)
    def _():
        act_ref[...] = jnp.zeros_like(act_ref)
        lf_ref[...] = jnp.zeros_like(lf_ref)

    @pl.when(s % nj == 0)
    def _():
        h_ref[...] = _rmsnorm(x_ref[...], g_ref[...]).astype(BF16)

    @pl.when((cons // nj) % blocks_per_seq == 0)
    def _():
        ret_state_ref[cj] = jnp.zeros((dk, dk), F32)
        for a in range(heads_per_step):
            hgrn_state_ref[heads_per_step * cj + a] = jnp.zeros((dh, dh), F32)

    for i_ref, o_ref in zip(side_in, side_out):
        o_ref[...] = i_ref[...].astype(BF16)

    h = h_ref[...]
    half = dk // 2
    cos = cos_sin_ref[:, :half]
    sin = cos_sin_ref[:, half:]
    dst = act_ref.at[s % 2]
    src = act_ref.at[(s + 1) % 2]
    src_lf = lf_ref.at[(s + 1) % 2]

    def rope(p):
        x1, x2 = p[:, :half], p[:, half:]
        return jnp.concatenate([x1 * cos - x2 * sin, x2 * cos + x1 * sin], axis=-1)

    consts = _HgrnConsts(HGRN_CHUNK, HGRN_SUB)
    hq, hk, hv, hlf = src[_HQ], src[_HK], src[_HV], src_lf[...]
    heads = []
    for a in range(heads_per_step):
        cols = slice(a * dh, (a + 1) * dh)
        heads.append(_HgrnHead(hq[:, cols], hk[:, cols], hv[:, cols], hlf[:, cols],
                               cp_ref.at[a], consts))
    ret = _RetentionHead(src[_RQ], src[_RK], src[_RV], cj, RET_CHUNK)

    def project_pair(first_group):
        return _dot(h, w_ref[first_group]), _dot(h, w_ref[first_group + 1])

    def project_hq_hf():
        p_hq, z = project_pair(_HQ)
        dst[_HQ] = _silu(p_hq).astype(BF16)
        lbl = lbl_ref[...]
        m = jnp.maximum(lbl, 0.0)
        e_l = jnp.exp(lbl - m)
        lb = e_l / (e_l + jnp.exp(-m))
        dst[_HK] = ((1.0 - lb) * jax.nn.sigmoid(-z)).astype(BF16)
        lf_ref[s % 2] = jnp.log2(lb + (1.0 - lb) * jax.nn.sigmoid(z))

    def project_hv_hg():
        p_hv, p_hg = project_pair(_HV)
        dst[_HV] = p_hv.astype(BF16)
        dst[_HG] = _silu(p_hg).astype(BF16)

    def project_rq_rk():
        p_rq, p_rk = project_pair(_RQ)
        dst[_RQ] = (rope(p_rq) * (dk ** -0.5)).astype(BF16)
        dst[_RK] = rope(p_rk).astype(BF16)

    def project_rv_rg():
        p_rv, p_rg = project_pair(_RV)
        dst[_RV] = p_rv.astype(BF16)
        dst[_RG] = _silu(p_rg).astype(BF16)

    def hgrn_outputs():
        hgate = src[_HG]
        for a, hd in enumerate(heads):
            cols = slice(a * dh, (a + 1) * dh)
            outs, state = hd.outputs(hgrn_state_ref[heads_per_step * cj + a])
            hgrn_state_ref[heads_per_step * cj + a] = state
            gain = hgain_ref[:, cols]
            for rows, o in zip(hd.chunks, outs):
                y = o * lax.rsqrt(jnp.mean(o * o, axis=-1, keepdims=True) + EPS)
                hgo_ref[rows, cols] = (y * gain * hgate[rows, cols].astype(F32)).astype(BF16)

    def retention_outputs():
        rgate = src[_RG]
        outs, state = ret.outputs(ret_state_ref[cj])
        ret_state_ref[cj] = state
        gain = rgain_ref[...]
        for rows, o in zip(ret.chunks, outs):
            mu = jnp.mean(o, axis=-1, keepdims=True)
            oc = o - mu
            var = jnp.mean(oc * oc, axis=-1, keepdims=True)
            ret_ref[rows, :] = (oc * lax.rsqrt(var + EPS) * gain
                                * rgate[rows, :].astype(F32)).astype(BF16)

    for hd in heads:
        hd.cumsum_dots()
    project_hq_hf()
    for hd in heads:
        hd.elementwise()
    for hd in heads:
        hd.free_dots()
    ret.free_dots()
    project_hv_hg()
    hgrn_outputs()
    project_rq_rk()
    retention_outputs()
    project_rv_rg()


def _mixer(x1, norm_g, w_units, lb_logits, ret_gain, hgrn_gain, cos_sin, seq, side_f32):
    t, d = x1.shape
    tm = MIXER_TM
    tn = RET_HEAD_DIM
    dh = HGRN_HEAD_DIM
    width = RET_HEADS * RET_HEAD_DIM
    assert w_units.shape == (width // tn, 8, d, tn) and width == HGRN_HEADS * dh and tn % dh == 0
    assert t % tm == 0 and seq % tm == 0
    assert tm % RET_CHUNK == 0 and tm % HGRN_CHUNK == 0
    assert HGRN_CHUNK % BF16_TILE_ROWS == 0 and BF16_TILE_ROWS % HGRN_SUB == 0
    nj = width // tn
    blocks_per_seq = seq // tm
    n_units = (t // tm) * nj
    prod = lambda s: jnp.minimum(s, n_units - 1)
    cons = lambda s: jnp.maximum(s - 1, 0)
    const = lambda s: (0, 0)
    pos = lambda s: ((prod(s) // nj) % blocks_per_seq, 0)
    cons_col = lambda s: (0, cons(s) % nj)

    side_specs = []
    for w in side_f32:
        rows = next(r for r in range(BF16_TILE_ROWS, w.shape[0] + 1, BF16_TILE_ROWS)
                    if w.shape[0] % r == 0 and w.shape[0] // r <= n_units)
        side_specs.append(pl.BlockSpec(
            (rows, w.shape[1]), lambda s, last=w.shape[0] // rows - 1: (jnp.minimum(s, last), 0)))

    out_spec = pl.BlockSpec((tm, tn), lambda s: (cons(s) // nj, cons(s) % nj))
    out_bf = jax.ShapeDtypeStruct((t, width), BF16)
    outs = pl.pallas_call(
        functools.partial(_mixer_kernel, n_col_blocks=nj, blocks_per_seq=blocks_per_seq,
                          n_side=len(side_f32)),
        grid=(n_units + 1,),
        in_specs=[pl.BlockSpec((tm, d), lambda s: (prod(s) // nj, 0)), pl.BlockSpec((1, d), const),
                  pl.BlockSpec((tm, tn), pos),
                  pl.BlockSpec((1, tn), lambda s: (0, prod(s) % nj)),
                  pl.BlockSpec((1, tn), cons_col), pl.BlockSpec((1, tn), cons_col),
                  pl.BlockSpec((None, 8, d, tn), lambda s: (prod(s) % nj, 0, 0, 0))] + side_specs,
        out_specs=[out_spec, out_spec] + side_specs,
        out_shape=[out_bf, out_bf] + [jax.ShapeDtypeStruct(w.shape, BF16) for w in side_f32],
        scratch_shapes=[pltpu.VMEM((tm, d), BF16),
                        pltpu.VMEM((2, 8, tm, tn), BF16),
                        pltpu.VMEM((2, tm, tn), F32),
                        pltpu.VMEM((RET_HEADS, RET_HEAD_DIM, RET_HEAD_DIM), F32),
                        pltpu.VMEM((HGRN_HEADS, dh, dh), F32),
                        pltpu.VMEM((tn // dh, tm, dh), F32)],
        compiler_params=pltpu.CompilerParams(
            dimension_semantics=("arbitrary",), vmem_limit_bytes=VMEM_LIMIT_BYTES),
        name="mixer",
    )(x1, norm_g.reshape(1, d), cos_sin, lb_logits, ret_gain.reshape(1, width),
      hgrn_gain.reshape(1, width), w_units, *side_f32)
    return outs[0], outs[1], outs[2:]


@jax.jit
def kernel(x, ffn1_norm, ffn1_w_gate, ffn1_w_up, ffn1_w_down, mix_norm, w_in, ret_norm_g, hgrn_lb_logits, hgrn_norm_g, w_out, ffn2_norm, ffn2_w_gate, ffn2_w_up, ffn2_w_down, final_norm):
    batch, seq, d = x.shape
    assert ffn1_norm.shape[0] == 1, "single-layer stack"
    t = batch * seq
    bf = lambda w: w[0].astype(BF16)

    inv = np.power(ROPE_BASE, -np.arange(0, RET_HEAD_DIM, 2, dtype=np.float64) / RET_HEAD_DIM)
    ang = np.arange(seq, dtype=np.float64)[:, None] * inv[None, :]
    cos_sin = jnp.asarray(np.concatenate([np.cos(ang), np.sin(ang)], axis=1), dtype=F32)

    x0 = x.reshape(t, d)
    x1, w_units = _ffn(x0, ffn1_norm[0], bf(ffn1_w_gate), bf(ffn1_w_up), bf(ffn1_w_down),
                       name="ffn1", w_in_f32=w_in[0].astype(F32))
    side = [ffn2_w_gate[0].astype(F32), ffn2_w_up[0].astype(F32), ffn2_w_down[0].astype(F32),
            w_out[0].astype(F32)]
    ret, hgo, (wg2, wu2, wd2, wo) = _mixer(
        x1, mix_norm[0], w_units, hgrn_lb_logits.astype(F32),
        ret_norm_g[0].astype(F32), hgrn_norm_g[0].astype(F32), cos_sin, seq, side)
    x2 = _outproj(x1, ret, hgo, wo)
    out = _ffn(x2, ffn2_norm[0], wg2, wu2, wd2, name="ffn2", final_g=final_norm)
    return out.reshape(batch, seq, d)
```

```python
import functools

import jax
import jax.numpy as jnp
import numpy as np
from jax import lax
from jax.experimental import pallas as pl
from jax.experimental.pallas import tpu as pltpu

F32 = jnp.float32
BF16 = jnp.bfloat16

RET_HEADS = 4
RET_HEAD_DIM = 256
HGRN_HEADS = 8
HGRN_HEAD_DIM = 128
ROPE_BASE = 10000.0
EPS = 1e-6
FFN_RESIDUAL_WEIGHT = 0.5

VMEM_LIMIT_BYTES = 56 * 1024 * 1024
BF16_TILE_ROWS = 16

FFN_TM = 1024
FFN_TF = 512
FFN_EDGE_SPLIT = 2
MIXER_TM = 512
OUTPROJ_TM = 512
RET_CHUNK = 256
HGRN_CHUNK = 64
HGRN_SUB = 8


def _dot(a, b):
    return jnp.dot(a, b, preferred_element_type=F32)


def _dot_nt(a, b):
    return lax.dot_general(a, b, (((1,), (1,)), ((), ())), preferred_element_type=F32)


def _dot_tn(a, b):
    return lax.dot_general(a, b, (((0,), (0,)), ((), ())), preferred_element_type=F32)


def _rmsnorm(x, g):
    return x * lax.rsqrt(jnp.mean(x * x, axis=-1, keepdims=True) + EPS) * g


def _silu(x):
    return x * jax.nn.sigmoid(x)


def _ffn_kernel(*refs, final_norm, cast_side, n_ff_steps):
    refs = list(refs)
    x_ref, g_ref, wg_ref, wu_ref, wd_ref = refs[:5]
    rest = refs[5:]
    fg_ref = rest.pop(0) if final_norm else None
    side_in_ref = rest.pop(0) if cast_side else None
    out_ref = rest.pop(0)
    side_out_ref = rest.pop(0) if cast_side else None
    (h_ref,) = rest

    j = pl.program_id(1)
    tm = x_ref.shape[0]
    part = tm // FFN_EDGE_SPLIT
    parts = [slice(p * part, (p + 1) * part) for p in range(FFN_EDGE_SPLIT)]

    def ride_along_cast():
        if cast_side:
            tn = side_out_ref.shape[2]
            for jj in range(side_out_ref.shape[0]):
                side_out_ref[jj] = side_in_ref[:, jj * tn:(jj + 1) * tn].astype(BF16)

    def ffn_slice(h):
        gate = _dot(h, wg_ref[...])
        up = _dot(h, wu_ref[...])
        act = (_silu(gate) * up * FFN_RESIDUAL_WEIGHT).astype(BF16)
        return _dot(act, wd_ref[...])

    @pl.when(j == 0)
    def _():
        ride_along_cast()
        for rows in parts:
            x = x_ref[rows, :]
            h = _rmsnorm(x, g_ref[...]).astype(BF16)
            h_ref[rows, :] = h
            out_ref[rows, :] = x + ffn_slice(h)

    last = n_ff_steps - 1 if final_norm else n_ff_steps

    @pl.when((j > 0) & (j < last))
    def _():
        ride_along_cast()
        out_ref[...] += ffn_slice(h_ref[...])

    if final_norm:
        @pl.when(j == last)
        def _():
            ride_along_cast()
            for rows in parts:
                y = out_ref[rows, :] + ffn_slice(h_ref[rows, :])
                out_ref[rows, :] = _rmsnorm(y, fg_ref[...])


def _ffn(x, norm_g, w_gate, w_up, w_down, *, name, final_g=None, w_in_f32=None):
    t, d = x.shape
    d_ff = w_gate.shape[1]
    tm, tf = FFN_TM, FFN_TF
    assert t % tm == 0 and d_ff % tf == 0
    n_blocks = t // tm
    n_ff_steps = d_ff // tf
    row = lambda i, j: (i, 0)
    const = lambda i, j: (0, 0)
    in_specs = [pl.BlockSpec((tm, d), row),
                pl.BlockSpec((1, d), const),
                pl.BlockSpec((d, tf), lambda i, j: (0, j)),
                pl.BlockSpec((d, tf), lambda i, j: (0, j)),
                pl.BlockSpec((tf, d), lambda i, j: (j, 0))]
    args = [x, norm_g.reshape(1, d), w_gate, w_up, w_down]
    out_specs = [pl.BlockSpec((tm, d), row)]
    out_shape = [jax.ShapeDtypeStruct((t, d), F32)]
    if final_g is not None:
        in_specs.append(pl.BlockSpec((1, d), const))
        args.append(final_g.reshape(1, d))
    if w_in_f32 is not None:
        n_groups = 8
        tn = RET_HEAD_DIM
        width = w_in_f32.shape[1] // n_groups
        nj = width // tn
        rows = d // n_blocks
        assert d % n_blocks == 0 and rows % BF16_TILE_ROWS == 0 and n_ff_steps >= n_groups
        group = lambda i, j: (i, jnp.minimum(j, n_groups - 1))
        in_specs.append(pl.BlockSpec((rows, width), group))
        args.append(w_in_f32)
        out_specs.append(pl.BlockSpec((nj, None, rows, tn),
                                      lambda i, j: (0, jnp.minimum(j, n_groups - 1), i, 0)))
        out_shape.append(jax.ShapeDtypeStruct((nj, n_groups, d, tn), BF16))
    kern = functools.partial(_ffn_kernel, final_norm=final_g is not None,
                             cast_side=w_in_f32 is not None, n_ff_steps=n_ff_steps)
    outs = pl.pallas_call(
        kern,
        grid=(n_blocks, n_ff_steps),
        in_specs=in_specs,
        out_specs=out_specs,
        out_shape=out_shape,
        scratch_shapes=[pltpu.VMEM((tm, d), BF16)],
        compiler_params=pltpu.CompilerParams(
            dimension_semantics=("parallel", "arbitrary"), vmem_limit_bytes=VMEM_LIMIT_BYTES),
        name=name,
    )(*args)
    return outs if w_in_f32 is not None else outs[0]


def _outproj_kernel(x_ref, ret_ref, hgo_ref, wo_r_ref, wo_h_ref, out_ref):
    out_ref[...] = (x_ref[...] + _dot(ret_ref[...], wo_r_ref[...])
                    + _dot(hgo_ref[...], wo_h_ref[...]))


def _outproj(x1, ret, hgo, w_out):
    t, d = x1.shape
    tm = OUTPROJ_TM
    half = ret.shape[1]
    assert t % tm == 0 and hgo.shape[1] == half and w_out.shape == (2 * half, d)
    row = lambda i: (i, 0)
    return pl.pallas_call(
        _outproj_kernel,
        grid=(t // tm,),
        in_specs=[pl.BlockSpec((tm, d), row),
                  pl.BlockSpec((tm, half), row), pl.BlockSpec((tm, half), row),
                  pl.BlockSpec((half, d), lambda i: (0, 0)), pl.BlockSpec((half, d), lambda i: (1, 0))],
        out_specs=pl.BlockSpec((tm, d), row),
        out_shape=jax.ShapeDtypeStruct((t, d), F32),
        compiler_params=pltpu.CompilerParams(
            dimension_semantics=("parallel",), vmem_limit_bytes=VMEM_LIMIT_BYTES),
        name="outproj",
    )(x1, ret, hgo, w_out, w_out)


class _RetentionHead:
    def __init__(self, q, k, v, head, c):
        tb, dk = q.shape
        self.c = c
        self.chunks = [slice(n * c, (n + 1) * c) for n in range(tb // c)]
        self.q, self.k, self.v = q, k, v

        def log_gamma(shape):
            return jnp.log(1.0 - jnp.exp2(-5.0 - jnp.full(shape, head, jnp.int32).astype(F32)))

        row = lax.broadcasted_iota(jnp.int32, (c, dk), 0).astype(F32)
        rel = (lax.broadcasted_iota(jnp.int32, (c, c), 0) -
               lax.broadcasted_iota(jnp.int32, (c, c), 1)).astype(F32)
        self.decay = jnp.where(rel >= 0, jnp.exp(log_gamma((c, c)) * jnp.maximum(rel, 0.0)), 0.0)
        self.q_dec = jnp.exp(log_gamma((c, dk)) * (row + 1.0))
        self.k_dec = jnp.exp(log_gamma((c, dk)) * (c - 1.0 - row))
        self.g_chunk = jnp.exp(log_gamma((1, dk)) * float(c))

    def free_dots(self):
        self.scores = [_dot_nt(self.q[r], self.k[r]) for r in self.chunks]
        self.updates = [_dot_tn((self.k[r].astype(F32) * self.k_dec).astype(BF16), self.v[r])
                        for r in self.chunks]

    def outputs(self, state):
        states = []
        for upd in self.updates:
            states.append(state.astype(BF16))
            state = self.g_chunk * state + upd
        outs = []
        for r, s, st in zip(self.chunks, self.scores, states):
            inner = _dot((s * self.decay).astype(BF16), self.v[r])
            outs.append(inner + _dot(self.q[r], st) * self.q_dec)
        return outs, state


class _HgrnConsts:
    def __init__(self, c, sub):
        d = HGRN_HEAD_DIM
        pair = BF16_TILE_ROWS
        self.c, self.sub, self.pair = c, sub, pair
        self.tri = (lax.broadcasted_iota(jnp.int32, (c, c), 0) >=
                    lax.broadcasted_iota(jnp.int32, (c, c), 1)).astype(BF16)
        self.pair_lane = lax.broadcasted_iota(jnp.int32, (pair, d), 1) % sub
        a_row = lax.broadcasted_iota(jnp.int32, (c, c), 0)
        a_col = lax.broadcasted_iota(jnp.int32, (c, c), 1)
        self.diag_mask = ((a_row // sub) == (a_col // sub)) & (a_col <= a_row)
        self.halves = []
        h = c // 2
        while h >= sub:
            self.halves.append(h)
            h //= 2
        self.level_masks = [(a_row // (2 * h)) == (a_col // (2 * h)) for h in self.halves]


class _HgrnHead:
    def __init__(self, q, k, v, lf, cp_ref, consts):
        self.q, self.k, self.v, self.lf, self.cp_ref, self.cs = q, k, v, lf, cp_ref, consts
        c = consts.c
        self.chunks = [slice(n * c, (n + 1) * c) for n in range(q.shape[0] // c)]

    def cumsum_dots(self):
        d = HGRN_HEAD_DIM
        self.cums = []
        for rows in self.chunks:
            lf = self.lf[rows]
            lf1 = lf.astype(BF16)
            r1 = lf - lf1.astype(F32)
            lf2 = r1.astype(BF16)
            lf3 = (r1 - lf2.astype(F32)).astype(BF16)
            cum3 = _dot(self.cs.tri, jnp.concatenate([lf1, lf2, lf3], axis=1))
            self.cums.append(cum3[:, :d] + cum3[:, d:2 * d] + cum3[:, 2 * d:])

    def elementwise(self):
        cs, cp_ref = self.cs, self.cp_ref
        c, sub, pair, d = cs.c, cs.sub, cs.pair, HGRN_HEAD_DIM
        self.level_ops, self.a_diags, self.q_ins, self.k_ends, self.decays = [], [], [], [], []
        for rows, cum in zip(self.chunks, self.cums):
            r0 = rows.start
            q = self.q[rows].astype(F32)
            k = self.k[rows].astype(F32)
            cp = cum - jnp.log2(k)
            cp_ref[rows, :] = cp
            total = cum[c - 1:c, :]
            self.q_ins.append((q * jnp.exp2(cum)).astype(BF16))
            self.k_ends.append(jnp.exp2(total - cp).astype(BF16))
            self.decays.append(jnp.exp2(total))

            ops = []
            for h in cs.halves:
                q_parts, k_parts = [], []
                for p0 in range(0, c, 2 * h):
                    lo, up = slice(p0, p0 + h), slice(p0 + h, p0 + 2 * h)
                    c_b = cum[p0 + h - 1:p0 + h, :]
                    q_parts += [jnp.zeros((h, d), F32), q[up] * jnp.exp2(cum[up] - c_b)]
                    k_parts += [jnp.exp2(c_b - cp[lo]), jnp.zeros((h, d), F32)]
                ops.append((jnp.concatenate(q_parts, axis=0).astype(BF16),
                            jnp.concatenate(k_parts, axis=0).astype(BF16)))
            self.level_ops.append(ops)

            pairs = []
            for i0 in range(0, c, pair):
                q_i = q[i0:i0 + pair]
                c_i = cum[i0:i0 + pair]
                a_pair = jnp.zeros((pair, d), F32)
                for j in range(sub):
                    srcs = [cp_ref[r0 + i0 + s + j:r0 + i0 + s + j + 1, :]
                            for s in range(0, pair, sub)]
                    if len(srcs) == 1:
                        cp_j = srcs[0]
                    else:
                        cp_j = jnp.concatenate([jnp.broadcast_to(s, (sub, d)) for s in srcs], axis=0)
                    s_j = jnp.sum(q_i * jnp.exp2(c_i - cp_j), axis=-1, keepdims=True)
                    a_pair = jnp.where(cs.pair_lane == j, s_j, a_pair)
                pairs.append(a_pair)
            self.a_diags.append(jnp.concatenate(pairs, axis=0)[:, :c])

    def free_dots(self):
        self.lows = [[_dot_nt(q_l, k_l) for q_l, k_l in ops] for ops in self.level_ops]
        self.updates = [_dot_tn(self.v[rows], k_end) for rows, k_end in zip(self.chunks, self.k_ends)]

    def outputs(self, state):
        cs = self.cs
        states = []
        for decay, upd in zip(self.decays, self.updates):
            states.append(jnp.transpose(state).astype(BF16))
            state = decay * state + upd
        outs = []
        for n, rows in enumerate(self.chunks):
            attn = jnp.where(cs.diag_mask, self.a_diags[n], 0.0)
            for h, mask, low in zip(cs.halves, cs.level_masks, self.lows[n]):
                attn = attn + (low if 2 * h == cs.c else jnp.where(mask, low, 0.0))
            lhs = jnp.concatenate([self.q_ins[n], attn.astype(BF16)], axis=1)
            rhs = jnp.concatenate([states[n], self.v[rows]], axis=0)
            outs.append(_dot(lhs, rhs))
        return outs, state


_RQ, _RK, _RV, _RG, _HQ, _HK, _HV, _HG = range(8)


def _mixer_kernel(x_ref, g_ref, cos_sin_ref, lbl_ref, rgain_ref, hgain_ref,
                  w_ref, *refs, n_col_blocks, blocks_per_seq, n_side):
    side_in = refs[:n_side]
    ret_ref, hgo_ref = refs[n_side:n_side + 2]
    side_out = refs[n_side + 2:2 * n_side + 2]
    h_ref, act_ref, lf_ref, ret_state_ref, hgrn_state_ref, cp_ref = refs[2 * n_side + 2:]

    s = pl.program_id(0)
    nj = n_col_blocks
    dk, dh = RET_HEAD_DIM, HGRN_HEAD_DIM
    heads_per_step = dk // dh
    cons = jnp.maximum(s - 1, 0)
    cj = cons % nj

    @pl.when(s == 0)
    def _():
        act_ref[...] = jnp.zeros_like(act_ref)
        lf_ref[...] = jnp.zeros_like(lf_ref)

    @pl.when(s % nj == 0)
    def _():
        h_ref[...] = _rmsnorm(x_ref[...], g_ref[...]).astype(BF16)

    @pl.when((cons // nj) % blocks_per_seq == 0)
    def _():
        ret_state_ref[cj] = jnp.zeros((dk, dk), F32)
        for a in range(heads_per_step):
            hgrn_state_ref[heads_per_step * cj + a] = jnp.zeros((dh, dh), F32)

    for i_ref, o_ref in zip(side_in, side_out):
        o_ref[...] = i_ref[...].astype(BF16)

    h = h_ref[...]
    half = dk // 2
    cos = cos_sin_ref[:, :half]
    sin = cos_sin_ref[:, half:]
    dst = act_ref.at[s % 2]
    src = act_ref.at[(s + 1) % 2]
    src_lf = lf_ref.at[(s + 1) % 2]

    def rope(p):
        x1, x2 = p[:, :half], p[:, half:]
        return jnp.concatenate([x1 * cos - x2 * sin, x2 * cos + x1 * sin], axis=-1)

    consts = _HgrnConsts(HGRN_CHUNK, HGRN_SUB)
    hq, hk, hv, hlf = src[_HQ], src[_HK], src[_HV], src_lf[...]
    heads = []
    for a in range(heads_per_step):
        cols = slice(a * dh, (a + 1) * dh)
        heads.append(_HgrnHead(hq[:, cols], hk[:, cols], hv[:, cols], hlf[:, cols],
                               cp_ref.at[a], consts))
    ret = _RetentionHead(src[_RQ], src[_RK], src[_RV], cj, RET_CHUNK)

    def project_pair(first_group):
        return _dot(h, w_ref[first_group]), _dot(h, w_ref[first_group + 1])

    def project_hq_hf():
        p_hq, z = project_pair(_HQ)
        dst[_HQ] = _silu(p_hq).astype(BF16)
        lbl = lbl_ref[...]
        m = jnp.maximum(lbl, 0.0)
        e_l = jnp.exp(lbl - m)
        lb = e_l / (e_l + jnp.exp(-m))
        dst[_HK] = ((1.0 - lb) * jax.nn.sigmoid(-z)).astype(BF16)
        lf_ref[s % 2] = jnp.log2(lb + (1.0 - lb) * jax.nn.sigmoid(z))

    def project_hv_hg():
        p_hv, p_hg = project_pair(_HV)
        dst[_HV] = p_hv.astype(BF16)
        dst[_HG] = _silu(p_hg).astype(BF16)

    def project_rq_rk():
        p_rq, p_rk = project_pair(_RQ)
        dst[_RQ] = (rope(p_rq) * (dk ** -0.5)).astype(BF16)
        dst[_RK] = rope(p_rk).astype(BF16)

    def project_rv_rg():
        p_rv, p_rg = project_pair(_RV)
        dst[_RV] = p_rv.astype(BF16)
        dst[_RG] = _silu(p_rg).astype(BF16)

    def hgrn_outputs():
        hgate = src[_HG]
        for a, hd in enumerate(heads):
            cols = slice(a * dh, (a + 1) * dh)
            outs, state = hd.outputs(hgrn_state_ref[heads_per_step * cj + a])
            hgrn_state_ref[heads_per_step * cj + a] = state
            gain = hgain_ref[:, cols]
            for rows, o in zip(hd.chunks, outs):
                y = o * lax.rsqrt(jnp.mean(o * o, axis=-1, keepdims=True) + EPS)
                hgo_ref[rows, cols] = (y * gain * hgate[rows, cols].astype(F32)).astype(BF16)

    def retention_outputs():
        rgate = src[_RG]
        outs, state = ret.outputs(ret_state_ref[cj])
        ret_state_ref[cj] = state
        gain = rgain_ref[...]
        for rows, o in zip(ret.chunks, outs):
            mu = jnp.mean(o, axis=-1, keepdims=True)
            oc = o - mu
            var = jnp.mean(oc * oc, axis=-1, keepdims=True)
            ret_ref[rows, :] = (oc * lax.rsqrt(var + EPS) * gain
                                * rgate[rows, :].astype(F32)).astype(BF16)

    for hd in heads:
        hd.cumsum_dots()
    project_hq_hf()
    for hd in heads:
        hd.elementwise()
    for hd in heads:
        hd.free_dots()
    ret.free_dots()
    project_hv_hg()
    hgrn_outputs()
    project_rq_rk()
    retention_outputs()
    project_rv_rg()


def _mixer(x1, norm_g, w_units, lb_logits, ret_gain, hgrn_gain, cos_sin, seq, side_f32):
    t, d = x1.shape
    tm = MIXER_TM
    tn = RET_HEAD_DIM
    dh = HGRN_HEAD_DIM
    width = RET_HEADS * RET_HEAD_DIM
    assert w_units.shape == (width // tn, 8, d, tn) and width == HGRN_HEADS * dh and tn % dh == 0
    assert t % tm == 0 and seq % tm == 0
    assert tm % RET_CHUNK == 0 and tm % HGRN_CHUNK == 0
    assert HGRN_CHUNK % BF16_TILE_ROWS == 0 and BF16_TILE_ROWS % HGRN_SUB == 0
    nj = width // tn
    blocks_per_seq = seq // tm
    n_units = (t // tm) * nj
    prod = lambda s: jnp.minimum(s, n_units - 1)
    cons = lambda s: jnp.maximum(s - 1, 0)
    const = lambda s: (0, 0)
    pos = lambda s: ((prod(s) // nj) % blocks_per_seq, 0)
    cons_col = lambda s: (0, cons(s) % nj)

    side_specs = []
    for w in side_f32:
        rows = next(r for r in range(BF16_TILE_ROWS, w.shape[0] + 1, BF16_TILE_ROWS)
                    if w.shape[0] % r == 0 and w.shape[0] // r <= n_units)
        side_specs.append(pl.BlockSpec(
            (rows, w.shape[1]), lambda s, last=w.shape[0] // rows - 1: (jnp.minimum(s, last), 0)))

    out_spec = pl.BlockSpec((tm, tn), lambda s: (cons(s) // nj, cons(s) % nj))
    out_bf = jax.ShapeDtypeStruct((t, width), BF16)
    outs = pl.pallas_call(
        functools.partial(_mixer_kernel, n_col_blocks=nj, blocks_per_seq=blocks_per_seq,
                          n_side=len(side_f32)),
        grid=(n_units + 1,),
        in_specs=[pl.BlockSpec((tm, d), lambda s: (prod(s) // nj, 0)), pl.BlockSpec((1, d), const),
                  pl.BlockSpec((tm, tn), pos),
                  pl.BlockSpec((1, tn), lambda s: (0, prod(s) % nj)),
                  pl.BlockSpec((1, tn), cons_col), pl.BlockSpec((1, tn), cons_col),
                  pl.BlockSpec((None, 8, d, tn), lambda s: (prod(s) % nj, 0, 0, 0))] + side_specs,
        out_specs=[out_spec, out_spec] + side_specs,
        out_shape=[out_bf, out_bf] + [jax.ShapeDtypeStruct(w.shape, BF16) for w in side_f32],
        scratch_shapes=[pltpu.VMEM((tm, d), BF16),
                        pltpu.VMEM((2, 8, tm, tn), BF16),
                        pltpu.VMEM((2, tm, tn), F32),
                        pltpu.VMEM((RET_HEADS, RET_HEAD_DIM, RET_HEAD_DIM), F32),
                        pltpu.VMEM((HGRN_HEADS, dh, dh), F32),
                        pltpu.VMEM((tn // dh, tm, dh), F32)],
        compiler_params=pltpu.CompilerParams(
            dimension_semantics=("arbitrary",), vmem_limit_bytes=VMEM_LIMIT_BYTES),
        name="mixer",
    )(x1, norm_g.reshape(1, d), cos_sin, lb_logits, ret_gain.reshape(1, width),
      hgrn_gain.reshape(1, width), w_units, *side_f32)
    return outs[0], outs[1], outs[2:]


@jax.jit
def kernel(x, ffn1_norm, ffn1_w_gate, ffn1_w_up, ffn1_w_down, mix_norm, w_in, ret_norm_g, hgrn_lb_logits, hgrn_norm_g, w_out, ffn2_norm, ffn2_w_gate, ffn2_w_up, ffn2_w_down, final_norm):
    batch, seq, d = x.shape
    assert ffn1_norm.shape[0] == 1, "single-layer stack"
    t = batch * seq
    bf = lambda w: w[0].astype(BF16)

    inv = np.power(ROPE_BASE, -np.arange(0, RET_HEAD_DIM, 2, dtype=np.float64) / RET_HEAD_DIM)
    ang = np.arange(seq, dtype=np.float64)[:, None] * inv[None, :]
    cos_sin = jnp.asarray(np.concatenate([np.cos(ang), np.sin(ang)], axis=1), dtype=F32)

    x0 = x.reshape(t, d)
    x1, w_units = _ffn(x0, ffn1_norm[0], bf(ffn1_w_gate), bf(ffn1_w_up), bf(ffn1_w_down),
                       name="ffn1", w_in_f32=w_in[0].astype(F32))
    side = [ffn2_w_gate[0].astype(F32), ffn2_w_up[0].astype(F32), ffn2_w_down[0].astype(F32),
            w_out[0].astype(F32)]
    ret, hgo, (wg2, wu2, wd2, wo) = _mixer(
        x1, mix_norm[0], w_units, hgrn_lb_logits.astype(F32),
        ret_norm_g[0].astype(F32), hgrn_norm_g[0].astype(F32), cos_sin, seq, side)
    x2 = _outproj(x1, ret, hgo, wo)
    out = _ffn(x2, ffn2_norm[0], wg2, wu2, wd2, name="ffn2", final_g=final_norm)
    return out.reshape(batch, seq, d)
```

```python
import functools

import jax
import jax.numpy as jnp
import numpy as np
from jax import lax
from jax.experimental import pallas as pl
from jax.experimental.pallas import tpu as pltpu

F32 = jnp.float32
BF16 = jnp.bfloat16

RET_HEADS = 4
RET_HEAD_DIM = 256
HGRN_HEADS = 8
HGRN_HEAD_DIM = 128
ROPE_BASE = 10000.0
EPS = 1e-6
FFN_RESIDUAL_WEIGHT = 0.5

VMEM_LIMIT_BYTES = 56 * 1024 * 1024
BF16_TILE_ROWS = 16

FFN_TM = 1024
FFN_TF = 512
FFN_EDGE_SPLIT = 2
MIXER_TM = 512
OUTPROJ_TM = 512
RET_CHUNK = 256
HGRN_CHUNK = 64
HGRN_SUB = 8


def _dot(a, b):
    return jnp.dot(a, b, preferred_element_type=F32)


def _dot_nt(a, b):
    return lax.dot_general(a, b, (((1,), (1,)), ((), ())), preferred_element_type=F32)


def _dot_tn(a, b):
    return lax.dot_general(a, b, (((0,), (0,)), ((), ())), preferred_element_type=F32)


def _rmsnorm(x, g):
    return x * lax.rsqrt(jnp.mean(x * x, axis=-1, keepdims=True) + EPS) * g


def _silu(x):
    return x * jax.nn.sigmoid(x)


def _ffn_kernel(*refs, final_norm, cast_side, n_ff_steps):
    refs = list(refs)
    x_ref, g_ref, wg_ref, wu_ref, wd_ref = refs[:5]
    rest = refs[5:]
    fg_ref = rest.pop(0) if final_norm else None
    side_in_ref = rest.pop(0) if cast_side else None
    out_ref = rest.pop(0)
    side_out_ref = rest.pop(0) if cast_side else None
    (h_ref,) = rest

    j = pl.program_id(1)
    tm = x_ref.shape[0]
    part = tm // FFN_EDGE_SPLIT
    parts = [slice(p * part, (p + 1) * part) for p in range(FFN_EDGE_SPLIT)]

    def ride_along_cast():
        if cast_side:
            tn = side_out_ref.shape[2]
            for jj in range(side_out_ref.shape[0]):
                side_out_ref[jj] = side_in_ref[:, jj * tn:(jj + 1) * tn].astype(BF16)

    def ffn_slice(h):
        gate = _dot(h, wg_ref[...])
        up = _dot(h, wu_ref[...])
        act = (_silu(gate) * up * FFN_RESIDUAL_WEIGHT).astype(BF16)
        return _dot(act, wd_ref[...])

    @pl.when(j == 0)
    def _():
        ride_along_cast()
        for rows in parts:
            x = x_ref[rows, :]
            h = _rmsnorm(x, g_ref[...]).astype(BF16)
            h_ref[rows, :] = h
            out_ref[rows, :] = x + ffn_slice(h)

    last = n_ff_steps - 1 if final_norm else n_ff_steps

    @pl.when((j > 0) & (j < last))
    def _():
        ride_along_cast()
        out_ref[...] += ffn_slice(h_ref[...])

    if final_norm:
        @pl.when(j == last)
        def _():
            ride_along_cast()
            for rows in parts:
                y = out_ref[rows, :] + ffn_slice(h_ref[rows, :])
                out_ref[rows, :] = _rmsnorm(y, fg_ref[...])


def _ffn(x, norm_g, w_gate, w_up, w_down, *, name, final_g=None, w_in_f32=None):
    t, d = x.shape
    d_ff = w_gate.shape[1]
    tm, tf = FFN_TM, FFN_TF
    assert t % tm == 0 and d_ff % tf == 0
    n_blocks = t // tm
    n_ff_steps = d_ff // tf
    row = lambda i, j: (i, 0)
    const = lambda i, j: (0, 0)
    in_specs = [pl.BlockSpec((tm, d), row),
                pl.BlockSpec((1, d), const),
                pl.BlockSpec((d, tf), lambda i, j: (0, j)),
                pl.BlockSpec((d, tf), lambda i, j: (0, j)),
                pl.BlockSpec((tf, d), lambda i, j: (j, 0))]
    args = [x, norm_g.reshape(1, d), w_gate, w_up, w_down]
    out_specs = [pl.BlockSpec((tm, d), row)]
    out_shape = [jax.ShapeDtypeStruct((t, d), F32)]
    if final_g is not None:
        in_specs.append(pl.BlockSpec((1, d), const))
        args.append(final_g.reshape(1, d))
    if w_in_f32 is not None:
        n_groups = 8
        tn = RET_HEAD_DIM
        width = w_in_f32.shape[1] // n_groups
        nj = width // tn
        rows = d // n_blocks
        assert d % n_blocks == 0 and rows % BF16_TILE_ROWS == 0 and n_ff_steps >= n_groups
        group = lambda i, j: (i, jnp.minimum(j, n_groups - 1))
        in_specs.append(pl.BlockSpec((rows, width), group))
        args.append(w_in_f32)
        out_specs.append(pl.BlockSpec((nj, None, rows, tn),
                                      lambda i, j: (0, jnp.minimum(j, n_groups - 1), i, 0)))
        out_shape.append(jax.ShapeDtypeStruct((nj, n_groups, d, tn), BF16))
    kern = functools.partial(_ffn_kernel, final_norm=final_g is not None,
                             cast_side=w_in_f32 is not None, n_ff_steps=n_ff_steps)
    outs = pl.pallas_call(
        kern,
        grid=(n_blocks, n_ff_steps),
        in_specs=in_specs,
        out_specs=out_specs,
        out_shape=out_shape,
        scratch_shapes=[pltpu.VMEM((tm, d), BF16)],
        compiler_params=pltpu.CompilerParams(
            dimension_semantics=("parallel", "arbitrary"), vmem_limit_bytes=VMEM_LIMIT_BYTES),
        name=name,
    )(*args)
    return outs if w_in_f32 is not None else outs[0]


def _outproj_kernel(x_ref, ret_ref, hgo_ref, wo_r_ref, wo_h_ref, out_ref):
    out_ref[...] = (x_ref[...] + _dot(ret_ref[...], wo_r_ref[...])
                    + _dot(hgo_ref[...], wo_h_ref[...]))


def _outproj(x1, ret, hgo, w_out):
    t, d = x1.shape
    tm = OUTPROJ_TM
    half = ret.shape[1]
    assert t % tm == 0 and hgo.shape[1] == half and w_out.shape == (2 * half, d)
    row = lambda i: (i, 0)
    return pl.pallas_call(
        _outproj_kernel,
        grid=(t // tm,),
        in_specs=[pl.BlockSpec((tm, d), row),
                  pl.BlockSpec((tm, half), row), pl.BlockSpec((tm, half), row),
                  pl.BlockSpec((half, d), lambda i: (0, 0)), pl.BlockSpec((half, d), lambda i: (1, 0))],
        out_specs=pl.BlockSpec((tm, d), row),
        out_shape=jax.ShapeDtypeStruct((t, d), F32),
        compiler_params=pltpu.CompilerParams(
            dimension_semantics=("parallel",), vmem_limit_bytes=VMEM_LIMIT_BYTES),
        name="outproj",
    )(x1, ret, hgo, w_out, w_out)


class _RetentionHead:
    def __init__(self, q, k, v, head, c):
        tb, dk = q.shape
        self.c = c
        self.chunks = [slice(n * c, (n + 1) * c) for n in range(tb // c)]
        self.q, self.k, self.v = q, k, v

        def log_gamma(shape):
            return jnp.log(1.0 - jnp.exp2(-5.0 - jnp.full(shape, head, jnp.int32).astype(F32)))

        row = lax.broadcasted_iota(jnp.int32, (c, dk), 0).astype(F32)
        rel = (lax.broadcasted_iota(jnp.int32, (c, c), 0) -
               lax.broadcasted_iota(jnp.int32, (c, c), 1)).astype(F32)
        self.decay = jnp.where(rel >= 0, jnp.exp(log_gamma((c, c)) * jnp.maximum(rel, 0.0)), 0.0)
        self.q_dec = jnp.exp(log_gamma((c, dk)) * (row + 1.0))
        self.k_dec = jnp.exp(log_gamma((c, dk)) * (c - 1.0 - row))
        self.g_chunk = jnp.exp(log_gamma((1, dk)) * float(c))

    def free_dots(self):
        self.scores = [_dot_nt(self.q[r], self.k[r]) for r in self.chunks]
        self.updates = [_dot_tn((self.k[r].astype(F32) * self.k_dec).astype(BF16), self.v[r])
                        for r in self.chunks]

    def outputs(self, state):
        states = []
        for upd in self.updates:
            states.append(state.astype(BF16))
            state = self.g_chunk * state + upd
        outs = []
        for r, s, st in zip(self.chunks, self.scores, states):
            inner = _dot((s * self.decay).astype(BF16), self.v[r])
            outs.append(inner + _dot(self.q[r], st) * self.q_dec)
        return outs, state


class _HgrnConsts:
    def __init__(self, c, sub):
        d = HGRN_HEAD_DIM
        pair = BF16_TILE_ROWS
        self.c, self.sub, self.pair = c, sub, pair
        self.tri = (lax.broadcasted_iota(jnp.int32, (c, c), 0) >=
                    lax.broadcasted_iota(jnp.int32, (c, c), 1)).astype(BF16)
        self.pair_lane = lax.broadcasted_iota(jnp.int32, (pair, d), 1) % sub
        a_row = lax.broadcasted_iota(jnp.int32, (c, c), 0)
        a_col = lax.broadcasted_iota(jnp.int32, (c, c), 1)
        self.diag_mask = ((a_row // sub) == (a_col // sub)) & (a_col <= a_row)
        self.halves = []
        h = c // 2
        while h >= sub:
            self.halves.append(h)
            h //= 2
        self.level_masks = [(a_row // (2 * h)) == (a_col // (2 * h)) for h in self.halves]


class _HgrnHead:
    def __init__(self, q, k, v, lf, cp_ref, consts):
        self.q, self.k, self.v, self.lf, self.cp_ref, self.cs = q, k, v, lf, cp_ref, consts
        c = consts.c
        self.chunks = [slice(n * c, (n + 1) * c) for n in range(q.shape[0] // c)]

    def cumsum_dots(self):
        d = HGRN_HEAD_DIM
        self.cums = []
        for rows in self.chunks:
            lf = self.lf[rows]
            lf1 = lf.astype(BF16)
            r1 = lf - lf1.astype(F32)
            lf2 = r1.astype(BF16)
            lf3 = (r1 - lf2.astype(F32)).astype(BF16)
            cum3 = _dot(self.cs.tri, jnp.concatenate([lf1, lf2, lf3], axis=1))
            self.cums.append(cum3[:, :d] + cum3[:, d:2 * d] + cum3[:, 2 * d:])

    def elementwise(self):
        cs, cp_ref = self.cs, self.cp_ref
        c, sub, pair, d = cs.c, cs.sub, cs.pair, HGRN_HEAD_DIM
        self.level_ops, self.a_diags, self.q_ins, self.k_ends, self.decays = [], [], [], [], []
        for rows, cum in zip(self.chunks, self.cums):
            r0 = rows.start
            q = self.q[rows].astype(F32)
            k = self.k[rows].astype(F32)
            cp = cum - jnp.log2(k)
            cp_ref[rows, :] = cp
            total = cum[c - 1:c, :]
            self.q_ins.append((q * jnp.exp2(cum)).astype(BF16))
            self.k_ends.append(jnp.exp2(total - cp).astype(BF16))
            self.decays.append(jnp.exp2(total))

            ops = []
            for h in cs.halves:
                q_parts, k_parts = [], []
                for p0 in range(0, c, 2 * h):
                    lo, up = slice(p0, p0 + h), slice(p0 + h, p0 + 2 * h)
                    c_b = cum[p0 + h - 1:p0 + h, :]
                    q_parts.append(q[up] * jnp.exp2(cum[up] - c_b))
                    k_parts += [jnp.exp2(c_b - cp[lo]), jnp.zeros((h, d), F32)]
                ops.append((jnp.concatenate(q_parts, axis=0).astype(BF16),
                            jnp.concatenate(k_parts, axis=0).astype(BF16)))
            self.level_ops.append(ops)

            pairs = []
            for i0 in range(0, c, pair):
                q_i = q[i0:i0 + pair]
                c_i = cum[i0:i0 + pair]
                a_pair = jnp.zeros((pair, d), F32)
                for j in range(sub):
                    srcs = [cp_ref[r0 + i0 + s + j:r0 + i0 + s + j + 1, :]
                            for s in range(0, pair, sub)]
                    if len(srcs) == 1:
                        cp_j = srcs[0]
                    else:
                        cp_j = jnp.concatenate([jnp.broadcast_to(s, (sub, d)) for s in srcs], axis=0)
                    s_j = jnp.sum(q_i * jnp.exp2(c_i - cp_j), axis=-1, keepdims=True)
                    a_pair = jnp.where(cs.pair_lane == j, s_j, a_pair)
                pairs.append(a_pair)
            self.a_diags.append(jnp.concatenate(pairs, axis=0)[:, :c])

    def free_dots(self):
        self.lows = [[_dot_nt(q_l, k_l) for q_l, k_l in ops] for ops in self.level_ops]
        self.updates = [_dot_tn(self.v[rows], k_end) for rows, k_end in zip(self.chunks, self.k_ends)]

    def outputs(self, state):
        cs = self.cs
        states = []
        for decay, upd in zip(self.decays, self.updates):
            states.append(jnp.transpose(state).astype(BF16))
            state = decay * state + upd
        outs = []
        for n, rows in enumerate(self.chunks):
            attn = jnp.where(cs.diag_mask, self.a_diags[n], 0.0)
            for h, mask, low in zip(cs.halves, cs.level_masks, self.lows[n]):
                parts = []
                for g in range(cs.c // (2 * h)):
                    parts += [jnp.zeros((h, cs.c), F32), low[g * h:(g + 1) * h]]
                low = jnp.concatenate(parts, axis=0)
                attn = attn + (low if 2 * h == cs.c else jnp.where(mask, low, 0.0))
            lhs = jnp.concatenate([self.q_ins[n], attn.astype(BF16)], axis=1)
            rhs = jnp.concatenate([states[n], self.v[rows]], axis=0)
            outs.append(_dot(lhs, rhs))
        return outs, state


_RQ, _RK, _RV, _RG, _HQ, _HK, _HV, _HG = range(8)


def _mixer_kernel(x_ref, g_ref, cos_sin_ref, lbl_ref, rgain_ref, hgain_ref,
                  w_ref, *refs, n_col_blocks, blocks_per_seq, n_side):
    side_in = refs[:n_side]
    ret_ref, hgo_ref = refs[n_side:n_side + 2]
    side_out = refs[n_side + 2:2 * n_side + 2]
    h_ref, act_ref, lf_ref, ret_state_ref, hgrn_state_ref, cp_ref = refs[2 * n_side + 2:]

    s = pl.program_id(0)
    nj = n_col_blocks
    dk, dh = RET_HEAD_DIM, HGRN_HEAD_DIM
    heads_per_step = dk // dh
    cons = jnp.maximum(s - 1, 0)
    cj = cons % nj

    @pl.when(s == 0)
    def _():
        act_ref[...] = jnp.zeros_like(act_ref)
        lf_ref[...] = jnp.zeros_like(lf_ref)

    @pl.when(s % nj == 0)
    def _():
        h_ref[...] = _rmsnorm(x_ref[...], g_ref[...]).astype(BF16)

    @pl.when((cons // nj) % blocks_per_seq == 0)
    def _():
        ret_state_ref[cj] = jnp.zeros((dk, dk), F32)
        for a in range(heads_per_step):
            hgrn_state_ref[heads_per_step * cj + a] = jnp.zeros((dh, dh), F32)

    for i_ref, o_ref in zip(side_in, side_out):
        o_ref[...] = i_ref[...].astype(BF16)

    h = h_ref[...]
    half = dk // 2
    cos = cos_sin_ref[:, :half]
    sin = cos_sin_ref[:, half:]
    dst = act_ref.at[s % 2]
    src = act_ref.at[(s + 1) % 2]
    src_lf = lf_ref.at[(s + 1) % 2]

    def rope(p):
        x1, x2 = p[:, :half], p[:, half:]
        return jnp.concatenate([x1 * cos - x2 * sin, x2 * cos + x1 * sin], axis=-1)

    consts = _HgrnConsts(HGRN_CHUNK, HGRN_SUB)
    hq, hk, hv, hlf = src[_HQ], src[_HK], src[_HV], src_lf[...]
    heads = []
    for a in range(heads_per_step):
        cols = slice(a * dh, (a + 1) * dh)
        heads.append(_HgrnHead(hq[:, cols], hk[:, cols], hv[:, cols], hlf[:, cols],
                               cp_ref.at[a], consts))
    ret = _RetentionHead(src[_RQ], src[_RK], src[_RV], cj, RET_CHUNK)

    def project_pair(first_group):
        return _dot(h, w_ref[first_group]), _dot(h, w_ref[first_group + 1])

    def project_hq_hf():
        p_hq, z = project_pair(_HQ)
        dst[_HQ] = _silu(p_hq).astype(BF16)
        lbl = lbl_ref[...]
        m = jnp.maximum(lbl, 0.0)
        e_l = jnp.exp(lbl - m)
        lb = e_l / (e_l + jnp.exp(-m))
        dst[_HK] = ((1.0 - lb) * jax.nn.sigmoid(-z)).astype(BF16)
        lf_ref[s % 2] = jnp.log2(lb + (1.0 - lb) * jax.nn.sigmoid(z))

    def project_hv_hg():
        p_hv, p_hg = project_pair(_HV)
        dst[_HV] = p_hv.astype(BF16)
        dst[_HG] = _silu(p_hg).astype(BF16)

    def project_rq_rk():
        p_rq, p_rk = project_pair(_RQ)
        dst[_RQ] = (rope(p_rq) * (dk ** -0.5)).astype(BF16)
        dst[_RK] = rope(p_rk).astype(BF16)

    def project_rv_rg():
        p_rv, p_rg = project_pair(_RV)
        dst[_RV] = p_rv.astype(BF16)
        dst[_RG] = _silu(p_rg).astype(BF16)

    def hgrn_outputs():
        hgate = src[_HG]
        for a, hd in enumerate(heads):
            cols = slice(a * dh, (a + 1) * dh)
            outs, state = hd.outputs(hgrn_state_ref[heads_per_step * cj + a])
            hgrn_state_ref[heads_per_step * cj + a] = state
            gain = hgain_ref[:, cols]
            for rows, o in zip(hd.chunks, outs):
                y = o * lax.rsqrt(jnp.mean(o * o, axis=-1, keepdims=True) + EPS)
                hgo_ref[rows, cols] = (y * gain * hgate[rows, cols].astype(F32)).astype(BF16)

    def retention_outputs():
        rgate = src[_RG]
        outs, state = ret.outputs(ret_state_ref[cj])
        ret_state_ref[cj] = state
        gain = rgain_ref[...]
        for rows, o in zip(ret.chunks, outs):
            mu = jnp.mean(o, axis=-1, keepdims=True)
            oc = o - mu
            var = jnp.mean(oc * oc, axis=-1, keepdims=True)
            ret_ref[rows, :] = (oc * lax.rsqrt(var + EPS) * gain
                                * rgate[rows, :].astype(F32)).astype(BF16)

    for hd in heads:
        hd.cumsum_dots()
    project_hq_hf()
    for hd in heads:
        hd.elementwise()
    for hd in heads:
        hd.free_dots()
    ret.free_dots()
    project_hv_hg()
    hgrn_outputs()
    project_rq_rk()
    retention_outputs()
    project_rv_rg()


def _mixer(x1, norm_g, w_units, lb_logits, ret_gain, hgrn_gain, cos_sin, seq, side_f32):
    t, d = x1.shape
    tm = MIXER_TM
    tn = RET_HEAD_DIM
    dh = HGRN_HEAD_DIM
    width = RET_HEADS * RET_HEAD_DIM
    assert w_units.shape == (width // tn, 8, d, tn) and width == HGRN_HEADS * dh and tn % dh == 0
    assert t % tm == 0 and seq % tm == 0
    assert tm % RET_CHUNK == 0 and tm % HGRN_CHUNK == 0
    assert HGRN_CHUNK % BF16_TILE_ROWS == 0 and BF16_TILE_ROWS % HGRN_SUB == 0
    nj = width // tn
    blocks_per_seq = seq // tm
    n_units = (t // tm) * nj
    prod = lambda s: jnp.minimum(s, n_units - 1)
    cons = lambda s: jnp.maximum(s - 1, 0)
    const = lambda s: (0, 0)
    pos = lambda s: ((prod(s) // nj) % blocks_per_seq, 0)
    cons_col = lambda s: (0, cons(s) % nj)

    side_specs = []
    for w in side_f32:
        rows = next(r for r in range(BF16_TILE_ROWS, w.shape[0] + 1, BF16_TILE_ROWS)
                    if w.shape[0] % r == 0 and w.shape[0] // r <= n_units)
        side_specs.append(pl.BlockSpec(
            (rows, w.shape[1]), lambda s, last=w.shape[0] // rows - 1: (jnp.minimum(s, last), 0)))

    out_spec = pl.BlockSpec((tm, tn), lambda s: (cons(s) // nj, cons(s) % nj))
    out_bf = jax.ShapeDtypeStruct((t, width), BF16)
    outs = pl.pallas_call(
        functools.partial(_mixer_kernel, n_col_blocks=nj, blocks_per_seq=blocks_per_seq,
                          n_side=len(side_f32)),
        grid=(n_units + 1,),
        in_specs=[pl.BlockSpec((tm, d), lambda s: (prod(s) // nj, 0)), pl.BlockSpec((1, d), const),
                  pl.BlockSpec((tm, tn), pos),
                  pl.BlockSpec((1, tn), lambda s: (0, prod(s) % nj)),
                  pl.BlockSpec((1, tn), cons_col), pl.BlockSpec((1, tn), cons_col),
                  pl.BlockSpec((None, 8, d, tn), lambda s: (prod(s) % nj, 0, 0, 0))] + side_specs,
        out_specs=[out_spec, out_spec] + side_specs,
        out_shape=[out_bf, out_bf] + [jax.ShapeDtypeStruct(w.shape, BF16) for w in side_f32],
        scratch_shapes=[pltpu.VMEM((tm, d), BF16),
                        pltpu.VMEM((2, 8, tm, tn), BF16),
                        pltpu.VMEM((2, tm, tn), F32),
                        pltpu.VMEM((RET_HEADS, RET_HEAD_DIM, RET_HEAD_DIM), F32),
                        pltpu.VMEM((HGRN_HEADS, dh, dh), F32),
                        pltpu.VMEM((tn // dh, tm, dh), F32)],
        compiler_params=pltpu.CompilerParams(
            dimension_semantics=("arbitrary",), vmem_limit_bytes=VMEM_LIMIT_BYTES),
        name="mixer",
    )(x1, norm_g.reshape(1, d), cos_sin, lb_logits, ret_gain.reshape(1, width),
      hgrn_gain.reshape(1, width), w_units, *side_f32)
    return outs[0], outs[1], outs[2:]


@jax.jit
def kernel(x, ffn1_norm, ffn1_w_gate, ffn1_w_up, ffn1_w_down, mix_norm, w_in, ret_norm_g, hgrn_lb_logits, hgrn_norm_g, w_out, ffn2_norm, ffn2_w_gate, ffn2_w_up, ffn2_w_down, final_norm):
    batch, seq, d = x.shape
    assert ffn1_norm.shape[0] == 1, "single-layer stack"
    t = batch * seq
    bf = lambda w: w[0].astype(BF16)

    inv = np.power(ROPE_BASE, -np.arange(0, RET_HEAD_DIM, 2, dtype=np.float64) / RET_HEAD_DIM)
    ang = np.arange(seq, dtype=np.float64)[:, None] * inv[None, :]
    cos_sin = jnp.asarray(np.concatenate([np.cos(ang), np.sin(ang)], axis=1), dtype=F32)

    x0 = x.reshape(t, d)
    x1, w_units = _ffn(x0, ffn1_norm[0], bf(ffn1_w_gate), bf(ffn1_w_up), bf(ffn1_w_down),
                       name="ffn1", w_in_f32=w_in[0].astype(F32))
    side = [ffn2_w_gate[0].astype(F32), ffn2_w_up[0].astype(F32), ffn2_w_down[0].astype(F32),
            w_out[0].astype(F32)]
    ret, hgo, (wg2, wu2, wd2, wo) = _mixer(
        x1, mix_norm[0], w_units, hgrn_lb_logits.astype(F32),
        ret_norm_g[0].astype(F32), hgrn_norm_g[0].astype(F32), cos_sin, seq, side)
    x2 = _outproj(x1, ret, hgo, wo)
    out = _ffn(x2, ffn2_norm[0], wg2, wu2, wd2, name="ffn2", final_g=final_norm)
    return out.reshape(batch, seq, d)
```

```python
import functools

import jax
import jax.numpy as jnp
import numpy as np
from jax import lax
from jax.experimental import pallas as pl
from jax.experimental.pallas import tpu as pltpu

F32 = jnp.float32
BF16 = jnp.bfloat16

RET_HEADS = 4
RET_HEAD_DIM = 256
HGRN_HEADS = 8
HGRN_HEAD_DIM = 128
ROPE_BASE = 10000.0
EPS = 1e-6
FFN_RESIDUAL_WEIGHT = 0.5

VMEM_LIMIT_BYTES = 56 * 1024 * 1024
BF16_TILE_ROWS = 16

FFN_TM = 1024
FFN_TF = 512
FFN_EDGE_SPLIT = 2
MIXER_TM = 512
MIXER_GROUP = 4
OUTPROJ_TM = 512
RET_CHUNK = 256
HGRN_CHUNK = 64
HGRN_SUB = 8


def _dot(a, b):
    return jnp.dot(a, b, preferred_element_type=F32)


def _dot_nt(a, b):
    return lax.dot_general(a, b, (((1,), (1,)), ((), ())), preferred_element_type=F32)


def _dot_tn(a, b):
    return lax.dot_general(a, b, (((0,), (0,)), ((), ())), preferred_element_type=F32)


def _rmsnorm(x, g):
    return x * lax.rsqrt(jnp.mean(x * x, axis=-1, keepdims=True) + EPS) * g


def _silu(x):
    return x * jax.nn.sigmoid(x)


def _ffn_kernel(*refs, final_norm, cast_side, n_ff_steps):
    refs = list(refs)
    x_ref, g_ref, wg_ref, wu_ref, wd_ref = refs[:5]
    rest = refs[5:]
    fg_ref = rest.pop(0) if final_norm else None
    side_in_ref = rest.pop(0) if cast_side else None
    out_ref = rest.pop(0)
    side_out_ref = rest.pop(0) if cast_side else None
    (h_ref,) = rest

    j = pl.program_id(1)
    tm = x_ref.shape[0]
    part = tm // FFN_EDGE_SPLIT
    parts = [slice(p * part, (p + 1) * part) for p in range(FFN_EDGE_SPLIT)]

    def ride_along_cast():
        if cast_side:
            tn = side_out_ref.shape[2]
            for jj in range(side_out_ref.shape[0]):
                side_out_ref[jj] = side_in_ref[:, jj * tn:(jj + 1) * tn].astype(BF16)

    def ffn_slice(h):
        gate = _dot(h, wg_ref[...])
        up = _dot(h, wu_ref[...])
        act = (_silu(gate) * up * FFN_RESIDUAL_WEIGHT).astype(BF16)
        return _dot(act, wd_ref[...])

    @pl.when(j == 0)
    def _():
        ride_along_cast()
        for rows in parts:
            x = x_ref[rows, :]
            h = _rmsnorm(x, g_ref[...]).astype(BF16)
            h_ref[rows, :] = h
            out_ref[rows, :] = x + ffn_slice(h)

    last = n_ff_steps - 1 if final_norm else n_ff_steps

    @pl.when((j > 0) & (j < last))
    def _():
        ride_along_cast()
        out_ref[...] += ffn_slice(h_ref[...])

    if final_norm:
        @pl.when(j == last)
        def _():
            ride_along_cast()
            for rows in parts:
                y = out_ref[rows, :] + ffn_slice(h_ref[rows, :])
                out_ref[rows, :] = _rmsnorm(y, fg_ref[...])


def _ffn(x, norm_g, w_gate, w_up, w_down, *, name, final_g=None, w_in_f32=None):
    t, d = x.shape
    d_ff = w_gate.shape[1]
    tm, tf = FFN_TM, FFN_TF
    assert t % tm == 0 and d_ff % tf == 0
    n_blocks = t // tm
    n_ff_steps = d_ff // tf
    row = lambda i, j: (i, 0)
    const = lambda i, j: (0, 0)
    in_specs = [pl.BlockSpec((tm, d), row),
                pl.BlockSpec((1, d), const),
                pl.BlockSpec((d, tf), lambda i, j: (0, j)),
                pl.BlockSpec((d, tf), lambda i, j: (0, j)),
                pl.BlockSpec((tf, d), lambda i, j: (j, 0))]
    args = [x, norm_g.reshape(1, d), w_gate, w_up, w_down]
    out_specs = [pl.BlockSpec((tm, d), row)]
    out_shape = [jax.ShapeDtypeStruct((t, d), F32)]
    if final_g is not None:
        in_specs.append(pl.BlockSpec((1, d), const))
        args.append(final_g.reshape(1, d))
    if w_in_f32 is not None:
        n_groups = 8
        tn = RET_HEAD_DIM
        width = w_in_f32.shape[1] // n_groups
        nj = width // tn
        rows = d // n_blocks
        assert d % n_blocks == 0 and rows % BF16_TILE_ROWS == 0 and n_ff_steps >= n_groups
        group = lambda i, j: (i, jnp.minimum(j, n_groups - 1))
        in_specs.append(pl.BlockSpec((rows, width), group))
        args.append(w_in_f32)
        out_specs.append(pl.BlockSpec((nj, None, rows, tn),
                                      lambda i, j: (0, jnp.minimum(j, n_groups - 1), i, 0)))
        out_shape.append(jax.ShapeDtypeStruct((nj, n_groups, d, tn), BF16))
    kern = functools.partial(_ffn_kernel, final_norm=final_g is not None,
                             cast_side=w_in_f32 is not None, n_ff_steps=n_ff_steps)
    outs = pl.pallas_call(
        kern,
        grid=(n_blocks, n_ff_steps),
        in_specs=in_specs,
        out_specs=out_specs,
        out_shape=out_shape,
        scratch_shapes=[pltpu.VMEM((tm, d), BF16)],
        compiler_params=pltpu.CompilerParams(
            dimension_semantics=("parallel", "arbitrary"), vmem_limit_bytes=VMEM_LIMIT_BYTES),
        name=name,
    )(*args)
    return outs if w_in_f32 is not None else outs[0]


def _outproj_kernel(x_ref, ret_ref, hgo_ref, wo_r_ref, wo_h_ref, out_ref):
    out_ref[...] = (x_ref[...] + _dot(ret_ref[...], wo_r_ref[...])
                    + _dot(hgo_ref[...], wo_h_ref[...]))


def _outproj(x1, ret, hgo, w_out):
    t, d = x1.shape
    tm = OUTPROJ_TM
    half = ret.shape[1]
    assert t % tm == 0 and hgo.shape[1] == half and w_out.shape == (2 * half, d)
    row = lambda i: (i, 0)
    return pl.pallas_call(
        _outproj_kernel,
        grid=(t // tm,),
        in_specs=[pl.BlockSpec((tm, d), row),
                  pl.BlockSpec((tm, half), row), pl.BlockSpec((tm, half), row),
                  pl.BlockSpec((half, d), lambda i: (0, 0)), pl.BlockSpec((half, d), lambda i: (1, 0))],
        out_specs=pl.BlockSpec((tm, d), row),
        out_shape=jax.ShapeDtypeStruct((t, d), F32),
        compiler_params=pltpu.CompilerParams(
            dimension_semantics=("parallel",), vmem_limit_bytes=VMEM_LIMIT_BYTES),
        name="outproj",
    )(x1, ret, hgo, w_out, w_out)


class _RetentionHead:
    def __init__(self, q, k, v, head, c):
        tb, dk = q.shape
        self.c = c
        self.chunks = [slice(n * c, (n + 1) * c) for n in range(tb // c)]
        self.q, self.k, self.v = q, k, v

        def log_gamma(shape):
            return jnp.log(1.0 - jnp.exp2(-5.0 - jnp.full(shape, head, jnp.int32).astype(F32)))

        row = lax.broadcasted_iota(jnp.int32, (c, dk), 0).astype(F32)
        rel = (lax.broadcasted_iota(jnp.int32, (c, c), 0) -
               lax.broadcasted_iota(jnp.int32, (c, c), 1)).astype(F32)
        self.decay = jnp.where(rel >= 0, jnp.exp(log_gamma((c, c)) * jnp.maximum(rel, 0.0)), 0.0)
        self.q_dec = jnp.exp(log_gamma((c, dk)) * (row + 1.0))
        self.k_dec = jnp.exp(log_gamma((c, dk)) * (c - 1.0 - row))
        self.g_chunk = jnp.exp(log_gamma((1, dk)) * float(c))

    def free_dots(self):
        self.scores = [_dot_nt(self.q[r], self.k[r]) for r in self.chunks]
        self.updates = [_dot_tn((self.k[r].astype(F32) * self.k_dec).astype(BF16), self.v[r])
                        for r in self.chunks]

    def outputs(self, state):
        states = []
        for upd in self.updates:
            states.append(state.astype(BF16))
            state = self.g_chunk * state + upd
        outs = []
        for r, s, st in zip(self.chunks, self.scores, states):
            inner = _dot((s * self.decay).astype(BF16), self.v[r])
            outs.append(inner + _dot(self.q[r], st) * self.q_dec)
        return outs, state


class _HgrnConsts:
    def __init__(self, c, sub):
        d = HGRN_HEAD_DIM
        pair = BF16_TILE_ROWS
        self.c, self.sub, self.pair = c, sub, pair
        self.tri = (lax.broadcasted_iota(jnp.int32, (c, c), 0) >=
                    lax.broadcasted_iota(jnp.int32, (c, c), 1)).astype(BF16)
        self.pair_lane = lax.broadcasted_iota(jnp.int32, (pair, d), 1) % sub
        a_row = lax.broadcasted_iota(jnp.int32, (c, c), 0)
        a_col = lax.broadcasted_iota(jnp.int32, (c, c), 1)
        self.diag_mask = ((a_row // sub) == (a_col // sub)) & (a_col <= a_row)
        self.halves = []
        h = c // 2
        while h >= sub:
            self.halves.append(h)
            h //= 2
        self.level_masks = [(a_row // (2 * h)) == (a_col // (2 * h)) for h in self.halves]


class _HgrnHead:
    def __init__(self, q, k, v, lf, cp_ref, consts):
        self.q, self.k, self.v, self.lf, self.cp_ref, self.cs = q, k, v, lf, cp_ref, consts
        c = consts.c
        self.chunks = [slice(n * c, (n + 1) * c) for n in range(q.shape[0] // c)]

    def cumsum_dots(self):
        d = HGRN_HEAD_DIM
        self.cums = []
        for rows in self.chunks:
            lf = self.lf[rows]
            lf1 = lf.astype(BF16)
            r1 = lf - lf1.astype(F32)
            lf2 = r1.astype(BF16)
            lf3 = (r1 - lf2.astype(F32)).astype(BF16)
            cum3 = _dot(self.cs.tri, jnp.concatenate([lf1, lf2, lf3], axis=1))
            self.cums.append(cum3[:, :d] + cum3[:, d:2 * d] + cum3[:, 2 * d:])

    def elementwise(self):
        cs, cp_ref = self.cs, self.cp_ref
        c, sub, pair, d = cs.c, cs.sub, cs.pair, HGRN_HEAD_DIM
        self.level_ops, self.a_diags, self.q_ins, self.k_ends, self.decays = [], [], [], [], []
        for rows, cum in zip(self.chunks, self.cums):
            r0 = rows.start
            q = self.q[rows].astype(F32)
            k = self.k[rows].astype(F32)
            cp = cum - jnp.log2(k)
            cp_ref[rows, :] = cp
            total = cum[c - 1:c, :]
            self.q_ins.append((q * jnp.exp2(cum)).astype(BF16))
            self.k_ends.append(jnp.exp2(total - cp).astype(BF16))
            self.decays.append(jnp.exp2(total))

            ops = []
            for h in cs.halves:
                q_parts, k_parts = [], []
                for p0 in range(0, c, 2 * h):
                    lo, up = slice(p0, p0 + h), slice(p0 + h, p0 + 2 * h)
                    c_b = cum[p0 + h - 1:p0 + h, :]
                    q_parts.append(q[up] * jnp.exp2(cum[up] - c_b))
                    k_parts += [jnp.exp2(c_b - cp[lo]), jnp.zeros((h, d), F32)]
                ops.append((jnp.concatenate(q_parts, axis=0).astype(BF16),
                            jnp.concatenate(k_parts, axis=0).astype(BF16)))
            self.level_ops.append(ops)

            pairs = []
            for i0 in range(0, c, pair):
                q_i = q[i0:i0 + pair]
                c_i = cum[i0:i0 + pair]
                a_pair = jnp.zeros((pair, d), F32)
                for j in range(sub):
                    srcs = [cp_ref[r0 + i0 + s + j:r0 + i0 + s + j + 1, :]
                            for s in range(0, pair, sub)]
                    if len(srcs) == 1:
                        cp_j = srcs[0]
                    else:
                        cp_j = jnp.concatenate([jnp.broadcast_to(s, (sub, d)) for s in srcs], axis=0)
                    s_j = jnp.sum(q_i * jnp.exp2(c_i - cp_j), axis=-1, keepdims=True)
                    a_pair = jnp.where(cs.pair_lane == j, s_j, a_pair)
                pairs.append(a_pair)
            self.a_diags.append(jnp.concatenate(pairs, axis=0)[:, :c])

    def free_dots(self):
        self.lows = [[_dot_nt(q_l, k_l) for q_l, k_l in ops] for ops in self.level_ops]
        self.updates = [_dot_tn(self.v[rows], k_end) for rows, k_end in zip(self.chunks, self.k_ends)]

    def outputs(self, state):
        cs = self.cs
        states = []
        for decay, upd in zip(self.decays, self.updates):
            states.append(jnp.transpose(state).astype(BF16))
            state = decay * state + upd
        outs = []
        for n, rows in enumerate(self.chunks):
            attn = jnp.where(cs.diag_mask, self.a_diags[n], 0.0)
            for h, mask, low in zip(cs.halves, cs.level_masks, self.lows[n]):
                parts = []
                for g in range(cs.c // (2 * h)):
                    parts += [jnp.zeros((h, cs.c), F32), low[g * h:(g + 1) * h]]
                low = jnp.concatenate(parts, axis=0)
                attn = attn + (low if 2 * h == cs.c else jnp.where(mask, low, 0.0))
            lhs = jnp.concatenate([self.q_ins[n], attn.astype(BF16)], axis=1)
            rhs = jnp.concatenate([states[n], self.v[rows]], axis=0)
            outs.append(_dot(lhs, rhs))
        return outs, state


_RQ, _RK, _RV, _RG, _HQ, _HK, _HV, _HG = range(8)


def _unit_blocks(u, nj):
    per_group = MIXER_GROUP * nj
    r = u % per_group
    return MIXER_GROUP * (u // per_group) + r % MIXER_GROUP, r // MIXER_GROUP


def _mixer_kernel(x_ref, g_ref, cos_sin_ref, lbl_ref, rgain_ref, hgain_ref,
                  w_ref, *refs, n_col_blocks, blocks_per_seq, n_side):
    side_in = refs[:n_side]
    ret_ref, hgo_ref = refs[n_side:n_side + 2]
    side_out = refs[n_side + 2:2 * n_side + 2]
    h_ref, act_ref, lf_ref, ret_state_ref, hgrn_state_ref, cp_ref = refs[2 * n_side + 2:]

    s = pl.program_id(0)
    nj = n_col_blocks
    dk, dh = RET_HEAD_DIM, HGRN_HEAD_DIM
    heads_per_step = dk // dh
    n_units = pl.num_programs(0) - 1
    pi, pj = _unit_blocks(jnp.minimum(s, n_units - 1), nj)
    ci, cj = _unit_blocks(jnp.maximum(s - 1, 0), nj)
    h_slot = pi % MIXER_GROUP

    @pl.when(s == 0)
    def _():
        act_ref[...] = jnp.zeros_like(act_ref)
        lf_ref[...] = jnp.zeros_like(lf_ref)

    @pl.when(pj == 0)
    def _():
        h_ref[h_slot] = _rmsnorm(x_ref[...], g_ref[...]).astype(BF16)

    @pl.when(ci % blocks_per_seq == 0)
    def _():
        ret_state_ref[cj] = jnp.zeros((dk, dk), F32)
        for a in range(heads_per_step):
            hgrn_state_ref[heads_per_step * cj + a] = jnp.zeros((dh, dh), F32)

    for i_ref, o_ref in zip(side_in, side_out):
        o_ref[...] = i_ref[...].astype(BF16)

    h = h_ref[h_slot]
    half = dk // 2
    cos = cos_sin_ref[:, :half]
    sin = cos_sin_ref[:, half:]
    dst = act_ref.at[s % 2]
    src = act_ref.at[(s + 1) % 2]
    src_lf = lf_ref.at[(s + 1) % 2]

    def rope(p):
        x1, x2 = p[:, :half], p[:, half:]
        return jnp.concatenate([x1 * cos - x2 * sin, x2 * cos + x1 * sin], axis=-1)

    consts = _HgrnConsts(HGRN_CHUNK, HGRN_SUB)
    hq, hk, hv, hlf = src[_HQ], src[_HK], src[_HV], src_lf[...]
    heads = []
    for a in range(heads_per_step):
        cols = slice(a * dh, (a + 1) * dh)
        heads.append(_HgrnHead(hq[:, cols], hk[:, cols], hv[:, cols], hlf[:, cols],
                               cp_ref.at[a], consts))
    ret = _RetentionHead(src[_RQ], src[_RK], src[_RV], cj, RET_CHUNK)

    def project_pair(first_group):
        return _dot(h, w_ref[first_group]), _dot(h, w_ref[first_group + 1])

    def project_hq_hf():
        p_hq, z = project_pair(_HQ)
        dst[_HQ] = _silu(p_hq).astype(BF16)
        lbl = lbl_ref[...]
        m = jnp.maximum(lbl, 0.0)
        e_l = jnp.exp(lbl - m)
        lb = e_l / (e_l + jnp.exp(-m))
        dst[_HK] = ((1.0 - lb) * jax.nn.sigmoid(-z)).astype(BF16)
        lf_ref[s % 2] = jnp.log2(lb + (1.0 - lb) * jax.nn.sigmoid(z))

    def project_hv_hg():
        p_hv, p_hg = project_pair(_HV)
        dst[_HV] = p_hv.astype(BF16)
        dst[_HG] = _silu(p_hg).astype(BF16)

    def project_rq_rk():
        p_rq, p_rk = project_pair(_RQ)
        dst[_RQ] = (rope(p_rq) * (dk ** -0.5)).astype(BF16)
        dst[_RK] = rope(p_rk).astype(BF16)

    def project_rv_rg():
        p_rv, p_rg = project_pair(_RV)
        dst[_RV] = p_rv.astype(BF16)
        dst[_RG] = _silu(p_rg).astype(BF16)

    def hgrn_outputs():
        hgate = src[_HG]
        for a, hd in enumerate(heads):
            cols = slice(a * dh, (a + 1) * dh)
            outs, state = hd.outputs(hgrn_state_ref[heads_per_step * cj + a])
            hgrn_state_ref[heads_per_step * cj + a] = state
            gain = hgain_ref[:, cols]
            for rows, o in zip(hd.chunks, outs):
                y = o * lax.rsqrt(jnp.mean(o * o, axis=-1, keepdims=True) + EPS)
                hgo_ref[rows, cols] = (y * gain * hgate[rows, cols].astype(F32)).astype(BF16)

    def retention_outputs():
        rgate = src[_RG]
        outs, state = ret.outputs(ret_state_ref[cj])
        ret_state_ref[cj] = state
        gain = rgain_ref[...]
        for rows, o in zip(ret.chunks, outs):
            mu = jnp.mean(o, axis=-1, keepdims=True)
            oc = o - mu
            var = jnp.mean(oc * oc, axis=-1, keepdims=True)
            ret_ref[rows, :] = (oc * lax.rsqrt(var + EPS) * gain
                                * rgate[rows, :].astype(F32)).astype(BF16)

    for hd in heads:
        hd.cumsum_dots()
    project_hq_hf()
    for hd in heads:
        hd.elementwise()
    for hd in heads:
        hd.free_dots()
    ret.free_dots()
    project_hv_hg()
    hgrn_outputs()
    project_rq_rk()
    retention_outputs()
    project_rv_rg()


def _mixer(x1, norm_g, w_units, lb_logits, ret_gain, hgrn_gain, cos_sin, seq, side_f32):
    t, d = x1.shape
    tm = MIXER_TM
    tn = RET_HEAD_DIM
    dh = HGRN_HEAD_DIM
    width = RET_HEADS * RET_HEAD_DIM
    assert w_units.shape == (width // tn, 8, d, tn) and width == HGRN_HEADS * dh and tn % dh == 0
    assert t % tm == 0 and seq % tm == 0
    assert tm % RET_CHUNK == 0 and tm % HGRN_CHUNK == 0
    assert HGRN_CHUNK % BF16_TILE_ROWS == 0 and BF16_TILE_ROWS % HGRN_SUB == 0
    nj = width // tn
    blocks_per_seq = seq // tm
    assert (t // tm) % MIXER_GROUP == 0
    n_units = (t // tm) * nj
    prod = lambda s: _unit_blocks(jnp.minimum(s, n_units - 1), nj)
    cons = lambda s: _unit_blocks(jnp.maximum(s - 1, 0), nj)
    const = lambda s: (0, 0)
    pos = lambda s: (prod(s)[0] % blocks_per_seq, 0)
    cons_col = lambda s: (0, cons(s)[1])

    def x_block(s):
        i, j = prod(s)
        return jnp.where(j == 0, i, i - i % MIXER_GROUP + MIXER_GROUP - 1), 0

    side_specs = []
    for w in side_f32:
        rows = next(r for r in range(BF16_TILE_ROWS, w.shape[0] + 1, BF16_TILE_ROWS)
                    if w.shape[0] % r == 0 and w.shape[0] // r <= n_units)
        side_specs.append(pl.BlockSpec(
            (rows, w.shape[1]), lambda s, last=w.shape[0] // rows - 1: (jnp.minimum(s, last), 0)))

    out_spec = pl.BlockSpec((tm, tn), cons)
    out_bf = jax.ShapeDtypeStruct((t, width), BF16)
    outs = pl.pallas_call(
        functools.partial(_mixer_kernel, n_col_blocks=nj, blocks_per_seq=blocks_per_seq,
                          n_side=len(side_f32)),
        grid=(n_units + 1,),
        in_specs=[pl.BlockSpec((tm, d), x_block), pl.BlockSpec((1, d), const),
                  pl.BlockSpec((tm, tn), pos),
                  pl.BlockSpec((1, tn), lambda s: (0, prod(s)[1])),
                  pl.BlockSpec((1, tn), cons_col), pl.BlockSpec((1, tn), cons_col),
                  pl.BlockSpec((None, 8, d, tn), lambda s: (prod(s)[1], 0, 0, 0))] + side_specs,
        out_specs=[out_spec, out_spec] + side_specs,
        out_shape=[out_bf, out_bf] + [jax.ShapeDtypeStruct(w.shape, BF16) for w in side_f32],
        scratch_shapes=[pltpu.VMEM((MIXER_GROUP, tm, d), BF16),
                        pltpu.VMEM((2, 8, tm, tn), BF16),
                        pltpu.VMEM((2, tm, tn), F32),
                        pltpu.VMEM((RET_HEADS, RET_HEAD_DIM, RET_HEAD_DIM), F32),
                        pltpu.VMEM((HGRN_HEADS, dh, dh), F32),
                        pltpu.VMEM((tn // dh, tm, dh), F32)],
        compiler_params=pltpu.CompilerParams(
            dimension_semantics=("arbitrary",), vmem_limit_bytes=VMEM_LIMIT_BYTES),
        name="mixer",
    )(x1, norm_g.reshape(1, d), cos_sin, lb_logits, ret_gain.reshape(1, width),
      hgrn_gain.reshape(1, width), w_units, *side_f32)
    return outs[0], outs[1], outs[2:]


@jax.jit
def kernel(x, ffn1_norm, ffn1_w_gate, ffn1_w_up, ffn1_w_down, mix_norm, w_in, ret_norm_g, hgrn_lb_logits, hgrn_norm_g, w_out, ffn2_norm, ffn2_w_gate, ffn2_w_up, ffn2_w_down, final_norm):
    batch, seq, d = x.shape
    assert ffn1_norm.shape[0] == 1, "single-layer stack"
    t = batch * seq
    bf = lambda w: w[0].astype(BF16)

    inv = np.power(ROPE_BASE, -np.arange(0, RET_HEAD_DIM, 2, dtype=np.float64) / RET_HEAD_DIM)
    ang = np.arange(seq, dtype=np.float64)[:, None] * inv[None, :]
    cos_sin = jnp.asarray(np.concatenate([np.cos(ang), np.sin(ang)], axis=1), dtype=F32)

    x0 = x.reshape(t, d)
    x1, w_units = _ffn(x0, ffn1_norm[0], bf(ffn1_w_gate), bf(ffn1_w_up), bf(ffn1_w_down),
                       name="ffn1", w_in_f32=w_in[0].astype(F32))
    side = [ffn2_w_gate[0].astype(F32), ffn2_w_up[0].astype(F32), ffn2_w_down[0].astype(F32),
            w_out[0].astype(F32)]
    ret, hgo, (wg2, wu2, wd2, wo) = _mixer(
        x1, mix_norm[0], w_units, hgrn_lb_logits.astype(F32),
        ret_norm_g[0].astype(F32), hgrn_norm_g[0].astype(F32), cos_sin, seq, side)
    x2 = _outproj(x1, ret, hgo, wo)
    out = _ffn(x2, ffn2_norm[0], wg2, wu2, wd2, name="ffn2", final_g=final_norm)
    return out.reshape(batch, seq, d)
```

```python
import functools

import jax
import jax.numpy as jnp
import numpy as np
from jax import lax
from jax.experimental import pallas as pl
from jax.experimental.pallas import tpu as pltpu

F32 = jnp.float32
BF16 = jnp.bfloat16

RET_HEADS = 4
RET_HEAD_DIM = 256
HGRN_HEADS = 8
HGRN_HEAD_DIM = 128
ROPE_BASE = 10000.0
EPS = 1e-6
FFN_RESIDUAL_WEIGHT = 0.5

VMEM_LIMIT_BYTES = 56 * 1024 * 1024
BF16_TILE_ROWS = 16

FFN_TM = 1024
FFN_TF = 512
FFN_EDGE_SPLIT = 2
MIXER_TM = 512
OUTPROJ_TM = 512
RET_CHUNK = 256
HGRN_CHUNK = 64
HGRN_SUB = 8


def _dot(a, b):
    return jnp.dot(a, b, preferred_element_type=F32)


def _dot_nt(a, b):
    return lax.dot_general(a, b, (((1,), (1,)), ((), ())), preferred_element_type=F32)


def _dot_tn(a, b):
    return lax.dot_general(a, b, (((0,), (0,)), ((), ())), preferred_element_type=F32)


def _rmsnorm(x, g):
    return x * lax.rsqrt(jnp.mean(x * x, axis=-1, keepdims=True) + EPS) * g


def _silu(x):
    return x * jax.nn.sigmoid(x)


def _ffn_kernel(*refs, final_norm, cast_side, n_ff_steps):
    refs = list(refs)
    x_ref, g_ref, wg_ref, wu_ref, wd_ref = refs[:5]
    rest = refs[5:]
    fg_ref = rest.pop(0) if final_norm else None
    side_in_ref = rest.pop(0) if cast_side else None
    out_ref = rest.pop(0)
    side_out_ref = rest.pop(0) if cast_side else None
    (h_ref,) = rest

    j = pl.program_id(1)
    tm = x_ref.shape[0]
    part = tm // FFN_EDGE_SPLIT
    parts = [slice(p * part, (p + 1) * part) for p in range(FFN_EDGE_SPLIT)]

    def ride_along_cast():
        if cast_side:
            tn = side_out_ref.shape[2]
            for jj in range(side_out_ref.shape[0]):
                side_out_ref[jj] = side_in_ref[:, jj * tn:(jj + 1) * tn].astype(BF16)

    def ffn_slice(h):
        gate = _dot(h, wg_ref[...])
        up = _dot(h, wu_ref[...])
        act = (_silu(gate) * up * FFN_RESIDUAL_WEIGHT).astype(BF16)
        return _dot(act, wd_ref[...])

    @pl.when(j == 0)
    def _():
        ride_along_cast()
        for rows in parts:
            x = x_ref[rows, :]
            h = _rmsnorm(x, g_ref[...]).astype(BF16)
            h_ref[rows, :] = h
            out_ref[rows, :] = x + ffn_slice(h)

    last = n_ff_steps - 1 if final_norm else n_ff_steps

    @pl.when((j > 0) & (j < last))
    def _():
        ride_along_cast()
        for rows in parts:
            out_ref[rows, :] += ffn_slice(h_ref[rows, :])

    if final_norm:
        @pl.when(j == last)
        def _():
            ride_along_cast()
            for rows in parts:
                y = out_ref[rows, :] + ffn_slice(h_ref[rows, :])
                out_ref[rows, :] = _rmsnorm(y, fg_ref[...])


def _ffn(x, norm_g, w_gate, w_up, w_down, *, name, final_g=None, w_in_f32=None):
    t, d = x.shape
    d_ff = w_gate.shape[1]
    tm, tf = FFN_TM, FFN_TF
    assert t % tm == 0 and d_ff % tf == 0
    n_blocks = t // tm
    n_ff_steps = d_ff // tf
    row = lambda i, j: (i, 0)
    const = lambda i, j: (0, 0)
    in_specs = [pl.BlockSpec((tm, d), row),
                pl.BlockSpec((1, d), const),
                pl.BlockSpec((d, tf), lambda i, j: (0, j)),
                pl.BlockSpec((d, tf), lambda i, j: (0, j)),
                pl.BlockSpec((tf, d), lambda i, j: (j, 0))]
    args = [x, norm_g.reshape(1, d), w_gate, w_up, w_down]
    out_specs = [pl.BlockSpec((tm, d), row)]
    out_shape = [jax.ShapeDtypeStruct((t, d), F32)]
    if final_g is not None:
        in_specs.append(pl.BlockSpec((1, d), const))
        args.append(final_g.reshape(1, d))
    if w_in_f32 is not None:
        n_groups = 8
        tn = RET_HEAD_DIM
        width = w_in_f32.shape[1] // n_groups
        nj = width // tn
        rows = d // n_blocks
        assert d % n_blocks == 0 and rows % BF16_TILE_ROWS == 0 and n_ff_steps >= n_groups
        group = lambda i, j: (i, jnp.minimum(j, n_groups - 1))
        in_specs.append(pl.BlockSpec((rows, width), group))
        args.append(w_in_f32)
        out_specs.append(pl.BlockSpec((nj, None, rows, tn),
                                      lambda i, j: (0, jnp.minimum(j, n_groups - 1), i, 0)))
        out_shape.append(jax.ShapeDtypeStruct((nj, n_groups, d, tn), BF16))
    kern = functools.partial(_ffn_kernel, final_norm=final_g is not None,
                             cast_side=w_in_f32 is not None, n_ff_steps=n_ff_steps)
    outs = pl.pallas_call(
        kern,
        grid=(n_blocks, n_ff_steps),
        in_specs=in_specs,
        out_specs=out_specs,
        out_shape=out_shape,
        scratch_shapes=[pltpu.VMEM((tm, d), BF16)],
        compiler_params=pltpu.CompilerParams(
            dimension_semantics=("parallel", "arbitrary"), vmem_limit_bytes=VMEM_LIMIT_BYTES),
        name=name,
    )(*args)
    return outs if w_in_f32 is not None else outs[0]


def _outproj_kernel(x_ref, ret_ref, hgo_ref, wo_r_ref, wo_h_ref, out_ref):
    out_ref[...] = (x_ref[...] + _dot(ret_ref[...], wo_r_ref[...])
                    + _dot(hgo_ref[...], wo_h_ref[...]))


def _outproj(x1, ret, hgo, w_out):
    t, d = x1.shape
    tm = OUTPROJ_TM
    half = ret.shape[1]
    assert t % tm == 0 and hgo.shape[1] == half and w_out.shape == (2 * half, d)
    row = lambda i: (i, 0)
    return pl.pallas_call(
        _outproj_kernel,
        grid=(t // tm,),
        in_specs=[pl.BlockSpec((tm, d), row),
                  pl.BlockSpec((tm, half), row), pl.BlockSpec((tm, half), row),
                  pl.BlockSpec((half, d), lambda i: (0, 0)), pl.BlockSpec((half, d), lambda i: (1, 0))],
        out_specs=pl.BlockSpec((tm, d), row),
        out_shape=jax.ShapeDtypeStruct((t, d), F32),
        compiler_params=pltpu.CompilerParams(
            dimension_semantics=("parallel",), vmem_limit_bytes=VMEM_LIMIT_BYTES),
        name="outproj",
    )(x1, ret, hgo, w_out, w_out)


class _RetentionHead:
    def __init__(self, q, k, v, head, c):
        tb, dk = q.shape
        self.c = c
        self.chunks = [slice(n * c, (n + 1) * c) for n in range(tb // c)]
        self.q, self.k, self.v = q, k, v

        def log_gamma(shape):
            return jnp.log(1.0 - jnp.exp2(-5.0 - jnp.full(shape, head, jnp.int32).astype(F32)))

        row = lax.broadcasted_iota(jnp.int32, (c, dk), 0).astype(F32)
        rel = (lax.broadcasted_iota(jnp.int32, (c, c), 0) -
               lax.broadcasted_iota(jnp.int32, (c, c), 1)).astype(F32)
        self.decay = jnp.where(rel >= 0, jnp.exp(log_gamma((c, c)) * jnp.maximum(rel, 0.0)), 0.0)
        self.q_dec = jnp.exp(log_gamma((c, dk)) * (row + 1.0))
        self.k_dec = jnp.exp(log_gamma((c, dk)) * (c - 1.0 - row))
        self.g_chunk = jnp.exp(log_gamma((1, dk)) * float(c))

    def free_dots(self):
        self.scores = [_dot_nt(self.q[r], self.k[r]) for r in self.chunks]
        self.updates = [_dot_tn((self.k[r].astype(F32) * self.k_dec).astype(BF16), self.v[r])
                        for r in self.chunks]

    def outputs(self, state):
        states = []
        for upd in self.updates:
            states.append(state.astype(BF16))
            state = self.g_chunk * state + upd
        outs = []
        for r, s, st in zip(self.chunks, self.scores, states):
            inner = _dot((s * self.decay).astype(BF16), self.v[r])
            outs.append(inner + _dot(self.q[r], st) * self.q_dec)
        return outs, state


class _HgrnConsts:
    def __init__(self, c, sub):
        d = HGRN_HEAD_DIM
        pair = BF16_TILE_ROWS
        self.c, self.sub, self.pair = c, sub, pair
        self.tri = (lax.broadcasted_iota(jnp.int32, (c, c), 0) >=
                    lax.broadcasted_iota(jnp.int32, (c, c), 1)).astype(BF16)
        self.pair_lane = lax.broadcasted_iota(jnp.int32, (pair, d), 1) % sub
        a_row = lax.broadcasted_iota(jnp.int32, (c, c), 0)
        a_col = lax.broadcasted_iota(jnp.int32, (c, c), 1)
        self.diag_mask = ((a_row // sub) == (a_col // sub)) & (a_col <= a_row)
        self.halves = []
        h = c // 2
        while h >= sub:
            self.halves.append(h)
            h //= 2
        self.level_masks = [(a_row // (2 * h)) == (a_col // (2 * h)) for h in self.halves]


class _HgrnHead:
    def __init__(self, q, k, v, lf, cp_ref, consts):
        self.q, self.k, self.v, self.lf, self.cp_ref, self.cs = q, k, v, lf, cp_ref, consts
        c = consts.c
        self.chunks = [slice(n * c, (n + 1) * c) for n in range(q.shape[0] // c)]

    def cumsum_dots(self):
        d = HGRN_HEAD_DIM
        self.cums = []
        for rows in self.chunks:
            lf = self.lf[rows]
            lf1 = lf.astype(BF16)
            r1 = lf - lf1.astype(F32)
            lf2 = r1.astype(BF16)
            lf3 = (r1 - lf2.astype(F32)).astype(BF16)
            cum3 = _dot(self.cs.tri, jnp.concatenate([lf1, lf2, lf3], axis=1))
            self.cums.append(cum3[:, :d] + cum3[:, d:2 * d] + cum3[:, 2 * d:])

    def elementwise(self):
        cs, cp_ref = self.cs, self.cp_ref
        c, sub, pair, d = cs.c, cs.sub, cs.pair, HGRN_HEAD_DIM
        self.level_ops, self.a_diags, self.q_ins, self.k_ends, self.decays = [], [], [], [], []
        for rows, cum in zip(self.chunks, self.cums):
            r0 = rows.start
            q = self.q[rows].astype(F32)
            k = self.k[rows].astype(F32)
            cp = cum - jnp.log2(k)
            cp_ref[rows, :] = cp
            total = cum[c - 1:c, :]
            self.q_ins.append((q * jnp.exp2(cum)).astype(BF16))
            self.k_ends.append(jnp.exp2(total - cp).astype(BF16))
            self.decays.append(jnp.exp2(total))

            ops = []
            for h in cs.halves:
                q_parts, k_parts = [], []
                for p0 in range(0, c, 2 * h):
                    lo, up = slice(p0, p0 + h), slice(p0 + h, p0 + 2 * h)
                    c_b = cum[p0 + h - 1:p0 + h, :]
                    q_parts.append(q[up] * jnp.exp2(cum[up] - c_b))
                    k_parts += [jnp.exp2(c_b - cp[lo]), jnp.zeros((h, d), F32)]
                ops.append((jnp.concatenate(q_parts, axis=0).astype(BF16),
                            jnp.concatenate(k_parts, axis=0).astype(BF16)))
            self.level_ops.append(ops)

            pairs = []
            for i0 in range(0, c, pair):
                q_i = q[i0:i0 + pair]
                c_i = cum[i0:i0 + pair]
                a_pair = jnp.zeros((pair, d), F32)
                for j in range(sub):
                    srcs = [cp_ref[r0 + i0 + s + j:r0 + i0 + s + j + 1, :]
                            for s in range(0, pair, sub)]
                    if len(srcs) == 1:
                        cp_j = srcs[0]
                    else:
                        cp_j = jnp.concatenate([jnp.broadcast_to(s, (sub, d)) for s in srcs], axis=0)
                    s_j = jnp.sum(q_i * jnp.exp2(c_i - cp_j), axis=-1, keepdims=True)
                    a_pair = jnp.where(cs.pair_lane == j, s_j, a_pair)
                pairs.append(a_pair)
            self.a_diags.append(jnp.concatenate(pairs, axis=0)[:, :c])

    def free_dots(self):
        self.lows = [[_dot_nt(q_l, k_l) for q_l, k_l in ops] for ops in self.level_ops]
        self.updates = [_dot_tn(self.v[rows], k_end) for rows, k_end in zip(self.chunks, self.k_ends)]

    def outputs(self, state):
        cs = self.cs
        states = []
        for decay, upd in zip(self.decays, self.updates):
            states.append(jnp.transpose(state).astype(BF16))
            state = decay * state + upd
        outs = []
        for n, rows in enumerate(self.chunks):
            attn = jnp.where(cs.diag_mask, self.a_diags[n], 0.0)
            for h, mask, low in zip(cs.halves, cs.level_masks, self.lows[n]):
                parts = []
                for g in range(cs.c // (2 * h)):
                    parts += [jnp.zeros((h, cs.c), F32), low[g * h:(g + 1) * h]]
                low = jnp.concatenate(parts, axis=0)
                attn = attn + (low if 2 * h == cs.c else jnp.where(mask, low, 0.0))
            lhs = jnp.concatenate([self.q_ins[n], attn.astype(BF16)], axis=1)
            rhs = jnp.concatenate([states[n], self.v[rows]], axis=0)
            outs.append(_dot(lhs, rhs))
        return outs, state


_RQ, _RK, _RV, _RG, _HQ, _HK, _HV, _HG = range(8)


def _mixer_kernel(x_ref, g_ref, cos_sin_ref, lbl_ref, rgain_ref, hgain_ref,
                  w_ref, *refs, n_col_blocks, blocks_per_seq, n_side):
    side_in = refs[:n_side]
    ret_ref, hgo_ref = refs[n_side:n_side + 2]
    side_out = refs[n_side + 2:2 * n_side + 2]
    h_ref, act_ref, lf_ref, ret_state_ref, hgrn_state_ref, cp_ref = refs[2 * n_side + 2:]

    s = pl.program_id(0)
    nj = n_col_blocks
    dk, dh = RET_HEAD_DIM, HGRN_HEAD_DIM
    heads_per_step = dk // dh
    cons = jnp.maximum(s - 1, 0)
    cj = cons % nj

    @pl.when(s == 0)
    def _():
        act_ref[...] = jnp.zeros_like(act_ref)
        lf_ref[...] = jnp.zeros_like(lf_ref)

    @pl.when(s % nj == 0)
    def _():
        h_ref[...] = _rmsnorm(x_ref[...], g_ref[...]).astype(BF16)

    @pl.when((cons // nj) % blocks_per_seq == 0)
    def _():
        ret_state_ref[cj] = jnp.zeros((dk, dk), F32)
        for a in range(heads_per_step):
            hgrn_state_ref[heads_per_step * cj + a] = jnp.zeros((dh, dh), F32)

    for i_ref, o_ref in zip(side_in, side_out):
        o_ref[...] = i_ref[...].astype(BF16)

    h = h_ref[...]
    half = dk // 2
    cos = cos_sin_ref[:, :half]
    sin = cos_sin_ref[:, half:]
    dst = act_ref.at[s % 2]
    src = act_ref.at[(s + 1) % 2]
    src_lf = lf_ref.at[(s + 1) % 2]

    def rope(p):
        x1, x2 = p[:, :half], p[:, half:]
        return jnp.concatenate([x1 * cos - x2 * sin, x2 * cos + x1 * sin], axis=-1)

    consts = _HgrnConsts(HGRN_CHUNK, HGRN_SUB)
    hq, hk, hv, hlf = src[_HQ], src[_HK], src[_HV], src_lf[...]
    heads = []
    for a in range(heads_per_step):
        cols = slice(a * dh, (a + 1) * dh)
        heads.append(_HgrnHead(hq[:, cols], hk[:, cols], hv[:, cols], hlf[:, cols],
                               cp_ref.at[a], consts))
    ret = _RetentionHead(src[_RQ], src[_RK], src[_RV], cj, RET_CHUNK)

    def project_pair(first_group):
        return _dot(h, w_ref[first_group]), _dot(h, w_ref[first_group + 1])

    def project_hq_hf():
        p_hq, z = project_pair(_HQ)
        dst[_HQ] = _silu(p_hq).astype(BF16)
        lbl = lbl_ref[...]
        m = jnp.maximum(lbl, 0.0)
        e_l = jnp.exp(lbl - m)
        lb = e_l / (e_l + jnp.exp(-m))
        dst[_HK] = ((1.0 - lb) * jax.nn.sigmoid(-z)).astype(BF16)
        lf_ref[s % 2] = jnp.log2(lb + (1.0 - lb) * jax.nn.sigmoid(z))

    def project_hv_hg():
        p_hv, p_hg = project_pair(_HV)
        dst[_HV] = p_hv.astype(BF16)
        dst[_HG] = _silu(p_hg).astype(BF16)

    def project_rq_rk():
        p_rq, p_rk = project_pair(_RQ)
        dst[_RQ] = (rope(p_rq) * (dk ** -0.5)).astype(BF16)
        dst[_RK] = rope(p_rk).astype(BF16)

    def project_rv_rg():
        p_rv, p_rg = project_pair(_RV)
        dst[_RV] = p_rv.astype(BF16)
        dst[_RG] = _silu(p_rg).astype(BF16)

    def hgrn_outputs():
        hgate = src[_HG]
        for a, hd in enumerate(heads):
            cols = slice(a * dh, (a + 1) * dh)
            outs, state = hd.outputs(hgrn_state_ref[heads_per_step * cj + a])
            hgrn_state_ref[heads_per_step * cj + a] = state
            gain = hgain_ref[:, cols]
            for rows, o in zip(hd.chunks, outs):
                y = o * lax.rsqrt(jnp.mean(o * o, axis=-1, keepdims=True) + EPS)
                hgo_ref[rows, cols] = (y * gain * hgate[rows, cols].astype(F32)).astype(BF16)

    def retention_outputs():
        rgate = src[_RG]
        outs, state = ret.outputs(ret_state_ref[cj])
        ret_state_ref[cj] = state
        gain = rgain_ref[...]
        for rows, o in zip(ret.chunks, outs):
            mu = jnp.mean(o, axis=-1, keepdims=True)
            oc = o - mu
            var = jnp.mean(oc * oc, axis=-1, keepdims=True)
            ret_ref[rows, :] = (oc * lax.rsqrt(var + EPS) * gain
                                * rgate[rows, :].astype(F32)).astype(BF16)

    for hd in heads:
        hd.cumsum_dots()
    project_hq_hf()
    for hd in heads:
        hd.elementwise()
    for hd in heads:
        hd.free_dots()
    ret.free_dots()
    project_hv_hg()
    hgrn_outputs()
    project_rq_rk()
    retention_outputs()
    project_rv_rg()


def _mixer(x1, norm_g, w_units, lb_logits, ret_gain, hgrn_gain, cos_sin, seq, side_f32):
    t, d = x1.shape
    tm = MIXER_TM
    tn = RET_HEAD_DIM
    dh = HGRN_HEAD_DIM
    width = RET_HEADS * RET_HEAD_DIM
    assert w_units.shape == (width // tn, 8, d, tn) and width == HGRN_HEADS * dh and tn % dh == 0
    assert t % tm == 0 and seq % tm == 0
    assert tm % RET_CHUNK == 0 and tm % HGRN_CHUNK == 0
    assert HGRN_CHUNK % BF16_TILE_ROWS == 0 and BF16_TILE_ROWS % HGRN_SUB == 0
    nj = width // tn
    blocks_per_seq = seq // tm
    n_units = (t // tm) * nj
    prod = lambda s: jnp.minimum(s, n_units - 1)
    cons = lambda s: jnp.maximum(s - 1, 0)
    const = lambda s: (0, 0)
    pos = lambda s: ((prod(s) // nj) % blocks_per_seq, 0)
    cons_col = lambda s: (0, cons(s) % nj)

    side_specs = []
    for w in side_f32:
        rows = next(r for r in range(BF16_TILE_ROWS, w.shape[0] + 1, BF16_TILE_ROWS)
                    if w.shape[0] % r == 0 and w.shape[0] // r <= n_units)
        side_specs.append(pl.BlockSpec(
            (rows, w.shape[1]), lambda s, last=w.shape[0] // rows - 1: (jnp.minimum(s, last), 0)))

    out_spec = pl.BlockSpec((tm, tn), lambda s: (cons(s) // nj, cons(s) % nj))
    out_bf = jax.ShapeDtypeStruct((t, width), BF16)
    outs = pl.pallas_call(
        functools.partial(_mixer_kernel, n_col_blocks=nj, blocks_per_seq=blocks_per_seq,
                          n_side=len(side_f32)),
        grid=(n_units + 1,),
        in_specs=[pl.BlockSpec((tm, d), lambda s: (prod(s) // nj, 0)), pl.BlockSpec((1, d), const),
                  pl.BlockSpec((tm, tn), pos),
                  pl.BlockSpec((1, tn), lambda s: (0, prod(s) % nj)),
                  pl.BlockSpec((1, tn), cons_col), pl.BlockSpec((1, tn), cons_col),
                  pl.BlockSpec((None, 8, d, tn), lambda s: (prod(s) % nj, 0, 0, 0))] + side_specs,
        out_specs=[out_spec, out_spec] + side_specs,
        out_shape=[out_bf, out_bf] + [jax.ShapeDtypeStruct(w.shape, BF16) for w in side_f32],
        scratch_shapes=[pltpu.VMEM((tm, d), BF16),
                        pltpu.VMEM((2, 8, tm, tn), BF16),
                        pltpu.VMEM((2, tm, tn), F32),
                        pltpu.VMEM((RET_HEADS, RET_HEAD_DIM, RET_HEAD_DIM), F32),
                        pltpu.VMEM((HGRN_HEADS, dh, dh), F32),
                        pltpu.VMEM((tn // dh, tm, dh), F32)],
        compiler_params=pltpu.CompilerParams(
            dimension_semantics=("arbitrary",), vmem_limit_bytes=VMEM_LIMIT_BYTES),
        name="mixer",
    )(x1, norm_g.reshape(1, d), cos_sin, lb_logits, ret_gain.reshape(1, width),
      hgrn_gain.reshape(1, width), w_units, *side_f32)
    return outs[0], outs[1], outs[2:]


@jax.jit
def kernel(x, ffn1_norm, ffn1_w_gate, ffn1_w_up, ffn1_w_down, mix_norm, w_in, ret_norm_g, hgrn_lb_logits, hgrn_norm_g, w_out, ffn2_norm, ffn2_w_gate, ffn2_w_up, ffn2_w_down, final_norm):
    batch, seq, d = x.shape
    assert ffn1_norm.shape[0] == 1, "single-layer stack"
    t = batch * seq
    bf = lambda w: w[0].astype(BF16)

    inv = np.power(ROPE_BASE, -np.arange(0, RET_HEAD_DIM, 2, dtype=np.float64) / RET_HEAD_DIM)
    ang = np.arange(seq, dtype=np.float64)[:, None] * inv[None, :]
    cos_sin = jnp.asarray(np.concatenate([np.cos(ang), np.sin(ang)], axis=1), dtype=F32)

    x0 = x.reshape(t, d)
    x1, w_units = _ffn(x0, ffn1_norm[0], bf(ffn1_w_gate), bf(ffn1_w_up), bf(ffn1_w_down),
                       name="ffn1", w_in_f32=w_in[0].astype(F32))
    side = [ffn2_w_gate[0].astype(F32), ffn2_w_up[0].astype(F32), ffn2_w_down[0].astype(F32),
            w_out[0].astype(F32)]
    ret, hgo, (wg2, wu2, wd2, wo) = _mixer(
        x1, mix_norm[0], w_units, hgrn_lb_logits.astype(F32),
        ret_norm_g[0].astype(F32), hgrn_norm_g[0].astype(F32), cos_sin, seq, side)
    x2 = _outproj(x1, ret, hgo, wo)
    out = _ffn(x2, ffn2_norm[0], wg2, wu2, wd2, name="ffn2", final_g=final_norm)
    return out.reshape(batch, seq, d)
```

```python
import functools

import jax
import jax.numpy as jnp
import numpy as np
from jax import lax
from jax.experimental import pallas as pl
from jax.experimental.pallas import tpu as pltpu

F32 = jnp.float32
BF16 = jnp.bfloat16

RET_HEADS = 4
RET_HEAD_DIM = 256
HGRN_HEADS = 8
HGRN_HEAD_DIM = 128
ROPE_BASE = 10000.0
EPS = 1e-6
FFN_RESIDUAL_WEIGHT = 0.5

VMEM_LIMIT_BYTES = 56 * 1024 * 1024
BF16_TILE_ROWS = 16

FFN_TM = 1024
FFN_TF = 512
FFN_EDGE_SPLIT = 2
MIXER_TM = 512
OUTPROJ_TM = 512
RET_CHUNK = 256
HGRN_CHUNK = 64
HGRN_SUB = 8


def _dot(a, b):
    return jnp.dot(a, b, preferred_element_type=F32)


def _dot_nt(a, b):
    return lax.dot_general(a, b, (((1,), (1,)), ((), ())), preferred_element_type=F32)


def _dot_tn(a, b):
    return lax.dot_general(a, b, (((0,), (0,)), ((), ())), preferred_element_type=F32)


def _rmsnorm(x, g):
    return x * lax.rsqrt(jnp.mean(x * x, axis=-1, keepdims=True) + EPS) * g


def _silu(x):
    return x * jax.nn.sigmoid(x)


def _ffn_kernel(*refs, final_norm, cast_side, n_ff_steps):
    refs = list(refs)
    x_ref, g_ref, wg_ref, wu_ref, wd_ref = refs[:5]
    rest = refs[5:]
    fg_ref = rest.pop(0) if final_norm else None
    side_in_ref = rest.pop(0) if cast_side else None
    out_ref = rest.pop(0)
    side_out_ref = rest.pop(0) if cast_side else None
    (h_ref,) = rest

    j = pl.program_id(1)
    tm = x_ref.shape[0]
    part = tm // FFN_EDGE_SPLIT
    parts = [slice(p * part, (p + 1) * part) for p in range(FFN_EDGE_SPLIT)]

    def ride_along_cast():
        if cast_side:
            tn = side_out_ref.shape[2]
            for jj in range(side_out_ref.shape[0]):
                side_out_ref[jj] = side_in_ref[:, jj * tn:(jj + 1) * tn].astype(BF16)

    def ffn_slice(h):
        gate = _dot(h, wg_ref[...])
        up = _dot(h, wu_ref[...])
        act = (_silu(gate) * up * FFN_RESIDUAL_WEIGHT).astype(BF16)
        return _dot(act, wd_ref[...])

    @pl.when(j == 0)
    def _():
        ride_along_cast()
        for rows in parts:
            x = x_ref[rows, :]
            h = _rmsnorm(x, g_ref[...]).astype(BF16)
            h_ref[rows, :] = h
            out_ref[rows, :] = x + ffn_slice(h)

    last = n_ff_steps - 1 if final_norm else n_ff_steps

    @pl.when((j > 0) & (j < last))
    def _():
        ride_along_cast()
        out_ref[...] += ffn_slice(h_ref[...])

    if final_norm:
        @pl.when(j == last)
        def _():
            ride_along_cast()
            for rows in parts:
                y = out_ref[rows, :] + ffn_slice(h_ref[rows, :])
                out_ref[rows, :] = _rmsnorm(y, fg_ref[...])


def _ffn(x, norm_g, w_gate, w_up, w_down, *, name, final_g=None, w_in_f32=None):
    t, d = x.shape
    d_ff = w_gate.shape[1]
    tm, tf = FFN_TM, FFN_TF
    assert t % tm == 0 and d_ff % tf == 0
    n_blocks = t // tm
    n_ff_steps = d_ff // tf
    row = lambda i, j: (i, 0)
    const = lambda i, j: (0, 0)
    in_specs = [pl.BlockSpec((tm, d), row),
                pl.BlockSpec((1, d), const),
                pl.BlockSpec((d, tf), lambda i, j: (0, j)),
                pl.BlockSpec((d, tf), lambda i, j: (0, j)),
                pl.BlockSpec((tf, d), lambda i, j: (j, 0))]
    args = [x, norm_g.reshape(1, d), w_gate, w_up, w_down]
    out_specs = [pl.BlockSpec((tm, d), row)]
    out_shape = [jax.ShapeDtypeStruct((t, d), F32)]
    if final_g is not None:
        in_specs.append(pl.BlockSpec((1, d), const))
        args.append(final_g.reshape(1, d))
    if w_in_f32 is not None:
        n_groups = 8
        tn = RET_HEAD_DIM
        width = w_in_f32.shape[1] // n_groups
        nj = width // tn
        rows = d // n_blocks
        assert d % n_blocks == 0 and rows % BF16_TILE_ROWS == 0 and n_ff_steps >= n_groups
        group = lambda i, j: (i, jnp.minimum(j, n_groups - 1))
        in_specs.append(pl.BlockSpec((rows, width), group))
        args.append(w_in_f32)
        out_specs.append(pl.BlockSpec((nj, None, rows, tn),
                                      lambda i, j: (0, jnp.minimum(j, n_groups - 1), i, 0)))
        out_shape.append(jax.ShapeDtypeStruct((nj, n_groups, d, tn), BF16))
    kern = functools.partial(_ffn_kernel, final_norm=final_g is not None,
                             cast_side=w_in_f32 is not None, n_ff_steps=n_ff_steps)
    outs = pl.pallas_call(
        kern,
        grid=(n_blocks, n_ff_steps),
        in_specs=in_specs,
        out_specs=out_specs,
        out_shape=out_shape,
        scratch_shapes=[pltpu.VMEM((tm, d), BF16)],
        compiler_params=pltpu.CompilerParams(
            dimension_semantics=("parallel", "arbitrary"), vmem_limit_bytes=VMEM_LIMIT_BYTES),
        name=name,
    )(*args)
    return outs if w_in_f32 is not None else outs[0]


def _ffn_inner_kernel(x_ref, g_ref, fg_ref, wg_hbm, wu_hbm, wd_hbm, out_ref, h_ref, *, tf):
    d = x_ref.shape[1]
    x = x_ref[...]
    out_ref[...] = x
    h_ref[...] = _rmsnorm(x, g_ref[...]).astype(BF16)

    def ff_tile(wg_ref, wu_ref, wd_ref):
        h = h_ref[...]
        gate = _dot(h, wg_ref[...])
        up = _dot(h, wu_ref[...])
        act = (_silu(gate) * up * FFN_RESIDUAL_WEIGHT).astype(BF16)
        out_ref[...] += _dot(act, wd_ref[...])

    pltpu.emit_pipeline(
        ff_tile,
        grid=(wg_hbm.shape[1] // tf,),
        in_specs=[pl.BlockSpec((d, tf), lambda j: (0, j)),
                  pl.BlockSpec((d, tf), lambda j: (0, j)),
                  pl.BlockSpec((tf, d), lambda j: (j, 0))],
    )(wg_hbm, wu_hbm, wd_hbm)

    out_ref[...] = _rmsnorm(out_ref[...], fg_ref[...])


def _ffn_inner(x, norm_g, w_gate, w_up, w_down, final_g, *, name):
    t, d = x.shape
    tm, tf = FFN_TM, FFN_TF
    assert t % tm == 0 and w_gate.shape[1] % tf == 0
    row = lambda i: (i, 0)
    const = lambda i: (0, 0)
    hbm = pl.BlockSpec(memory_space=pl.ANY)
    return pl.pallas_call(
        functools.partial(_ffn_inner_kernel, tf=tf),
        grid=(t // tm,),
        in_specs=[pl.BlockSpec((tm, d), row), pl.BlockSpec((1, d), const), pl.BlockSpec((1, d), const),
                  hbm, hbm, hbm],
        out_specs=pl.BlockSpec((tm, d), row),
        out_shape=jax.ShapeDtypeStruct((t, d), F32),
        scratch_shapes=[pltpu.VMEM((tm, d), BF16)],
        compiler_params=pltpu.CompilerParams(
            dimension_semantics=("arbitrary",), vmem_limit_bytes=VMEM_LIMIT_BYTES),
        name=name,
    )(x, norm_g.reshape(1, d), final_g.reshape(1, d), w_gate, w_up, w_down)


def _outproj_kernel(x_ref, ret_ref, hgo_ref, wo_r_ref, wo_h_ref, out_ref):
    out_ref[...] = (x_ref[...] + _dot(ret_ref[...], wo_r_ref[...])
                    + _dot(hgo_ref[...], wo_h_ref[...]))


def _outproj(x1, ret, hgo, w_out):
    t, d = x1.shape
    tm = OUTPROJ_TM
    half = ret.shape[1]
    assert t % tm == 0 and hgo.shape[1] == half and w_out.shape == (2 * half, d)
    row = lambda i: (i, 0)
    return pl.pallas_call(
        _outproj_kernel,
        grid=(t // tm,),
        in_specs=[pl.BlockSpec((tm, d), row),
                  pl.BlockSpec((tm, half), row), pl.BlockSpec((tm, half), row),
                  pl.BlockSpec((half, d), lambda i: (0, 0)), pl.BlockSpec((half, d), lambda i: (1, 0))],
        out_specs=pl.BlockSpec((tm, d), row),
        out_shape=jax.ShapeDtypeStruct((t, d), F32),
        compiler_params=pltpu.CompilerParams(
            dimension_semantics=("parallel",), vmem_limit_bytes=VMEM_LIMIT_BYTES),
        name="outproj",
    )(x1, ret, hgo, w_out, w_out)


class _RetentionHead:
    def __init__(self, q, k, v, head, c):
        tb, dk = q.shape
        self.c = c
        self.chunks = [slice(n * c, (n + 1) * c) for n in range(tb // c)]
        self.q, self.k, self.v = q, k, v

        def log_gamma(shape):
            return jnp.log(1.0 - jnp.exp2(-5.0 - jnp.full(shape, head, jnp.int32).astype(F32)))

        row = lax.broadcasted_iota(jnp.int32, (c, dk), 0).astype(F32)
        rel = (lax.broadcasted_iota(jnp.int32, (c, c), 0) -
               lax.broadcasted_iota(jnp.int32, (c, c), 1)).astype(F32)
        self.decay = jnp.where(rel >= 0, jnp.exp(log_gamma((c, c)) * jnp.maximum(rel, 0.0)), 0.0)
        self.q_dec = jnp.exp(log_gamma((c, dk)) * (row + 1.0))
        self.k_dec = jnp.exp(log_gamma((c, dk)) * (c - 1.0 - row))
        self.g_chunk = jnp.exp(log_gamma((1, dk)) * float(c))

    def free_dots(self):
        self.scores = [_dot_nt(self.q[r], self.k[r]) for r in self.chunks]
        self.updates = [_dot_tn((self.k[r].astype(F32) * self.k_dec).astype(BF16), self.v[r])
                        for r in self.chunks]

    def outputs(self, state):
        states = []
        for upd in self.updates:
            states.append(state.astype(BF16))
            state = self.g_chunk * state + upd
        outs = []
        for r, s, st in zip(self.chunks, self.scores, states):
            inner = _dot((s * self.decay).astype(BF16), self.v[r])
            outs.append(inner + _dot(self.q[r], st) * self.q_dec)
        return outs, state


class _HgrnConsts:
    def __init__(self, c, sub):
        d = HGRN_HEAD_DIM
        pair = BF16_TILE_ROWS
        self.c, self.sub, self.pair = c, sub, pair
        self.tri = (lax.broadcasted_iota(jnp.int32, (c, c), 0) >=
                    lax.broadcasted_iota(jnp.int32, (c, c), 1)).astype(BF16)
        self.pair_lane = lax.broadcasted_iota(jnp.int32, (pair, d), 1) % sub
        a_row = lax.broadcasted_iota(jnp.int32, (c, c), 0)
        a_col = lax.broadcasted_iota(jnp.int32, (c, c), 1)
        self.diag_mask = ((a_row // sub) == (a_col // sub)) & (a_col <= a_row)
        self.halves = []
        h = c // 2
        while h >= sub:
            self.halves.append(h)
            h //= 2
        self.level_masks = [(a_row // (2 * h)) == (a_col // (2 * h)) for h in self.halves]


class _HgrnHead:
    def __init__(self, q, k, v, lf, cp_ref, consts):
        self.q, self.k, self.v, self.lf, self.cp_ref, self.cs = q, k, v, lf, cp_ref, consts
        c = consts.c
        self.chunks = [slice(n * c, (n + 1) * c) for n in range(q.shape[0] // c)]

    def cumsum_dots(self):
        d = HGRN_HEAD_DIM
        self.cums = []
        for rows in self.chunks:
            lf = self.lf[rows]
            lf1 = lf.astype(BF16)
            r1 = lf - lf1.astype(F32)
            lf2 = r1.astype(BF16)
            lf3 = (r1 - lf2.astype(F32)).astype(BF16)
            cum3 = _dot(self.cs.tri, jnp.concatenate([lf1, lf2, lf3], axis=1))
            self.cums.append(cum3[:, :d] + cum3[:, d:2 * d] + cum3[:, 2 * d:])

    def elementwise(self):
        cs, cp_ref = self.cs, self.cp_ref
        c, sub, pair, d = cs.c, cs.sub, cs.pair, HGRN_HEAD_DIM
        self.level_ops, self.a_diags, self.q_ins, self.k_ends, self.decays = [], [], [], [], []
        for rows, cum in zip(self.chunks, self.cums):
            r0 = rows.start
            q = self.q[rows].astype(F32)
            k = self.k[rows].astype(F32)
            cp = cum - jnp.log2(k)
            cp_ref[rows, :] = cp
            total = cum[c - 1:c, :]
            self.q_ins.append((q * jnp.exp2(cum)).astype(BF16))
            self.k_ends.append(jnp.exp2(total - cp).astype(BF16))
            self.decays.append(jnp.exp2(total))

            ops = []
            for h in cs.halves:
                q_parts, k_parts = [], []
                for p0 in range(0, c, 2 * h):
                    lo, up = slice(p0, p0 + h), slice(p0 + h, p0 + 2 * h)
                    c_b = cum[p0 + h - 1:p0 + h, :]
                    q_parts.append(q[up] * jnp.exp2(cum[up] - c_b))
                    k_parts += [jnp.exp2(c_b - cp[lo]), jnp.zeros((h, d), F32)]
                ops.append((jnp.concatenate(q_parts, axis=0).astype(BF16),
                            jnp.concatenate(k_parts, axis=0).astype(BF16)))
            self.level_ops.append(ops)

            pairs = []
            for i0 in range(0, c, pair):
                q_i = q[i0:i0 + pair]
                c_i = cum[i0:i0 + pair]
                a_pair = jnp.zeros((pair, d), F32)
                for j in range(sub):
                    srcs = [cp_ref[r0 + i0 + s + j:r0 + i0 + s + j + 1, :]
                            for s in range(0, pair, sub)]
                    if len(srcs) == 1:
                        cp_j = srcs[0]
                    else:
                        cp_j = jnp.concatenate([jnp.broadcast_to(s, (sub, d)) for s in srcs], axis=0)
                    s_j = jnp.sum(q_i * jnp.exp2(c_i - cp_j), axis=-1, keepdims=True)
                    a_pair = jnp.where(cs.pair_lane == j, s_j, a_pair)
                pairs.append(a_pair)
            self.a_diags.append(jnp.concatenate(pairs, axis=0)[:, :c])

    def free_dots(self):
        self.lows = [[_dot_nt(q_l, k_l) for q_l, k_l in ops] for ops in self.level_ops]
        self.updates = [_dot_tn(self.v[rows], k_end) for rows, k_end in zip(self.chunks, self.k_ends)]

    def outputs(self, state):
        cs = self.cs
        states = []
        for decay, upd in zip(self.decays, self.updates):
            states.append(jnp.transpose(state).astype(BF16))
            state = decay * state + upd
        outs = []
        for n, rows in enumerate(self.chunks):
            attn = jnp.where(cs.diag_mask, self.a_diags[n], 0.0)
            for h, mask, low in zip(cs.halves, cs.level_masks, self.lows[n]):
                parts = []
                for g in range(cs.c // (2 * h)):
                    parts += [jnp.zeros((h, cs.c), F32), low[g * h:(g + 1) * h]]
                low = jnp.concatenate(parts, axis=0)
                attn = attn + (low if 2 * h == cs.c else jnp.where(mask, low, 0.0))
            lhs = jnp.concatenate([self.q_ins[n], attn.astype(BF16)], axis=1)
            rhs = jnp.concatenate([states[n], self.v[rows]], axis=0)
            outs.append(_dot(lhs, rhs))
        return outs, state


_RQ, _RK, _RV, _RG, _HQ, _HK, _HV, _HG = range(8)


def _mixer_kernel(x_ref, g_ref, cos_sin_ref, lbl_ref, rgain_ref, hgain_ref,
                  w_ref, *refs, n_col_blocks, blocks_per_seq, n_side):
    side_in = refs[:n_side]
    ret_ref, hgo_ref = refs[n_side:n_side + 2]
    side_out = refs[n_side + 2:2 * n_side + 2]
    h_ref, act_ref, lf_ref, ret_state_ref, hgrn_state_ref, cp_ref = refs[2 * n_side + 2:]

    s = pl.program_id(0)
    nj = n_col_blocks
    dk, dh = RET_HEAD_DIM, HGRN_HEAD_DIM
    heads_per_step = dk // dh
    cons = jnp.maximum(s - 1, 0)
    cj = cons % nj

    @pl.when(s == 0)
    def _():
        act_ref[...] = jnp.zeros_like(act_ref)
        lf_ref[...] = jnp.zeros_like(lf_ref)

    @pl.when(s % nj == 0)
    def _():
        h_ref[...] = _rmsnorm(x_ref[...], g_ref[...]).astype(BF16)

    @pl.when((cons // nj) % blocks_per_seq == 0)
    def _():
        ret_state_ref[cj] = jnp.zeros((dk, dk), F32)
        for a in range(heads_per_step):
            hgrn_state_ref[heads_per_step * cj + a] = jnp.zeros((dh, dh), F32)

    for i_ref, o_ref in zip(side_in, side_out):
        o_ref[...] = i_ref[...].astype(BF16)

    h = h_ref[...]
    half = dk // 2
    cos = cos_sin_ref[:, :half]
    sin = cos_sin_ref[:, half:]
    dst = act_ref.at[s % 2]
    src = act_ref.at[(s + 1) % 2]
    src_lf = lf_ref.at[(s + 1) % 2]

    def rope(p):
        x1, x2 = p[:, :half], p[:, half:]
        return jnp.concatenate([x1 * cos - x2 * sin, x2 * cos + x1 * sin], axis=-1)

    consts = _HgrnConsts(HGRN_CHUNK, HGRN_SUB)
    hq, hk, hv, hlf = src[_HQ], src[_HK], src[_HV], src_lf[...]
    heads = []
    for a in range(heads_per_step):
        cols = slice(a * dh, (a + 1) * dh)
        heads.append(_HgrnHead(hq[:, cols], hk[:, cols], hv[:, cols], hlf[:, cols],
                               cp_ref.at[a], consts))
    ret = _RetentionHead(src[_RQ], src[_RK], src[_RV], cj, RET_CHUNK)

    def project_pair(first_group):
        return _dot(h, w_ref[first_group]), _dot(h, w_ref[first_group + 1])

    def project_hq_hf():
        p_hq, z = project_pair(_HQ)
        dst[_HQ] = _silu(p_hq).astype(BF16)
        lbl = lbl_ref[...]
        m = jnp.maximum(lbl, 0.0)
        e_l = jnp.exp(lbl - m)
        lb = e_l / (e_l + jnp.exp(-m))
        dst[_HK] = ((1.0 - lb) * jax.nn.sigmoid(-z)).astype(BF16)
        lf_ref[s % 2] = jnp.log2(lb + (1.0 - lb) * jax.nn.sigmoid(z))

    def project_hv_hg():
        p_hv, p_hg = project_pair(_HV)
        dst[_HV] = p_hv.astype(BF16)
        dst[_HG] = _silu(p_hg).astype(BF16)

    def project_rq_rk():
        p_rq, p_rk = project_pair(_RQ)
        dst[_RQ] = (rope(p_rq) * (dk ** -0.5)).astype(BF16)
        dst[_RK] = rope(p_rk).astype(BF16)

    def project_rv_rg():
        p_rv, p_rg = project_pair(_RV)
        dst[_RV] = p_rv.astype(BF16)
        dst[_RG] = _silu(p_rg).astype(BF16)

    def hgrn_outputs():
        hgate = src[_HG]
        for a, hd in enumerate(heads):
            cols = slice(a * dh, (a + 1) * dh)
            outs, state = hd.outputs(hgrn_state_ref[heads_per_step * cj + a])
            hgrn_state_ref[heads_per_step * cj + a] = state
            gain = hgain_ref[:, cols]
            for rows, o in zip(hd.chunks, outs):
                y = o * lax.rsqrt(jnp.mean(o * o, axis=-1, keepdims=True) + EPS)
                hgo_ref[rows, cols] = (y * gain * hgate[rows, cols].astype(F32)).astype(BF16)

    def retention_outputs():
        rgate = src[_RG]
        outs, state = ret.outputs(ret_state_ref[cj])
        ret_state_ref[cj] = state
        gain = rgain_ref[...]
        for rows, o in zip(ret.chunks, outs):
            mu = jnp.mean(o, axis=-1, keepdims=True)
            oc = o - mu
            var = jnp.mean(oc * oc, axis=-1, keepdims=True)
            ret_ref[rows, :] = (oc * lax.rsqrt(var + EPS) * gain
                                * rgate[rows, :].astype(F32)).astype(BF16)

    for hd in heads:
        hd.cumsum_dots()
    project_hq_hf()
    for hd in heads:
        hd.elementwise()
    for hd in heads:
        hd.free_dots()
    ret.free_dots()
    project_hv_hg()
    hgrn_outputs()
    project_rq_rk()
    retention_outputs()
    project_rv_rg()


def _mixer(x1, norm_g, w_units, lb_logits, ret_gain, hgrn_gain, cos_sin, seq, side_f32):
    t, d = x1.shape
    tm = MIXER_TM
    tn = RET_HEAD_DIM
    dh = HGRN_HEAD_DIM
    width = RET_HEADS * RET_HEAD_DIM
    assert w_units.shape == (width // tn, 8, d, tn) and width == HGRN_HEADS * dh and tn % dh == 0
    assert t % tm == 0 and seq % tm == 0
    assert tm % RET_CHUNK == 0 and tm % HGRN_CHUNK == 0
    assert HGRN_CHUNK % BF16_TILE_ROWS == 0 and BF16_TILE_ROWS % HGRN_SUB == 0
    nj = width // tn
    blocks_per_seq = seq // tm
    n_units = (t // tm) * nj
    prod = lambda s: jnp.minimum(s, n_units - 1)
    cons = lambda s: jnp.maximum(s - 1, 0)
    const = lambda s: (0, 0)
    pos = lambda s: ((prod(s) // nj) % blocks_per_seq, 0)
    cons_col = lambda s: (0, cons(s) % nj)

    side_specs = []
    for w in side_f32:
        rows = next(r for r in range(BF16_TILE_ROWS, w.shape[0] + 1, BF16_TILE_ROWS)
                    if w.shape[0] % r == 0 and w.shape[0] // r <= n_units)
        side_specs.append(pl.BlockSpec(
            (rows, w.shape[1]), lambda s, last=w.shape[0] // rows - 1: (jnp.minimum(s, last), 0)))

    out_spec = pl.BlockSpec((tm, tn), lambda s: (cons(s) // nj, cons(s) % nj))
    out_bf = jax.ShapeDtypeStruct((t, width), BF16)
    outs = pl.pallas_call(
        functools.partial(_mixer_kernel, n_col_blocks=nj, blocks_per_seq=blocks_per_seq,
                          n_side=len(side_f32)),
        grid=(n_units + 1,),
        in_specs=[pl.BlockSpec((tm, d), lambda s: (prod(s) // nj, 0)), pl.BlockSpec((1, d), const),
                  pl.BlockSpec((tm, tn), pos),
                  pl.BlockSpec((1, tn), lambda s: (0, prod(s) % nj)),
                  pl.BlockSpec((1, tn), cons_col), pl.BlockSpec((1, tn), cons_col),
                  pl.BlockSpec((None, 8, d, tn), lambda s: (prod(s) % nj, 0, 0, 0))] + side_specs,
        out_specs=[out_spec, out_spec] + side_specs,
        out_shape=[out_bf, out_bf] + [jax.ShapeDtypeStruct(w.shape, BF16) for w in side_f32],
        scratch_shapes=[pltpu.VMEM((tm, d), BF16),
                        pltpu.VMEM((2, 8, tm, tn), BF16),
                        pltpu.VMEM((2, tm, tn), F32),
                        pltpu.VMEM((RET_HEADS, RET_HEAD_DIM, RET_HEAD_DIM), F32),
                        pltpu.VMEM((HGRN_HEADS, dh, dh), F32),
                        pltpu.VMEM((tn // dh, tm, dh), F32)],
        compiler_params=pltpu.CompilerParams(
            dimension_semantics=("arbitrary",), vmem_limit_bytes=VMEM_LIMIT_BYTES),
        name="mixer",
    )(x1, norm_g.reshape(1, d), cos_sin, lb_logits, ret_gain.reshape(1, width),
      hgrn_gain.reshape(1, width), w_units, *side_f32)
    return outs[0], outs[1], outs[2:]


@jax.jit
def kernel(x, ffn1_norm, ffn1_w_gate, ffn1_w_up, ffn1_w_down, mix_norm, w_in, ret_norm_g, hgrn_lb_logits, hgrn_norm_g, w_out, ffn2_norm, ffn2_w_gate, ffn2_w_up, ffn2_w_down, final_norm):
    batch, seq, d = x.shape
    assert ffn1_norm.shape[0] == 1, "single-layer stack"
    t = batch * seq
    bf = lambda w: w[0].astype(BF16)

    inv = np.power(ROPE_BASE, -np.arange(0, RET_HEAD_DIM, 2, dtype=np.float64) / RET_HEAD_DIM)
    ang = np.arange(seq, dtype=np.float64)[:, None] * inv[None, :]
    cos_sin = jnp.asarray(np.concatenate([np.cos(ang), np.sin(ang)], axis=1), dtype=F32)

    x0 = x.reshape(t, d)
    x1, w_units = _ffn(x0, ffn1_norm[0], bf(ffn1_w_gate), bf(ffn1_w_up), bf(ffn1_w_down),
                       name="ffn1", w_in_f32=w_in[0].astype(F32))
    side = [ffn2_w_gate[0].astype(F32), ffn2_w_up[0].astype(F32), ffn2_w_down[0].astype(F32),
            w_out[0].astype(F32)]
    ret, hgo, (wg2, wu2, wd2, wo) = _mixer(
        x1, mix_norm[0], w_units, hgrn_lb_logits.astype(F32),
        ret_norm_g[0].astype(F32), hgrn_norm_g[0].astype(F32), cos_sin, seq, side)
    x2 = _outproj(x1, ret, hgo, wo)
    out = _ffn_inner(x2, ffn2_norm[0], wg2, wu2, wd2, final_norm, name="ffn2")
    return out.reshape(batch, seq, d)
```

```python
import functools

import jax
import jax.numpy as jnp
import numpy as np
from jax import lax
from jax.experimental import pallas as pl
from jax.experimental.pallas import tpu as pltpu

F32 = jnp.float32
BF16 = jnp.bfloat16

RET_HEADS = 4
RET_HEAD_DIM = 256
HGRN_HEADS = 8
HGRN_HEAD_DIM = 128
ROPE_BASE = 10000.0
EPS = 1e-6
FFN_RESIDUAL_WEIGHT = 0.5

VMEM_LIMIT_BYTES = 56 * 1024 * 1024
BF16_TILE_ROWS = 16

FFN_TM = 1024
FFN_TF = 512
FFN_EDGE_SPLIT = 2
MIXER_TM = 512
OUTPROJ_TM = 512
RET_CHUNK = 256
HGRN_CHUNK = 64
HGRN_SUB = 8


def _dot(a, b):
    return jnp.dot(a, b, preferred_element_type=F32)


def _dot_nt(a, b):
    return lax.dot_general(a, b, (((1,), (1,)), ((), ())), preferred_element_type=F32)


def _dot_tn(a, b):
    return lax.dot_general(a, b, (((0,), (0,)), ((), ())), preferred_element_type=F32)


def _rmsnorm(x, g):
    return x * lax.rsqrt(jnp.mean(x * x, axis=-1, keepdims=True) + EPS) * g


def _silu(x):
    return x * jax.nn.sigmoid(x)


def _ffn_kernel(*refs, final_norm, cast_side, n_ff_steps):
    refs = list(refs)
    x_ref, g_ref, wg_ref, wu_ref, wd_ref = refs[:5]
    rest = refs[5:]
    fg_ref = rest.pop(0) if final_norm else None
    side_in_ref = rest.pop(0) if cast_side else None
    out_ref = rest.pop(0)
    side_out_ref = rest.pop(0) if cast_side else None
    (h_ref,) = rest

    j = pl.program_id(1)
    tm = x_ref.shape[0]
    part = tm // FFN_EDGE_SPLIT
    parts = [slice(p * part, (p + 1) * part) for p in range(FFN_EDGE_SPLIT)]

    def ride_along_cast():
        if cast_side:
            tn = side_out_ref.shape[2]
            for jj in range(side_out_ref.shape[0]):
                side_out_ref[jj] = side_in_ref[:, jj * tn:(jj + 1) * tn].astype(BF16)

    def ffn_slice(h):
        gate = _dot(h, wg_ref[...])
        up = _dot(h, wu_ref[...])
        act = (_silu(gate) * up * FFN_RESIDUAL_WEIGHT).astype(BF16)
        return _dot(act, wd_ref[...])

    @pl.when(j == 0)
    def _():
        ride_along_cast()
        for rows in parts:
            x = x_ref[rows, :]
            h = _rmsnorm(x, g_ref[...]).astype(BF16)
            h_ref[rows, :] = h
            out_ref[rows, :] = x + ffn_slice(h)

    last = n_ff_steps - 1 if final_norm else n_ff_steps

    @pl.when((j > 0) & (j < last))
    def _():
        ride_along_cast()
        out_ref[...] += ffn_slice(h_ref[...])

    if final_norm:
        @pl.when(j == last)
        def _():
            ride_along_cast()
            for rows in parts:
                y = out_ref[rows, :] + ffn_slice(h_ref[rows, :])
                out_ref[rows, :] = _rmsnorm(y, fg_ref[...])


def _ffn(x, norm_g, w_gate, w_up, w_down, *, name, final_g=None, w_in_f32=None):
    t, d = x.shape
    d_ff = w_gate.shape[1]
    tm, tf = FFN_TM, FFN_TF
    assert t % tm == 0 and d_ff % tf == 0
    n_blocks = t // tm
    n_ff_steps = d_ff // tf
    row = lambda i, j: (i, 0)
    const = lambda i, j: (0, 0)
    in_specs = [pl.BlockSpec((tm, d), row),
                pl.BlockSpec((1, d), const),
                pl.BlockSpec((d, tf), lambda i, j: (0, j)),
                pl.BlockSpec((d, tf), lambda i, j: (0, j)),
                pl.BlockSpec((tf, d), lambda i, j: (j, 0))]
    args = [x, norm_g.reshape(1, d), w_gate, w_up, w_down]
    out_specs = [pl.BlockSpec((tm, d), row)]
    out_shape = [jax.ShapeDtypeStruct((t, d), F32)]
    if final_g is not None:
        in_specs.append(pl.BlockSpec((1, d), const))
        args.append(final_g.reshape(1, d))
    if w_in_f32 is not None:
        n_groups = 8
        tn = RET_HEAD_DIM
        width = w_in_f32.shape[1] // n_groups
        nj = width // tn
        rows = d // n_blocks
        assert d % n_blocks == 0 and rows % BF16_TILE_ROWS == 0 and n_ff_steps >= n_groups
        group = lambda i, j: (i, jnp.minimum(j, n_groups - 1))
        in_specs.append(pl.BlockSpec((rows, width), group))
        args.append(w_in_f32)
        out_specs.append(pl.BlockSpec((nj, None, rows, tn),
                                      lambda i, j: (0, jnp.minimum(j, n_groups - 1), i, 0)))
        out_shape.append(jax.ShapeDtypeStruct((nj, n_groups, d, tn), BF16))
    kern = functools.partial(_ffn_kernel, final_norm=final_g is not None,
                             cast_side=w_in_f32 is not None, n_ff_steps=n_ff_steps)
    outs = pl.pallas_call(
        kern,
        grid=(n_blocks, n_ff_steps),
        in_specs=in_specs,
        out_specs=out_specs,
        out_shape=out_shape,
        scratch_shapes=[pltpu.VMEM((tm, d), BF16)],
        compiler_params=pltpu.CompilerParams(
            dimension_semantics=("parallel", "arbitrary"), vmem_limit_bytes=VMEM_LIMIT_BYTES),
        name=name,
    )(*args)
    return outs if w_in_f32 is not None else outs[0]


def _ffn_inner_kernel(x_hbm, g_ref, fg_ref, wg_hbm, wu_hbm, wd_hbm, out_hbm, h_ref, *, tm, tf):
    t, d = x_hbm.shape
    n_ff_steps = wg_hbm.shape[1] // tf

    def step(idx, x_ref, wg_ref, wu_ref, wd_ref, out_ref):
        j = idx[1]

        @pl.when(j == 0)
        def _():
            x = x_ref[...]
            out_ref[...] = x
            h_ref[...] = _rmsnorm(x, g_ref[...]).astype(BF16)

        h = h_ref[...]
        gate = _dot(h, wg_ref[...])
        up = _dot(h, wu_ref[...])
        act = (_silu(gate) * up * FFN_RESIDUAL_WEIGHT).astype(BF16)
        out_ref[...] += _dot(act, wd_ref[...])

        @pl.when(j == n_ff_steps - 1)
        def _():
            out_ref[...] = _rmsnorm(out_ref[...], fg_ref[...])

    pltpu.emit_pipeline(
        step,
        grid=(t // tm, n_ff_steps),
        in_specs=[pl.BlockSpec((tm, d), lambda i, j: (i, 0)),
                  pl.BlockSpec((d, tf), lambda i, j: (0, j)),
                  pl.BlockSpec((d, tf), lambda i, j: (0, j)),
                  pl.BlockSpec((tf, d), lambda i, j: (j, 0))],
        out_specs=[pl.BlockSpec((tm, d), lambda i, j: (i, 0))],
        _explicit_indices=True,
    )(x_hbm, wg_hbm, wu_hbm, wd_hbm, out_hbm)


def _ffn_inner(x, norm_g, w_gate, w_up, w_down, final_g, *, name):
    t, d = x.shape
    tm, tf = FFN_TM, FFN_TF
    assert t % tm == 0 and w_gate.shape[1] % tf == 0
    const = lambda i: (0, 0)
    hbm = pl.BlockSpec(memory_space=pl.ANY)
    return pl.pallas_call(
        functools.partial(_ffn_inner_kernel, tm=tm, tf=tf),
        grid=(1,),
        in_specs=[hbm, pl.BlockSpec((1, d), const), pl.BlockSpec((1, d), const), hbm, hbm, hbm],
        out_specs=hbm,
        out_shape=jax.ShapeDtypeStruct((t, d), F32),
        scratch_shapes=[pltpu.VMEM((tm, d), BF16)],
        compiler_params=pltpu.CompilerParams(
            dimension_semantics=("arbitrary",), vmem_limit_bytes=VMEM_LIMIT_BYTES),
        name=name,
    )(x, norm_g.reshape(1, d), final_g.reshape(1, d), w_gate, w_up, w_down)


def _outproj_kernel(x_ref, ret_ref, hgo_ref, wo_r_ref, wo_h_ref, out_ref):
    out_ref[...] = (x_ref[...] + _dot(ret_ref[...], wo_r_ref[...])
                    + _dot(hgo_ref[...], wo_h_ref[...]))


def _outproj(x1, ret, hgo, w_out):
    t, d = x1.shape
    tm = OUTPROJ_TM
    half = ret.shape[1]
    assert t % tm == 0 and hgo.shape[1] == half and w_out.shape == (2 * half, d)
    row = lambda i: (i, 0)
    return pl.pallas_call(
        _outproj_kernel,
        grid=(t // tm,),
        in_specs=[pl.BlockSpec((tm, d), row),
                  pl.BlockSpec((tm, half), row), pl.BlockSpec((tm, half), row),
                  pl.BlockSpec((half, d), lambda i: (0, 0)), pl.BlockSpec((half, d), lambda i: (1, 0))],
        out_specs=pl.BlockSpec((tm, d), row),
        out_shape=jax.ShapeDtypeStruct((t, d), F32),
        compiler_params=pltpu.CompilerParams(
            dimension_semantics=("parallel",), vmem_limit_bytes=VMEM_LIMIT_BYTES),
        name="outproj",
    )(x1, ret, hgo, w_out, w_out)


class _RetentionHead:
    def __init__(self, q, k, v, head, c):
        tb, dk = q.shape
        self.c = c
        self.chunks = [slice(n * c, (n + 1) * c) for n in range(tb // c)]
        self.q, self.k, self.v = q, k, v

        def log_gamma(shape):
            return jnp.log(1.0 - jnp.exp2(-5.0 - jnp.full(shape, head, jnp.int32).astype(F32)))

        row = lax.broadcasted_iota(jnp.int32, (c, dk), 0).astype(F32)
        rel = (lax.broadcasted_iota(jnp.int32, (c, c), 0) -
               lax.broadcasted_iota(jnp.int32, (c, c), 1)).astype(F32)
        self.decay = jnp.where(rel >= 0, jnp.exp(log_gamma((c, c)) * jnp.maximum(rel, 0.0)), 0.0)
        self.q_dec = jnp.exp(log_gamma((c, dk)) * (row + 1.0))
        self.k_dec = jnp.exp(log_gamma((c, dk)) * (c - 1.0 - row))
        self.g_chunk = jnp.exp(log_gamma((1, dk)) * float(c))

    def free_dots(self):
        self.scores = [_dot_nt(self.q[r], self.k[r]) for r in self.chunks]
        self.updates = [_dot_tn((self.k[r].astype(F32) * self.k_dec).astype(BF16), self.v[r])
                        for r in self.chunks]

    def outputs(self, state):
        states = []
        for upd in self.updates:
            states.append(state.astype(BF16))
            state = self.g_chunk * state + upd
        outs = []
        for r, s, st in zip(self.chunks, self.scores, states):
            inner = _dot((s * self.decay).astype(BF16), self.v[r])
            outs.append(inner + _dot(self.q[r], st) * self.q_dec)
        return outs, state


class _HgrnConsts:
    def __init__(self, c, sub):
        d = HGRN_HEAD_DIM
        pair = BF16_TILE_ROWS
        self.c, self.sub, self.pair = c, sub, pair
        self.tri = (lax.broadcasted_iota(jnp.int32, (c, c), 0) >=
                    lax.broadcasted_iota(jnp.int32, (c, c), 1)).astype(BF16)
        self.pair_lane = lax.broadcasted_iota(jnp.int32, (pair, d), 1) % sub
        a_row = lax.broadcasted_iota(jnp.int32, (c, c), 0)
        a_col = lax.broadcasted_iota(jnp.int32, (c, c), 1)
        self.diag_mask = ((a_row // sub) == (a_col // sub)) & (a_col <= a_row)
        self.halves = []
        h = c // 2
        while h >= sub:
            self.halves.append(h)
            h //= 2
        self.level_masks = [(a_row // (2 * h)) == (a_col // (2 * h)) for h in self.halves]


class _HgrnHead:
    def __init__(self, q, k, v, lf, cp_ref, consts):
        self.q, self.k, self.v, self.lf, self.cp_ref, self.cs = q, k, v, lf, cp_ref, consts
        c = consts.c
        self.chunks = [slice(n * c, (n + 1) * c) for n in range(q.shape[0] // c)]

    def cumsum_dots(self):
        d = HGRN_HEAD_DIM
        self.cums = []
        for rows in self.chunks:
            lf = self.lf[rows]
            lf1 = lf.astype(BF16)
            r1 = lf - lf1.astype(F32)
            lf2 = r1.astype(BF16)
            lf3 = (r1 - lf2.astype(F32)).astype(BF16)
            cum3 = _dot(self.cs.tri, jnp.concatenate([lf1, lf2, lf3], axis=1))
            self.cums.append(cum3[:, :d] + cum3[:, d:2 * d] + cum3[:, 2 * d:])

    def elementwise(self):
        cs, cp_ref = self.cs, self.cp_ref
        c, sub, pair, d = cs.c, cs.sub, cs.pair, HGRN_HEAD_DIM
        self.level_ops, self.a_diags, self.q_ins, self.k_ends, self.decays = [], [], [], [], []
        for rows, cum in zip(self.chunks, self.cums):
            r0 = rows.start
            q = self.q[rows].astype(F32)
            k = self.k[rows].astype(F32)
            cp = cum - jnp.log2(k)
            cp_ref[rows, :] = cp
            total = cum[c - 1:c, :]
            self.q_ins.append((q * jnp.exp2(cum)).astype(BF16))
            self.k_ends.append(jnp.exp2(total - cp).astype(BF16))
            self.decays.append(jnp.exp2(total))

            ops = []
            for h in cs.halves:
                q_parts, k_parts = [], []
                for p0 in range(0, c, 2 * h):
                    lo, up = slice(p0, p0 + h), slice(p0 + h, p0 + 2 * h)
                    c_b = cum[p0 + h - 1:p0 + h, :]
                    q_parts.append(q[up] * jnp.exp2(cum[up] - c_b))
                    k_parts += [jnp.exp2(c_b - cp[lo]), jnp.zeros((h, d), F32)]
                ops.append((jnp.concatenate(q_parts, axis=0).astype(BF16),
                            jnp.concatenate(k_parts, axis=0).astype(BF16)))
            self.level_ops.append(ops)

            pairs = []
            for i0 in range(0, c, pair):
                q_i = q[i0:i0 + pair]
                c_i = cum[i0:i0 + pair]
                a_pair = jnp.zeros((pair, d), F32)
                for j in range(sub):
                    srcs = [cp_ref[r0 + i0 + s + j:r0 + i0 + s + j + 1, :]
                            for s in range(0, pair, sub)]
                    if len(srcs) == 1:
                        cp_j = srcs[0]
                    else:
                        cp_j = jnp.concatenate([jnp.broadcast_to(s, (sub, d)) for s in srcs], axis=0)
                    s_j = jnp.sum(q_i * jnp.exp2(c_i - cp_j), axis=-1, keepdims=True)
                    a_pair = jnp.where(cs.pair_lane == j, s_j, a_pair)
                pairs.append(a_pair)
            self.a_diags.append(jnp.concatenate(pairs, axis=0)[:, :c])

    def free_dots(self):
        self.lows = [[_dot_nt(q_l, k_l) for q_l, k_l in ops] for ops in self.level_ops]
        self.updates = [_dot_tn(self.v[rows], k_end) for rows, k_end in zip(self.chunks, self.k_ends)]

    def outputs(self, state):
        cs = self.cs
        states = []
        for decay, upd in zip(self.decays, self.updates):
            states.append(jnp.transpose(state).astype(BF16))
            state = decay * state + upd
        outs = []
        for n, rows in enumerate(self.chunks):
            attn = jnp.where(cs.diag_mask, self.a_diags[n], 0.0)
            for h, mask, low in zip(cs.halves, cs.level_masks, self.lows[n]):
                parts = []
                for g in range(cs.c // (2 * h)):
                    parts += [jnp.zeros((h, cs.c), F32), low[g * h:(g + 1) * h]]
                low = jnp.concatenate(parts, axis=0)
                attn = attn + (low if 2 * h == cs.c else jnp.where(mask, low, 0.0))
            lhs = jnp.concatenate([self.q_ins[n], attn.astype(BF16)], axis=1)
            rhs = jnp.concatenate([states[n], self.v[rows]], axis=0)
            outs.append(_dot(lhs, rhs))
        return outs, state


_RQ, _RK, _RV, _RG, _HQ, _HK, _HV, _HG = range(8)


def _mixer_kernel(x_ref, g_ref, cos_sin_ref, lbl_ref, rgain_ref, hgain_ref,
                  w_ref, *refs, n_col_blocks, blocks_per_seq, n_side):
    side_in = refs[:n_side]
    ret_ref, hgo_ref = refs[n_side:n_side + 2]
    side_out = refs[n_side + 2:2 * n_side + 2]
    h_ref, act_ref, lf_ref, ret_state_ref, hgrn_state_ref, cp_ref = refs[2 * n_side + 2:]

    s = pl.program_id(0)
    nj = n_col_blocks
    dk, dh = RET_HEAD_DIM, HGRN_HEAD_DIM
    heads_per_step = dk // dh
    cons = jnp.maximum(s - 1, 0)
    cj = cons % nj

    @pl.when(s == 0)
    def _():
        act_ref[...] = jnp.zeros_like(act_ref)
        lf_ref[...] = jnp.zeros_like(lf_ref)

    @pl.when(s % nj == 0)
    def _():
        h_ref[...] = _rmsnorm(x_ref[...], g_ref[...]).astype(BF16)

    @pl.when((cons // nj) % blocks_per_seq == 0)
    def _():
        ret_state_ref[cj] = jnp.zeros((dk, dk), F32)
        for a in range(heads_per_step):
            hgrn_state_ref[heads_per_step * cj + a] = jnp.zeros((dh, dh), F32)

    for i_ref, o_ref in zip(side_in, side_out):
        o_ref[...] = i_ref[...].astype(BF16)

    h = h_ref[...]
    half = dk // 2
    cos = cos_sin_ref[:, :half]
    sin = cos_sin_ref[:, half:]
    dst = act_ref.at[s % 2]
    src = act_ref.at[(s + 1) % 2]
    src_lf = lf_ref.at[(s + 1) % 2]

    def rope(p):
        x1, x2 = p[:, :half], p[:, half:]
        return jnp.concatenate([x1 * cos - x2 * sin, x2 * cos + x1 * sin], axis=-1)

    consts = _HgrnConsts(HGRN_CHUNK, HGRN_SUB)
    hq, hk, hv, hlf = src[_HQ], src[_HK], src[_HV], src_lf[...]
    heads = []
    for a in range(heads_per_step):
        cols = slice(a * dh, (a + 1) * dh)
        heads.append(_HgrnHead(hq[:, cols], hk[:, cols], hv[:, cols], hlf[:, cols],
                               cp_ref.at[a], consts))
    ret = _RetentionHead(src[_RQ], src[_RK], src[_RV], cj, RET_CHUNK)

    def project_pair(first_group):
        return _dot(h, w_ref[first_group]), _dot(h, w_ref[first_group + 1])

    def project_hq_hf():
        p_hq, z = project_pair(_HQ)
        dst[_HQ] = _silu(p_hq).astype(BF16)
        lbl = lbl_ref[...]
        m = jnp.maximum(lbl, 0.0)
        e_l = jnp.exp(lbl - m)
        lb = e_l / (e_l + jnp.exp(-m))
        dst[_HK] = ((1.0 - lb) * jax.nn.sigmoid(-z)).astype(BF16)
        lf_ref[s % 2] = jnp.log2(lb + (1.0 - lb) * jax.nn.sigmoid(z))

    def project_hv_hg():
        p_hv, p_hg = project_pair(_HV)
        dst[_HV] = p_hv.astype(BF16)
        dst[_HG] = _silu(p_hg).astype(BF16)

    def project_rq_rk():
        p_rq, p_rk = project_pair(_RQ)
        dst[_RQ] = (rope(p_rq) * (dk ** -0.5)).astype(BF16)
        dst[_RK] = rope(p_rk).astype(BF16)

    def project_rv_rg():
        p_rv, p_rg = project_pair(_RV)
        dst[_RV] = p_rv.astype(BF16)
        dst[_RG] = _silu(p_rg).astype(BF16)

    def hgrn_outputs():
        hgate = src[_HG]
        for a, hd in enumerate(heads):
            cols = slice(a * dh, (a + 1) * dh)
            outs, state = hd.outputs(hgrn_state_ref[heads_per_step * cj + a])
            hgrn_state_ref[heads_per_step * cj + a] = state
            gain = hgain_ref[:, cols]
            for rows, o in zip(hd.chunks, outs):
                y = o * lax.rsqrt(jnp.mean(o * o, axis=-1, keepdims=True) + EPS)
                hgo_ref[rows, cols] = (y * gain * hgate[rows, cols].astype(F32)).astype(BF16)

    def retention_outputs():
        rgate = src[_RG]
        outs, state = ret.outputs(ret_state_ref[cj])
        ret_state_ref[cj] = state
        gain = rgain_ref[...]
        for rows, o in zip(ret.chunks, outs):
            mu = jnp.mean(o, axis=-1, keepdims=True)
            oc = o - mu
            var = jnp.mean(oc * oc, axis=-1, keepdims=True)
            ret_ref[rows, :] = (oc * lax.rsqrt(var + EPS) * gain
                                * rgate[rows, :].astype(F32)).astype(BF16)

    for hd in heads:
        hd.cumsum_dots()
    project_hq_hf()
    for hd in heads:
        hd.elementwise()
    for hd in heads:
        hd.free_dots()
    ret.free_dots()
    project_hv_hg()
    hgrn_outputs()
    project_rq_rk()
    retention_outputs()
    project_rv_rg()


def _mixer(x1, norm_g, w_units, lb_logits, ret_gain, hgrn_gain, cos_sin, seq, side_f32):
    t, d = x1.shape
    tm = MIXER_TM
    tn = RET_HEAD_DIM
    dh = HGRN_HEAD_DIM
    width = RET_HEADS * RET_HEAD_DIM
    assert w_units.shape == (width // tn, 8, d, tn) and width == HGRN_HEADS * dh and tn % dh == 0
    assert t % tm == 0 and seq % tm == 0
    assert tm % RET_CHUNK == 0 and tm % HGRN_CHUNK == 0
    assert HGRN_CHUNK % BF16_TILE_ROWS == 0 and BF16_TILE_ROWS % HGRN_SUB == 0
    nj = width // tn
    blocks_per_seq = seq // tm
    n_units = (t // tm) * nj
    prod = lambda s: jnp.minimum(s, n_units - 1)
    cons = lambda s: jnp.maximum(s - 1, 0)
    const = lambda s: (0, 0)
    pos = lambda s: ((prod(s) // nj) % blocks_per_seq, 0)
    cons_col = lambda s: (0, cons(s) % nj)

    side_specs = []
    for w in side_f32:
        rows = next(r for r in range(BF16_TILE_ROWS, w.shape[0] + 1, BF16_TILE_ROWS)
                    if w.shape[0] % r == 0 and w.shape[0] // r <= n_units)
        side_specs.append(pl.BlockSpec(
            (rows, w.shape[1]), lambda s, last=w.shape[0] // rows - 1: (jnp.minimum(s, last), 0)))

    out_spec = pl.BlockSpec((tm, tn), lambda s: (cons(s) // nj, cons(s) % nj))
    out_bf = jax.ShapeDtypeStruct((t, width), BF16)
    outs = pl.pallas_call(
        functools.partial(_mixer_kernel, n_col_blocks=nj, blocks_per_seq=blocks_per_seq,
                          n_side=len(side_f32)),
        grid=(n_units + 1,),
        in_specs=[pl.BlockSpec((tm, d), lambda s: (prod(s) // nj, 0)), pl.BlockSpec((1, d), const),
                  pl.BlockSpec((tm, tn), pos),
                  pl.BlockSpec((1, tn), lambda s: (0, prod(s) % nj)),
                  pl.BlockSpec((1, tn), cons_col), pl.BlockSpec((1, tn), cons_col),
                  pl.BlockSpec((None, 8, d, tn), lambda s: (prod(s) % nj, 0, 0, 0))] + side_specs,
        out_specs=[out_spec, out_spec] + side_specs,
        out_shape=[out_bf, out_bf] + [jax.ShapeDtypeStruct(w.shape, BF16) for w in side_f32],
        scratch_shapes=[pltpu.VMEM((tm, d), BF16),
                        pltpu.VMEM((2, 8, tm, tn), BF16),
                        pltpu.VMEM((2, tm, tn), F32),
                        pltpu.VMEM((RET_HEADS, RET_HEAD_DIM, RET_HEAD_DIM), F32),
                        pltpu.VMEM((HGRN_HEADS, dh, dh), F32),
                        pltpu.VMEM((tn // dh, tm, dh), F32)],
        compiler_params=pltpu.CompilerParams(
            dimension_semantics=("arbitrary",), vmem_limit_bytes=VMEM_LIMIT_BYTES),
        name="mixer",
    )(x1, norm_g.reshape(1, d), cos_sin, lb_logits, ret_gain.reshape(1, width),
      hgrn_gain.reshape(1, width), w_units, *side_f32)
    return outs[0], outs[1], outs[2:]


@jax.jit
def kernel(x, ffn1_norm, ffn1_w_gate, ffn1_w_up, ffn1_w_down, mix_norm, w_in, ret_norm_g, hgrn_lb_logits, hgrn_norm_g, w_out, ffn2_norm, ffn2_w_gate, ffn2_w_up, ffn2_w_down, final_norm):
    batch, seq, d = x.shape
    assert ffn1_norm.shape[0] == 1, "single-layer stack"
    t = batch * seq
    bf = lambda w: w[0].astype(BF16)

    inv = np.power(ROPE_BASE, -np.arange(0, RET_HEAD_DIM, 2, dtype=np.float64) / RET_HEAD_DIM)
    ang = np.arange(seq, dtype=np.float64)[:, None] * inv[None, :]
    cos_sin = jnp.asarray(np.concatenate([np.cos(ang), np.sin(ang)], axis=1), dtype=F32)

    x0 = x.reshape(t, d)
    x1, w_units = _ffn(x0, ffn1_norm[0], bf(ffn1_w_gate), bf(ffn1_w_up), bf(ffn1_w_down),
                       name="ffn1", w_in_f32=w_in[0].astype(F32))
    side = [ffn2_w_gate[0].astype(F32), ffn2_w_up[0].astype(F32), ffn2_w_down[0].astype(F32),
            w_out[0].astype(F32)]
    ret, hgo, (wg2, wu2, wd2, wo) = _mixer(
        x1, mix_norm[0], w_units, hgrn_lb_logits.astype(F32),
        ret_norm_g[0].astype(F32), hgrn_norm_g[0].astype(F32), cos_sin, seq, side)
    x2 = _outproj(x1, ret, hgo, wo)
    out = _ffn_inner(x2, ffn2_norm[0], wg2, wu2, wd2, final_norm, name="ffn2")
    return out.reshape(batch, seq, d)
```

```python
import functools

import jax
import jax.numpy as jnp
import numpy as np
from jax import lax
from jax.experimental import pallas as pl
from jax.experimental.pallas import tpu as pltpu

F32 = jnp.float32
BF16 = jnp.bfloat16

RET_HEADS = 4
RET_HEAD_DIM = 256
HGRN_HEADS = 8
HGRN_HEAD_DIM = 128
ROPE_BASE = 10000.0
EPS = 1e-6
FFN_RESIDUAL_WEIGHT = 0.5

VMEM_LIMIT_BYTES = 56 * 1024 * 1024
BF16_TILE_ROWS = 16

FFN_TM = 1024
FFN_TF = 512
FFN_EDGE_SPLIT = 2
MIXER_TM = 512
OUTPROJ_TM = 512
RET_CHUNK = 256
HGRN_CHUNK = 64
HGRN_SUB = 8


def _dot(a, b):
    return jnp.dot(a, b, preferred_element_type=F32)


def _dot_nt(a, b):
    return lax.dot_general(a, b, (((1,), (1,)), ((), ())), preferred_element_type=F32)


def _dot_tn(a, b):
    return lax.dot_general(a, b, (((0,), (0,)), ((), ())), preferred_element_type=F32)


def _rmsnorm(x, g):
    return x * lax.rsqrt(jnp.mean(x * x, axis=-1, keepdims=True) + EPS) * g


def _silu(x):
    return x * jax.nn.sigmoid(x)


def _ffn_kernel(*refs, final_norm, cast_side, n_ff_steps):
    refs = list(refs)
    x_ref, g_ref, wg_ref, wu_ref, wd_ref = refs[:5]
    rest = refs[5:]
    fg_ref = rest.pop(0) if final_norm else None
    side_in_ref = rest.pop(0) if cast_side else None
    out_ref = rest.pop(0)
    side_out_ref = rest.pop(0) if cast_side else None
    (h_ref,) = rest

    j = pl.program_id(1)
    tm = x_ref.shape[0]
    part = tm // FFN_EDGE_SPLIT
    parts = [slice(p * part, (p + 1) * part) for p in range(FFN_EDGE_SPLIT)]

    def ride_along_cast():
        if cast_side:
            tn = side_out_ref.shape[2]
            for jj in range(side_out_ref.shape[0]):
                side_out_ref[jj] = side_in_ref[:, jj * tn:(jj + 1) * tn].astype(BF16)

    def ffn_slice(h):
        gate = _dot(h, wg_ref[...])
        up = _dot(h, wu_ref[...])
        act = (_silu(gate) * up * FFN_RESIDUAL_WEIGHT).astype(BF16)
        return _dot(act, wd_ref[...])

    @pl.when(j == 0)
    def _():
        ride_along_cast()
        for rows in parts:
            x = x_ref[rows, :]
            h = _rmsnorm(x, g_ref[...]).astype(BF16)
            h_ref[rows, :] = h
            out_ref[rows, :] = x + ffn_slice(h)

    last = n_ff_steps - 1 if final_norm else n_ff_steps

    @pl.when((j > 0) & (j < last))
    def _():
        ride_along_cast()
        out_ref[...] += ffn_slice(h_ref[...])

    if final_norm:
        @pl.when(j == last)
        def _():
            ride_along_cast()
            for rows in parts:
                y = out_ref[rows, :] + ffn_slice(h_ref[rows, :])
                out_ref[rows, :] = _rmsnorm(y, fg_ref[...])


def _ffn(x, norm_g, w_gate, w_up, w_down, *, name, final_g=None, w_in_f32=None):
    t, d = x.shape
    d_ff = w_gate.shape[1]
    tm, tf = FFN_TM, FFN_TF
    assert t % tm == 0 and d_ff % tf == 0
    n_blocks = t // tm
    n_ff_steps = d_ff // tf
    row = lambda i, j: (i, 0)
    const = lambda i, j: (0, 0)
    in_specs = [pl.BlockSpec((tm, d), row),
                pl.BlockSpec((1, d), const),
                pl.BlockSpec((d, tf), lambda i, j: (0, j)),
                pl.BlockSpec((d, tf), lambda i, j: (0, j)),
                pl.BlockSpec((tf, d), lambda i, j: (j, 0))]
    args = [x, norm_g.reshape(1, d), w_gate, w_up, w_down]
    out_specs = [pl.BlockSpec((tm, d), row)]
    out_shape = [jax.ShapeDtypeStruct((t, d), F32)]
    if final_g is not None:
        in_specs.append(pl.BlockSpec((1, d), const))
        args.append(final_g.reshape(1, d))
    if w_in_f32 is not None:
        n_groups = 8
        tn = RET_HEAD_DIM
        width = w_in_f32.shape[1] // n_groups
        nj = width // tn
        rows = d // n_blocks
        assert d % n_blocks == 0 and rows % BF16_TILE_ROWS == 0 and n_ff_steps >= n_groups
        group = lambda i, j: (i, jnp.minimum(j, n_groups - 1))
        in_specs.append(pl.BlockSpec((rows, width), group))
        args.append(w_in_f32)
        out_specs.append(pl.BlockSpec((nj, None, rows, tn),
                                      lambda i, j: (0, jnp.minimum(j, n_groups - 1), i, 0)))
        out_shape.append(jax.ShapeDtypeStruct((nj, n_groups, d, tn), BF16))
    kern = functools.partial(_ffn_kernel, final_norm=final_g is not None,
                             cast_side=w_in_f32 is not None, n_ff_steps=n_ff_steps)
    outs = pl.pallas_call(
        kern,
        grid=(n_blocks, n_ff_steps),
        in_specs=in_specs,
        out_specs=out_specs,
        out_shape=out_shape,
        scratch_shapes=[pltpu.VMEM((tm, d), BF16)],
        compiler_params=pltpu.CompilerParams(
            dimension_semantics=("parallel", "arbitrary"), vmem_limit_bytes=VMEM_LIMIT_BYTES),
        name=name,
    )(*args)
    return outs if w_in_f32 is not None else outs[0]


def _outproj_kernel(x_ref, ret_ref, hgo_ref, wo_r_ref, wo_h_ref, out_ref):
    out_ref[...] = (x_ref[...] + _dot(ret_ref[...], wo_r_ref[...])
                    + _dot(hgo_ref[...], wo_h_ref[...]))


def _outproj(x1, ret, hgo, w_out):
    t, d = x1.shape
    tm = OUTPROJ_TM
    half = ret.shape[1]
    assert t % tm == 0 and hgo.shape[1] == half and w_out.shape == (2 * half, d)
    tm, tn = 2 * OUTPROJ_TM, d // 2
    row = lambda j, i: (i, 0)
    tile = lambda j, i: (i, j)
    return pl.pallas_call(
        _outproj_kernel,
        grid=(d // tn, t // tm),
        in_specs=[pl.BlockSpec((tm, tn), tile),
                  pl.BlockSpec((tm, half), row), pl.BlockSpec((tm, half), row),
                  pl.BlockSpec((half, tn), lambda j, i: (0, j)),
                  pl.BlockSpec((half, tn), lambda j, i: (1, j))],
        out_specs=pl.BlockSpec((tm, tn), tile),
        out_shape=jax.ShapeDtypeStruct((t, d), F32),
        compiler_params=pltpu.CompilerParams(
            dimension_semantics=("parallel", "parallel"), vmem_limit_bytes=VMEM_LIMIT_BYTES),
        name="outproj",
    )(x1, ret, hgo, w_out, w_out)


class _RetentionHead:
    def __init__(self, q, k, v, head, c):
        tb, dk = q.shape
        self.c = c
        self.chunks = [slice(n * c, (n + 1) * c) for n in range(tb // c)]
        self.q, self.k, self.v = q, k, v

        def log_gamma(shape):
            return jnp.log(1.0 - jnp.exp2(-5.0 - jnp.full(shape, head, jnp.int32).astype(F32)))

        row = lax.broadcasted_iota(jnp.int32, (c, dk), 0).astype(F32)
        rel = (lax.broadcasted_iota(jnp.int32, (c, c), 0) -
               lax.broadcasted_iota(jnp.int32, (c, c), 1)).astype(F32)
        self.decay = jnp.where(rel >= 0, jnp.exp(log_gamma((c, c)) * jnp.maximum(rel, 0.0)), 0.0)
        self.q_dec = jnp.exp(log_gamma((c, dk)) * (row + 1.0))
        self.k_dec = jnp.exp(log_gamma((c, dk)) * (c - 1.0 - row))
        self.g_chunk = jnp.exp(log_gamma((1, dk)) * float(c))

    def free_dots(self):
        self.scores = [_dot_nt(self.q[r], self.k[r]) for r in self.chunks]
        self.updates = [_dot_tn((self.k[r].astype(F32) * self.k_dec).astype(BF16), self.v[r])
                        for r in self.chunks]

    def outputs(self, state):
        states = []
        for upd in self.updates:
            states.append(state.astype(BF16))
            state = self.g_chunk * state + upd
        outs = []
        for r, s, st in zip(self.chunks, self.scores, states):
            inner = _dot((s * self.decay).astype(BF16), self.v[r])
            outs.append(inner + _dot(self.q[r], st) * self.q_dec)
        return outs, state


class _HgrnConsts:
    def __init__(self, c, sub):
        d = HGRN_HEAD_DIM
        pair = BF16_TILE_ROWS
        self.c, self.sub, self.pair = c, sub, pair
        self.tri = (lax.broadcasted_iota(jnp.int32, (c, c), 0) >=
                    lax.broadcasted_iota(jnp.int32, (c, c), 1)).astype(BF16)
        self.pair_lane = lax.broadcasted_iota(jnp.int32, (pair, d), 1) % sub
        a_row = lax.broadcasted_iota(jnp.int32, (c, c), 0)
        a_col = lax.broadcasted_iota(jnp.int32, (c, c), 1)
        self.diag_mask = ((a_row // sub) == (a_col // sub)) & (a_col <= a_row)
        self.halves = []
        h = c // 2
        while h >= sub:
            self.halves.append(h)
            h //= 2
        self.level_masks = [(a_row // (2 * h)) == (a_col // (2 * h)) for h in self.halves]


class _HgrnHead:
    def __init__(self, q, k, v, lf, cp_ref, consts):
        self.q, self.k, self.v, self.lf, self.cp_ref, self.cs = q, k, v, lf, cp_ref, consts
        c = consts.c
        self.chunks = [slice(n * c, (n + 1) * c) for n in range(q.shape[0] // c)]

    def cumsum_dots(self):
        d = HGRN_HEAD_DIM
        self.cums = []
        for rows in self.chunks:
            lf = self.lf[rows]
            lf1 = lf.astype(BF16)
            r1 = lf - lf1.astype(F32)
            lf2 = r1.astype(BF16)
            lf3 = (r1 - lf2.astype(F32)).astype(BF16)
            cum3 = _dot(self.cs.tri, jnp.concatenate([lf1, lf2, lf3], axis=1))
            self.cums.append(cum3[:, :d] + cum3[:, d:2 * d] + cum3[:, 2 * d:])

    def elementwise(self):
        cs, cp_ref = self.cs, self.cp_ref
        c, sub, pair, d = cs.c, cs.sub, cs.pair, HGRN_HEAD_DIM
        self.level_ops, self.a_diags, self.q_ins, self.k_ends, self.decays = [], [], [], [], []
        for rows, cum in zip(self.chunks, self.cums):
            r0 = rows.start
            q = self.q[rows].astype(F32)
            k = self.k[rows].astype(F32)
            cp = cum - jnp.log2(k)
            cp_ref[rows, :] = cp
            total = cum[c - 1:c, :]
            self.q_ins.append((q * jnp.exp2(cum)).astype(BF16))
            self.k_ends.append(jnp.exp2(total - cp).astype(BF16))
            self.decays.append(jnp.exp2(total))

            ops = []
            for h in cs.halves:
                q_parts, k_parts = [], []
                for p0 in range(0, c, 2 * h):
                    lo, up = slice(p0, p0 + h), slice(p0 + h, p0 + 2 * h)
                    c_b = cum[p0 + h - 1:p0 + h, :]
                    q_parts.append(q[up] * jnp.exp2(cum[up] - c_b))
                    k_parts += [jnp.exp2(c_b - cp[lo]), jnp.zeros((h, d), F32)]
                ops.append((jnp.concatenate(q_parts, axis=0).astype(BF16),
                            jnp.concatenate(k_parts, axis=0).astype(BF16)))
            self.level_ops.append(ops)

            pairs = []
            for i0 in range(0, c, pair):
                q_i = q[i0:i0 + pair]
                c_i = cum[i0:i0 + pair]
                a_pair = jnp.zeros((pair, d), F32)
                for j in range(sub):
                    srcs = [cp_ref[r0 + i0 + s + j:r0 + i0 + s + j + 1, :]
                            for s in range(0, pair, sub)]
                    if len(srcs) == 1:
                        cp_j = srcs[0]
                    else:
                        cp_j = jnp.concatenate([jnp.broadcast_to(s, (sub, d)) for s in srcs], axis=0)
                    s_j = jnp.sum(q_i * jnp.exp2(c_i - cp_j), axis=-1, keepdims=True)
                    a_pair = jnp.where(cs.pair_lane == j, s_j, a_pair)
                pairs.append(a_pair)
            self.a_diags.append(jnp.concatenate(pairs, axis=0)[:, :c])

    def free_dots(self):
        self.lows = [[_dot_nt(q_l, k_l) for q_l, k_l in ops] for ops in self.level_ops]
        self.updates = [_dot_tn(self.v[rows], k_end) for rows, k_end in zip(self.chunks, self.k_ends)]

    def outputs(self, state):
        cs = self.cs
        states = []
        for decay, upd in zip(self.decays, self.updates):
            states.append(jnp.transpose(state).astype(BF16))
            state = decay * state + upd
        outs = []
        for n, rows in enumerate(self.chunks):
            attn = jnp.where(cs.diag_mask, self.a_diags[n], 0.0)
            for h, mask, low in zip(cs.halves, cs.level_masks, self.lows[n]):
                parts = []
                for g in range(cs.c // (2 * h)):
                    parts += [jnp.zeros((h, cs.c), F32), low[g * h:(g + 1) * h]]
                low = jnp.concatenate(parts, axis=0)
                attn = attn + (low if 2 * h == cs.c else jnp.where(mask, low, 0.0))
            lhs = jnp.concatenate([self.q_ins[n], attn.astype(BF16)], axis=1)
            rhs = jnp.concatenate([states[n], self.v[rows]], axis=0)
            outs.append(_dot(lhs, rhs))
        return outs, state


_RQ, _RK, _RV, _RG, _HQ, _HK, _HV, _HG = range(8)


def _mixer_kernel(x_ref, g_ref, cos_sin_ref, lbl_ref, rgain_ref, hgain_ref,
                  w_ref, *refs, n_col_blocks, blocks_per_seq, n_side):
    side_in = refs[:n_side]
    ret_ref, hgo_ref = refs[n_side:n_side + 2]
    side_out = refs[n_side + 2:2 * n_side + 2]
    h_ref, act_ref, lf_ref, ret_state_ref, hgrn_state_ref, cp_ref = refs[2 * n_side + 2:]

    s = pl.program_id(0)
    nj = n_col_blocks
    dk, dh = RET_HEAD_DIM, HGRN_HEAD_DIM
    heads_per_step = dk // dh
    cons = jnp.maximum(s - 1, 0)
    cj = cons % nj

    @pl.when(s == 0)
    def _():
        act_ref[...] = jnp.zeros_like(act_ref)
        lf_ref[...] = jnp.zeros_like(lf_ref)

    @pl.when(s % nj == 0)
    def _():
        h_ref[...] = _rmsnorm(x_ref[...], g_ref[...]).astype(BF16)

    @pl.when((cons // nj) % blocks_per_seq == 0)
    def _():
        ret_state_ref[cj] = jnp.zeros((dk, dk), F32)
        for a in range(heads_per_step):
            hgrn_state_ref[heads_per_step * cj + a] = jnp.zeros((dh, dh), F32)

    for i_ref, o_ref in zip(side_in, side_out):
        o_ref[...] = i_ref[...].astype(BF16)

    h = h_ref[...]
    half = dk // 2
    cos = cos_sin_ref[:, :half]
    sin = cos_sin_ref[:, half:]
    dst = act_ref.at[s % 2]
    src = act_ref.at[(s + 1) % 2]
    src_lf = lf_ref.at[(s + 1) % 2]

    def rope(p):
        x1, x2 = p[:, :half], p[:, half:]
        return jnp.concatenate([x1 * cos - x2 * sin, x2 * cos + x1 * sin], axis=-1)

    consts = _HgrnConsts(HGRN_CHUNK, HGRN_SUB)
    hq, hk, hv, hlf = src[_HQ], src[_HK], src[_HV], src_lf[...]
    heads = []
    for a in range(heads_per_step):
        cols = slice(a * dh, (a + 1) * dh)
        heads.append(_HgrnHead(hq[:, cols], hk[:, cols], hv[:, cols], hlf[:, cols],
                               cp_ref.at[a], consts))
    ret = _RetentionHead(src[_RQ], src[_RK], src[_RV], cj, RET_CHUNK)

    def project_pair(first_group):
        return _dot(h, w_ref[first_group]), _dot(h, w_ref[first_group + 1])

    def project_hq_hf():
        p_hq, z = project_pair(_HQ)
        dst[_HQ] = _silu(p_hq).astype(BF16)
        lbl = lbl_ref[...]
        m = jnp.maximum(lbl, 0.0)
        e_l = jnp.exp(lbl - m)
        lb = e_l / (e_l + jnp.exp(-m))
        dst[_HK] = ((1.0 - lb) * jax.nn.sigmoid(-z)).astype(BF16)
        lf_ref[s % 2] = jnp.log2(lb + (1.0 - lb) * jax.nn.sigmoid(z))

    def project_hv_hg():
        p_hv, p_hg = project_pair(_HV)
        dst[_HV] = p_hv.astype(BF16)
        dst[_HG] = _silu(p_hg).astype(BF16)

    def project_rq_rk():
        p_rq, p_rk = project_pair(_RQ)
        dst[_RQ] = (rope(p_rq) * (dk ** -0.5)).astype(BF16)
        dst[_RK] = rope(p_rk).astype(BF16)

    def project_rv_rg():
        p_rv, p_rg = project_pair(_RV)
        dst[_RV] = p_rv.astype(BF16)
        dst[_RG] = _silu(p_rg).astype(BF16)

    def hgrn_outputs():
        hgate = src[_HG]
        for a, hd in enumerate(heads):
            cols = slice(a * dh, (a + 1) * dh)
            outs, state = hd.outputs(hgrn_state_ref[heads_per_step * cj + a])
            hgrn_state_ref[heads_per_step * cj + a] = state
            gain = hgain_ref[:, cols]
            for rows, o in zip(hd.chunks, outs):
                y = o * lax.rsqrt(jnp.mean(o * o, axis=-1, keepdims=True) + EPS)
                hgo_ref[rows, cols] = (y * gain * hgate[rows, cols].astype(F32)).astype(BF16)

    def retention_outputs():
        rgate = src[_RG]
        outs, state = ret.outputs(ret_state_ref[cj])
        ret_state_ref[cj] = state
        gain = rgain_ref[...]
        for rows, o in zip(ret.chunks, outs):
            mu = jnp.mean(o, axis=-1, keepdims=True)
            oc = o - mu
            var = jnp.mean(oc * oc, axis=-1, keepdims=True)
            ret_ref[rows, :] = (oc * lax.rsqrt(var + EPS) * gain
                                * rgate[rows, :].astype(F32)).astype(BF16)

    for hd in heads:
        hd.cumsum_dots()
    project_hq_hf()
    for hd in heads:
        hd.elementwise()
    for hd in heads:
        hd.free_dots()
    ret.free_dots()
    project_hv_hg()
    hgrn_outputs()
    project_rq_rk()
    retention_outputs()
    project_rv_rg()


def _mixer(x1, norm_g, w_units, lb_logits, ret_gain, hgrn_gain, cos_sin, seq, side_f32):
    t, d = x1.shape
    tm = MIXER_TM
    tn = RET_HEAD_DIM
    dh = HGRN_HEAD_DIM
    width = RET_HEADS * RET_HEAD_DIM
    assert w_units.shape == (width // tn, 8, d, tn) and width == HGRN_HEADS * dh and tn % dh == 0
    assert t % tm == 0 and seq % tm == 0
    assert tm % RET_CHUNK == 0 and tm % HGRN_CHUNK == 0
    assert HGRN_CHUNK % BF16_TILE_ROWS == 0 and BF16_TILE_ROWS % HGRN_SUB == 0
    nj = width // tn
    blocks_per_seq = seq // tm
    n_units = (t // tm) * nj
    prod = lambda s: jnp.minimum(s, n_units - 1)
    cons = lambda s: jnp.maximum(s - 1, 0)
    const = lambda s: (0, 0)
    pos = lambda s: ((prod(s) // nj) % blocks_per_seq, 0)
    cons_col = lambda s: (0, cons(s) % nj)

    side_specs = []
    for w in side_f32:
        rows = next(r for r in range(BF16_TILE_ROWS, w.shape[0] + 1, BF16_TILE_ROWS)
                    if w.shape[0] % r == 0 and w.shape[0] // r <= n_units)
        side_specs.append(pl.BlockSpec(
            (rows, w.shape[1]), lambda s, last=w.shape[0] // rows - 1: (jnp.minimum(s, last), 0)))

    out_spec = pl.BlockSpec((tm, tn), lambda s: (cons(s) // nj, cons(s) % nj))
    out_bf = jax.ShapeDtypeStruct((t, width), BF16)
    outs = pl.pallas_call(
        functools.partial(_mixer_kernel, n_col_blocks=nj, blocks_per_seq=blocks_per_seq,
                          n_side=len(side_f32)),
        grid=(n_units + 1,),
        in_specs=[pl.BlockSpec((tm, d), lambda s: (prod(s) // nj, 0)), pl.BlockSpec((1, d), const),
                  pl.BlockSpec((tm, tn), pos),
                  pl.BlockSpec((1, tn), lambda s: (0, prod(s) % nj)),
                  pl.BlockSpec((1, tn), cons_col), pl.BlockSpec((1, tn), cons_col),
                  pl.BlockSpec((None, 8, d, tn), lambda s: (prod(s) % nj, 0, 0, 0))] + side_specs,
        out_specs=[out_spec, out_spec] + side_specs,
        out_shape=[out_bf, out_bf] + [jax.ShapeDtypeStruct(w.shape, BF16) for w in side_f32],
        scratch_shapes=[pltpu.VMEM((tm, d), BF16),
                        pltpu.VMEM((2, 8, tm, tn), BF16),
                        pltpu.VMEM((2, tm, tn), F32),
                        pltpu.VMEM((RET_HEADS, RET_HEAD_DIM, RET_HEAD_DIM), F32),
                        pltpu.VMEM((HGRN_HEADS, dh, dh), F32),
                        pltpu.VMEM((tn // dh, tm, dh), F32)],
        compiler_params=pltpu.CompilerParams(
            dimension_semantics=("arbitrary",), vmem_limit_bytes=VMEM_LIMIT_BYTES),
        name="mixer",
    )(x1, norm_g.reshape(1, d), cos_sin, lb_logits, ret_gain.reshape(1, width),
      hgrn_gain.reshape(1, width), w_units, *side_f32)
    return outs[0], outs[1], outs[2:]


@jax.jit
def kernel(x, ffn1_norm, ffn1_w_gate, ffn1_w_up, ffn1_w_down, mix_norm, w_in, ret_norm_g, hgrn_lb_logits, hgrn_norm_g, w_out, ffn2_norm, ffn2_w_gate, ffn2_w_up, ffn2_w_down, final_norm):
    batch, seq, d = x.shape
    assert ffn1_norm.shape[0] == 1, "single-layer stack"
    t = batch * seq
    bf = lambda w: w[0].astype(BF16)

    inv = np.power(ROPE_BASE, -np.arange(0, RET_HEAD_DIM, 2, dtype=np.float64) / RET_HEAD_DIM)
    ang = np.arange(seq, dtype=np.float64)[:, None] * inv[None, :]
    cos_sin = jnp.asarray(np.concatenate([np.cos(ang), np.sin(ang)], axis=1), dtype=F32)

    x0 = x.reshape(t, d)
    x1, w_units = _ffn(x0, ffn1_norm[0], bf(ffn1_w_gate), bf(ffn1_w_up), bf(ffn1_w_down),
                       name="ffn1", w_in_f32=w_in[0].astype(F32))
    side = [ffn2_w_gate[0].astype(F32), ffn2_w_up[0].astype(F32), ffn2_w_down[0].astype(F32),
            w_out[0].astype(F32)]
    ret, hgo, (wg2, wu2, wd2, wo) = _mixer(
        x1, mix_norm[0], w_units, hgrn_lb_logits.astype(F32),
        ret_norm_g[0].astype(F32), hgrn_norm_g[0].astype(F32), cos_sin, seq, side)
    x2 = _outproj(x1, ret, hgo, wo)
    out = _ffn(x2, ffn2_norm[0], wg2, wu2, wd2, name="ffn2", final_g=final_norm)
    return out.reshape(batch, seq, d)
```
